```python
import jax, jax.numpy as jnp
from jax import lax
import numpy as np

D_MODEL = 1024
BATCH = 16
SEQ = 2048
DEPTH = 1

CTX_LEN = 256
GRID_W = 64
D_FOURIER = 512
N_FOURIER_GROUPS = 4
FOURIER_GROUP = D_FOURIER // N_FOURIER_GROUPS
D_RET = 512
N_RET_HEADS = 4
RET_HEAD_DIM = D_RET // N_RET_HEADS
D_MIX = D_FOURIER + D_RET
IN_SPLITS = (D_FOURIER, 2 * D_FOURIER, 2 * D_FOURIER + D_RET,
             2 * D_FOURIER + 2 * D_RET, 2 * D_FOURIER + 3 * D_RET)
D_IN = 2 * D_FOURIER + 4 * D_RET
RET_CHUNK = 128
ROPE_BASE = 10000.0
QK_SCALE = RET_HEAD_DIM ** -0.5
EPS = 1e-6

kernel_name = "hymba_fnet_retnet_prefix_dit_block"


def rms_norm(x, gain):
    xf = x.astype(jnp.float32)
    y = xf * lax.rsqrt(jnp.mean(xf * xf, axis=-1, keepdims=True) + EPS)
    return (y * gain.astype(jnp.float32)).astype(x.dtype)


def adaln_params(cvec, w_ada, b_ada):
    m = jax.nn.silu(cvec) @ w_ada + b_ada
    shift, scale, gate = jnp.split(m, 3, axis=-1)
    return shift[:, None, :], scale[:, None, :], gate[:, None, :]


def modulate(h, shift, scale):
    return h * (1.0 + scale) + shift


def split_heads(t):
    return t.reshape(t.shape[0], t.shape[1], N_RET_HEADS, RET_HEAD_DIM)


def axial_rotary(t, row, col):
    half = RET_HEAD_DIM // 2
    n_freq = half // 2
    inv_freq = ROPE_BASE ** (-jnp.arange(n_freq, dtype=jnp.float32) / n_freq)

    def rot(u, pos):
        ang = pos.astype(jnp.float32)[:, None] * inv_freq[None, :]
        cos = jnp.cos(ang)[None, :, None, :]
        sin = jnp.sin(ang)[None, :, None, :]
        u1, u2 = u[..., :n_freq].astype(jnp.float32), u[..., n_freq:].astype(jnp.float32)
        return jnp.concatenate([u1 * cos - u2 * sin, u1 * sin + u2 * cos], axis=-1)

    out = jnp.concatenate([rot(t[..., :half], row), rot(t[..., half:], col)], axis=-1)
    return out.astype(t.dtype)


def retention_scan(q, k, v, log_gamma, init_state, inclusive):
    bsz, length, heads, dh = q.shape
    n_chunks = length // RET_CHUNK

    def to_chunks(t):
        return t.reshape(bsz, n_chunks, RET_CHUNK, heads, dh).transpose(1, 0, 3, 2, 4)

    idx = jnp.arange(RET_CHUNK, dtype=jnp.float32)
    diff = idx[:, None] - idx[None, :]
    mask = (diff >= 0) if inclusive else (diff > 0)
    decay_mat = jnp.where(mask[None], jnp.exp(log_gamma[:, None, None] * jnp.where(mask, diff, 0.0)[None]), 0.0)
    q_decay = jnp.exp(log_gamma[:, None] * (idx[None, :] + 1.0))
    k_decay = jnp.exp(log_gamma[:, None] * (RET_CHUNK - 1.0 - idx[None, :]))
    chunk_decay = jnp.exp(log_gamma * RET_CHUNK)

    def step(state, qkv):
        qc, kc, vc = [t.astype(jnp.float32) for t in qkv]
        scores = jnp.einsum('bhnd,bhmd->bhnm', qc, kc) * decay_mat[None]
        inner = jnp.einsum('bhnm,bhme->bhne', scores, vc)
        cross = jnp.einsum('bhnd,bhde->bhne', qc * q_decay[None, :, :, None], state)
        new_state = chunk_decay[None, :, None, None] * state + jnp.einsum(
            'bhmd,bhme->bhde', kc * k_decay[None, :, :, None], vc)
        return new_state, inner + cross

    final_state, out = lax.scan(step, init_state, (to_chunks(q), to_chunks(k), to_chunks(v)))
    out = out.transpose(1, 0, 3, 2, 4).reshape(bsz, length, heads, dh)
    return out, final_state


def bidirectional_retention(q, k, v, decay_logit, init_fwd, init_bwd):
    log_g = jax.nn.log_sigmoid(decay_logit.astype(jnp.float32))
    out_f, fin_f = retention_scan(q, k, v, log_g[0], init_fwd, inclusive=True)
    out_b, fin_b = retention_scan(q[:, ::-1], k[:, ::-1], v[:, ::-1], log_g[1], init_bwd, inclusive=False)
    return out_f + out_b[:, ::-1], fin_f, fin_b


def gated_retention(o, r_gate, gn_gain):
    mu = jnp.mean(o, axis=-1, keepdims=True)
    var = jnp.mean(jnp.square(o - mu), axis=-1, keepdims=True)
    o = ((o - mu) * lax.rsqrt(var + EPS)).reshape(o.shape[0], o.shape[1], D_RET)
    o = o * gn_gain.astype(jnp.float32) * jax.nn.silu(r_gate.astype(jnp.float32))
    return o.astype(r_gate.dtype)


def fourier_branch(f_in, f_gate, w_fourier):
    bsz, length, _ = f_in.shape
    u = f_in.reshape(bsz, length, N_FOURIER_GROUPS, FOURIER_GROUP).astype(jnp.float32)
    mixed = jnp.real(jnp.fft.fft2(u, axes=(1, 3), norm='ortho')).astype(f_in.dtype)
    y = jnp.einsum('blgc,gcd->blgd', mixed, w_fourier).reshape(bsz, length, D_FOURIER)
    return y * jax.nn.silu(f_gate)


def setup_inputs(seed: int = 0) -> dict:
    key = jax.random.key(seed)
    ks = jax.random.split(key, 16)
    n = jax.random.normal
    base_gamma = 1.0 - 2.0 ** (-5.0 - np.arange(N_RET_HEADS, dtype=np.float32))
    base_logit = jnp.asarray(np.log(base_gamma / (1.0 - base_gamma)), jnp.float32)
    return {
        "x": n(ks[0], (BATCH, SEQ, D_MODEL), jnp.float32),
        "c": n(ks[1], (BATCH, D_MODEL), jnp.float32),
        "ctx": n(ks[2], (BATCH, CTX_LEN, D_MODEL), jnp.float32),
        "c_ctx": n(ks[3], (D_MODEL,), jnp.float32),
        "w_ada": n(ks[4], (DEPTH, D_MODEL, 3 * D_MODEL), jnp.float32) * (0.5 * D_MODEL ** -0.5),
        "b_ada": n(ks[5], (DEPTH, 3 * D_MODEL), jnp.float32) * 0.01,
        "g_pre": 1.0 + 0.02 * n(ks[6], (DEPTH, D_MODEL), jnp.float32),
        "g_post": 1.0 + 0.02 * n(ks[7], (DEPTH, D_MODEL), jnp.float32),
        "w_in": n(ks[8], (DEPTH, D_MODEL, D_IN), jnp.float32) * D_MODEL ** -0.5,
        "w_fourier": n(ks[9], (DEPTH, N_FOURIER_GROUPS, FOURIER_GROUP, FOURIER_GROUP), jnp.float32) * FOURIER_GROUP ** -0.5,
        "decay_logit": base_logit[None, None, :] + 0.05 * n(ks[10], (DEPTH, 2, N_RET_HEADS), jnp.float32),
        "ret_gn_gain": 1.0 + 0.02 * n(ks[11], (DEPTH, D_RET), jnp.float32),
        "w_out": n(ks[12], (DEPTH, D_MIX, D_MODEL), jnp.float32) * D_MIX ** -0.5,
    }


def reference(x, c, ctx, c_ctx, w_ada, b_ada, g_pre, g_post, w_in, w_fourier, decay_logit, ret_gn_gain, w_out):
    bsz, seq_len, _ = x.shape
    rows = seq_len // GRID_W
    row = jnp.repeat(jnp.arange(rows, dtype=jnp.int32), GRID_W)
    col = jnp.tile(jnp.arange(GRID_W, dtype=jnp.int32), rows)
    zero_state = jnp.zeros((bsz, N_RET_HEADS, RET_HEAD_DIM, RET_HEAD_DIM), jnp.float32)

    for layer in range(DEPTH):
        shift, scale, gate = adaln_params(c, w_ada[layer], b_ada[layer])
        shift_c, scale_c, gate_c = adaln_params(c_ctx[None, :], w_ada[layer], b_ada[layer])

        h_c = modulate(rms_norm(ctx, g_pre[layer]), shift_c, scale_c)
        fc_in, fc_gate, qc, kc, vc, rc_gate = jnp.split(h_c @ w_in[layer], IN_SPLITS, axis=-1)
        ret_c, st_fwd, st_bwd = bidirectional_retention(
            split_heads(qc) * QK_SCALE, split_heads(kc), split_heads(vc),
            decay_logit[layer], zero_state, zero_state)

        h = modulate(rms_norm(x, g_pre[layer]), shift, scale)
        f_in, f_gate, q, k, v, r_gate = jnp.split(h @ w_in[layer], IN_SPLITS, axis=-1)
        q = axial_rotary(split_heads(q), row, col) * QK_SCALE
        k = axial_rotary(split_heads(k), row, col)
        ret, _, _ = bidirectional_retention(q, k, split_heads(v), decay_logit[layer], st_fwd, st_bwd)
        y = jnp.concatenate([fourier_branch(f_in, f_gate, w_fourier[layer]),
                             gated_retention(ret, r_gate, ret_gn_gain[layer])], axis=-1) @ w_out[layer]
        x_next = x + gate * rms_norm(y, g_post[layer])

        if layer < DEPTH - 1:
            y_c = jnp.concatenate([fourier_branch(fc_in, fc_gate, w_fourier[layer]),
                                   gated_retention(ret_c, rc_gate, ret_gn_gain[layer])], axis=-1) @ w_out[layer]
            ctx = ctx + gate_c * rms_norm(y_c, g_post[layer])
        x = x_next

    return x
```

```python
import functools

import numpy as np
import jax
import jax.numpy as jnp
from jax import lax
from jax.experimental import pallas as pl
from jax.experimental.pallas import tpu as pltpu

D_MODEL = 1024
GRID_W = 64
D_FOURIER = 512
N_FOURIER_GROUPS = 4
FOURIER_GROUP = D_FOURIER // N_FOURIER_GROUPS
D_RET = 512
N_RET_HEADS = 4
RET_HEAD_DIM = D_RET // N_RET_HEADS
D_MIX = D_FOURIER + D_RET
D_IN = 2 * D_FOURIER + 4 * D_RET
RET_CHUNK = 128
ROPE_BASE = 10000.0
QK_SCALE = RET_HEAD_DIM ** -0.5
EPS = 1e-6

COL_F_IN = 0
COL_F_GATE = D_FOURIER
COL_Q = 2 * D_FOURIER
COL_K = COL_Q + D_RET
COL_V = COL_K + D_RET
COL_R_GATE = COL_V + D_RET

ADA_ROWS = 24
TOKEN_TILE = 512
VMEM_LIMIT = 56 * 1024 * 1024

F32 = jnp.float32
BF16 = jnp.bfloat16


def _silu(v):
    return v * jax.nn.sigmoid(v)


def _adaln_kernel(cv_ref, w_ref, b_ref, o_ref):
    s = _silu(cv_ref[...])
    o_ref[...] = jnp.dot(s, w_ref[...], preferred_element_type=F32) + b_ref[...]


def _adaln(cvec, w_ada, b_ada):
    n_out = w_ada.shape[1]
    bn = 512
    return pl.pallas_call(
        _adaln_kernel,
        grid=(n_out // bn,),
        in_specs=[
            pl.BlockSpec((ADA_ROWS, D_MODEL), lambda j: (0, 0)),
            pl.BlockSpec((D_MODEL, bn), lambda j: (0, j)),
            pl.BlockSpec((1, bn), lambda j: (0, j)),
        ],
        out_specs=pl.BlockSpec((ADA_ROWS, bn), lambda j: (0, j)),
        out_shape=jax.ShapeDtypeStruct((ADA_ROWS, n_out), F32),
        name="adaln",
    )(cvec, w_ada, b_ada.reshape(1, n_out))


def _tables_kernel(dl_ref, wf_ref, cc_ref, sc_ref, dm_ref, rd_ref, cd_ref, ab_ref):
    c = RET_CHUNK
    n = lax.broadcasted_iota(jnp.int32, (c, c), 0).astype(F32)
    m = lax.broadcasted_iota(jnp.int32, (c, c), 1).astype(F32)
    diff = n - m
    def log_sigmoid(v):
        return jnp.minimum(v, 0.0) - jnp.log1p(jnp.exp(-jnp.abs(v)))

    for h in range(N_RET_HEADS):
        lg_f = jnp.broadcast_to(log_sigmoid(dl_ref[0, h])[0:1, :], (c, c))
        lg_b = jnp.broadcast_to(log_sigmoid(dl_ref[1, h])[0:1, :], (c, c))
        dm_ref[h] = QK_SCALE * jnp.where(diff >= 0, jnp.exp(lg_f * jnp.maximum(diff, 0.0)),
                                         jnp.exp(lg_b * jnp.maximum(-diff, 0.0)))
        rd_ref[h, 0] = QK_SCALE * jnp.exp(lg_f * (n + 1.0))
        rd_ref[h, 1] = QK_SCALE * jnp.exp(lg_b * (c - n))
        rd_ref[h, 2] = jnp.exp(lg_f * (c - 1.0 - n))
        rd_ref[h, 3] = jnp.exp(lg_b * n)
        cd_ref[h, 0] = jnp.exp(lg_f[0:8, :] * float(c))
        cd_ref[h, 1] = jnp.exp(lg_b[0:8, :] * float(c))
    for g in range(N_FOURIER_GROUPS):
        wf = wf_ref[g]
        a = jnp.dot(cc_ref[...], wf, preferred_element_type=F32, precision=lax.Precision.HIGHEST)
        b = jnp.dot(sc_ref[...], wf, preferred_element_type=F32, precision=lax.Precision.HIGHEST)
        ab_ref[g, :, 0:FOURIER_GROUP] = a.astype(BF16)
        ab_ref[g, :, FOURIER_GROUP:2 * FOURIER_GROUP] = b.astype(BF16)


def _tables(decay_logit, w_fourier):
    c = RET_CHUNK
    dl = jnp.broadcast_to(decay_logit[:, :, None, None], (2, N_RET_HEADS, 8, 128))
    idx = np.arange(FOURIER_GROUP)
    ang = 2.0 * np.pi * ((idx[:, None] * idx[None, :]) % FOURIER_GROUP) / FOURIER_GROUP
    cc = jnp.asarray(np.cos(ang) / np.sqrt(FOURIER_GROUP), F32)
    sc = jnp.asarray(np.sin(ang) / np.sqrt(FOURIER_GROUP), F32)
    return pl.pallas_call(
        _tables_kernel,
        out_shape=(
            jax.ShapeDtypeStruct((N_RET_HEADS, c, c), F32),
            jax.ShapeDtypeStruct((N_RET_HEADS, 4, c, c), F32),
            jax.ShapeDtypeStruct((N_RET_HEADS, 2, 8, 128), F32),
            jax.ShapeDtypeStruct((N_FOURIER_GROUPS, FOURIER_GROUP, 2 * FOURIER_GROUP), BF16),
        ),
        name="tables",
    )(dl, w_fourier, cc, sc)


def _inproj_kernel(x_ref, shift_ref, scale_ref, g_ref, w_ref, cos_ref, sin_ref, o_ref, *, rot_lo, rot_hi):
    x = x_ref[0]
    ms = jnp.mean(x * x, axis=-1, keepdims=True)
    h = x * lax.rsqrt(ms + EPS) * g_ref[...]
    h = h * (1.0 + scale_ref[0]) + shift_ref[0]
    hb = h.astype(BF16)
    n_out = w_ref.shape[1]
    hd = RET_HEAD_DIM
    for j in range(n_out // hd):
        p = jnp.dot(hb, w_ref[:, j * hd:(j + 1) * hd], preferred_element_type=F32)
        if rot_lo <= j * hd < rot_hi:
            p = p * cos_ref[...] + pltpu.roll(p, hd // 2, axis=1) * sin_ref[...]
        o_ref[0, :, j * hd:(j + 1) * hd] = p.astype(BF16)


def _inproj(x, shift, scale, g_pre, w, cos_t, sin_t, *, rot_lo, rot_hi, tm):
    bsz, length, _ = x.shape
    n_out = w.shape[1]
    per_batch = shift.shape[0] > 1
    mod_map = (lambda b, t: (b, 0, 0)) if per_batch else (lambda b, t: (0, 0, 0))
    return pl.pallas_call(
        functools.partial(_inproj_kernel, rot_lo=rot_lo, rot_hi=rot_hi),
        grid=(bsz, length // tm),
        in_specs=[
            pl.BlockSpec((1, tm, D_MODEL), lambda b, t: (b, t, 0)),
            pl.BlockSpec((1, 1, D_MODEL), mod_map),
            pl.BlockSpec((1, 1, D_MODEL), mod_map),
            pl.BlockSpec((1, D_MODEL), lambda b, t: (0, 0)),
            pl.BlockSpec((D_MODEL, n_out), lambda b, t: (0, 0)),
            pl.BlockSpec((tm, RET_HEAD_DIM), lambda b, t: (t, 0)),
            pl.BlockSpec((tm, RET_HEAD_DIM), lambda b, t: (t, 0)),
        ],
        out_specs=pl.BlockSpec((1, tm, n_out), lambda b, t: (b, t, 0)),
        out_shape=jax.ShapeDtypeStruct((bsz, length, n_out), BF16),
        compiler_params=pltpu.CompilerParams(
            dimension_semantics=("parallel", "parallel"), vmem_limit_bytes=VMEM_LIMIT),
        name="inproj",
    )(x, shift, scale, g_pre, w, cos_t, sin_t)


def _rotary_tables(seq_len):
    n_freq = RET_HEAD_DIM // 4
    pos = np.arange(seq_len)
    row = (pos // GRID_W).astype(np.float64)
    col = (pos % GRID_W).astype(np.float64)
    inv_freq = ROPE_BASE ** (-np.arange(n_freq, dtype=np.float64) / n_freq)
    ang_r = row[:, None] * inv_freq[None, :]
    ang_c = col[:, None] * inv_freq[None, :]
    cos_t = np.concatenate([np.cos(ang_r), np.cos(ang_c), np.cos(ang_r), np.cos(ang_c)], axis=1)
    sin_t = np.concatenate([-np.sin(ang_r), -np.sin(ang_c), np.sin(ang_r), np.sin(ang_c)], axis=1)
    return jnp.asarray(cos_t, F32), jnp.asarray(sin_t, F32)


def _permute_head_columns(w):
    d = w.shape[0]
    n_freq = RET_HEAD_DIM // 4
    w = w.reshape(d, N_RET_HEADS, 2, 2, n_freq)
    return w.transpose(0, 1, 3, 2, 4).reshape(d, N_RET_HEADS * RET_HEAD_DIM)


def _fourier_kernel(fin_ref, fg_ref, ab_ref, cs_ref, o_ref, pq_ref):
    length = fin_ref.shape[1]
    fg = FOURIER_GROUP

    @pl.when(pl.program_id(1) == 0)
    def _():
        for g in range(N_FOURIER_GROUPS):
            u = fin_ref[0, :, g * fg:(g + 1) * fg]
            pq = jnp.dot(u, ab_ref[g], preferred_element_type=F32)
            pq_ref[0:length, g * fg:(g + 1) * fg] = pq[:, 0:fg].astype(BF16)
            pq_ref[length:2 * length, g * fg:(g + 1) * fg] = pq[:, fg:2 * fg].astype(BF16)

    y = jnp.dot(cs_ref[...], pq_ref[...], preferred_element_type=F32)
    o_ref[0] = (y * _silu(fg_ref[0].astype(F32))).astype(BF16)


def _fourier(proj, ab, cs, *, tm):
    bsz, length, _ = proj.shape
    return pl.pallas_call(
        _fourier_kernel,
        grid=(bsz, length // tm),
        in_specs=[
            pl.BlockSpec((1, length, D_FOURIER), lambda b, t: (b, 0, COL_F_IN // D_FOURIER)),
            pl.BlockSpec((1, tm, D_FOURIER), lambda b, t: (b, t, COL_F_GATE // D_FOURIER)),
            pl.BlockSpec((N_FOURIER_GROUPS, FOURIER_GROUP, 2 * FOURIER_GROUP), lambda b, t: (0, 0, 0)),
            pl.BlockSpec((tm, 2 * length), lambda b, t: (t, 0)),
        ],
        out_specs=pl.BlockSpec((1, tm, D_FOURIER), lambda b, t: (b, t, 0)),
        out_shape=jax.ShapeDtypeStruct((bsz, length, D_FOURIER), BF16),
        scratch_shapes=[pltpu.VMEM((2 * length, D_FOURIER), BF16)],
        compiler_params=pltpu.CompilerParams(
            dimension_semantics=("parallel", "arbitrary"), vmem_limit_bytes=VMEM_LIMIT),
        name="fourier",
    )(proj, proj, ab, cs)


def _seq_dft_matrix(length):
    idx = jnp.arange(length, dtype=jnp.int32)
    ang = ((idx[:, None] * idx[None, :]) % length).astype(F32) * (2.0 * np.pi / length)
    scale = 1.0 / np.sqrt(length)
    return jnp.concatenate([jnp.cos(ang) * scale, jnp.sin(ang) * (-scale)], axis=1).astype(BF16)


def _ret_kernel(q_ref, k_ref, v_ref, rg_ref, qc_ref, kc_ref, vc_ref, dm_ref, rd_ref, cd_ref, gain_ref,
                o_ref, sb_ref):
    c = RET_CHUNK
    n_chunks = q_ref.shape[1] // c
    n_ctx_chunks = qc_ref.shape[1] // c
    dm = dm_ref[0]
    q_dec_f, q_dec_b, k_dec_f, k_dec_b = rd_ref[0, 0], rd_ref[0, 1], rd_ref[0, 2], rd_ref[0, 3]
    c_dec_f = cd_ref[0, 0, 0:1, :]
    c_dec_b = cd_ref[0, 1, 0:1, :]
    gain = gain_ref[...]

    def kv_update(state, k, v, k_dec, c_dec):
        kd = (k.astype(F32) * k_dec).astype(BF16)
        return c_dec * state + lax.dot_general(kd, v, (((0,), (0,)), ((), ())), preferred_element_type=F32)

    sb = jnp.zeros((c, c), F32)
    for j in reversed(range(n_ctx_chunks)):
        sb = kv_update(sb, kc_ref[0, j * c:(j + 1) * c, :], vc_ref[0, j * c:(j + 1) * c, :], k_dec_b, c_dec_b)

    def bwd_body(i, sb):
        j = n_chunks - 1 - i
        rows = pl.ds(pl.multiple_of(j * c, c), c)
        sb_ref[j] = sb.astype(BF16)
        return kv_update(sb, k_ref[0, rows, :], v_ref[0, rows, :], k_dec_b, c_dec_b)

    lax.fori_loop(0, n_chunks, bwd_body, sb)

    sf = jnp.zeros((c, c), F32)
    for j in range(n_ctx_chunks):
        sf = kv_update(sf, kc_ref[0, j * c:(j + 1) * c, :], vc_ref[0, j * c:(j + 1) * c, :], k_dec_f, c_dec_f)

    def fwd_body(j, sf):
        rows = pl.ds(pl.multiple_of(j * c, c), c)
        q = q_ref[0, rows, :]
        k = k_ref[0, rows, :]
        v = v_ref[0, rows, :]
        s = lax.dot_general(q, k, (((1,), (1,)), ((), ())), preferred_element_type=F32)
        a = (s * dm).astype(BF16)
        qf = q.astype(F32)
        o = jnp.dot(a, v, preferred_element_type=F32)
        o += jnp.dot((qf * q_dec_f).astype(BF16), sf.astype(BF16), preferred_element_type=F32)
        o += jnp.dot((qf * q_dec_b).astype(BF16), sb_ref[j], preferred_element_type=F32)
        mu = jnp.mean(o, axis=-1, keepdims=True)
        d = o - mu
        var = jnp.mean(d * d, axis=-1, keepdims=True)
        on = d * lax.rsqrt(var + EPS)
        o_ref[0, rows, :] = (on * gain * _silu(rg_ref[0, rows, :].astype(F32))).astype(BF16)
        return kv_update(sf, k, v, k_dec_f, c_dec_f)

    lax.fori_loop(0, n_chunks, fwd_body, sf)


def _retention(proj, proj_c, dm, rd, cd, gain):
    bsz, length, _ = proj.shape
    ctx_len = proj_c.shape[1]
    hd = RET_HEAD_DIM
    c = RET_CHUNK

    def col(base):
        return lambda b, h: (b, 0, base // hd + h)

    return pl.pallas_call(
        _ret_kernel,
        grid=(bsz, N_RET_HEADS),
        in_specs=[
            pl.BlockSpec((1, length, hd), col(COL_Q)),
            pl.BlockSpec((1, length, hd), col(COL_K)),
            pl.BlockSpec((1, length, hd), col(COL_V)),
            pl.BlockSpec((1, length, hd), col(COL_R_GATE)),
            pl.BlockSpec((1, ctx_len, hd), col(0)),
            pl.BlockSpec((1, ctx_len, hd), col(D_RET)),
            pl.BlockSpec((1, ctx_len, hd), col(2 * D_RET)),
            pl.BlockSpec((1, c, c), lambda b, h: (h, 0, 0)),
            pl.BlockSpec((1, 4, c, c), lambda b, h: (h, 0, 0, 0)),
            pl.BlockSpec((1, 2, 8, 128), lambda b, h: (h, 0, 0, 0)),
            pl.BlockSpec((1, hd), lambda b, h: (0, h)),
        ],
        out_specs=pl.BlockSpec((1, length, hd), lambda b, h: (b, 0, h)),
        out_shape=jax.ShapeDtypeStruct((bsz, length, D_RET), BF16),
        scratch_shapes=[pltpu.VMEM((length // c, c, c), BF16)],
        compiler_params=pltpu.CompilerParams(
            dimension_semantics=("parallel", "parallel"), vmem_limit_bytes=VMEM_LIMIT),
        name="retention",
    )(proj, proj, proj, proj, proj_c, proj_c, proj_c, dm, rd, cd, gain)


def _outproj_kernel(yf_ref, yr_ref, w_ref, x_ref, gate_ref, g_ref, o_ref):
    y = jnp.dot(yf_ref[0], w_ref[0:D_FOURIER, :], preferred_element_type=F32)
    y += jnp.dot(yr_ref[0], w_ref[D_FOURIER:D_MIX, :], preferred_element_type=F32)
    ms = jnp.mean(y * y, axis=-1, keepdims=True)
    yn = y * lax.rsqrt(ms + EPS) * g_ref[...]
    o_ref[0] = x_ref[0] + gate_ref[0] * yn


def _outproj(yf, yr, w_out, x, gate, g_post, *, tm):
    bsz, length, _ = x.shape
    return pl.pallas_call(
        _outproj_kernel,
        grid=(bsz, length // tm),
        in_specs=[
            pl.BlockSpec((1, tm, D_FOURIER), lambda b, t: (b, t, 0)),
            pl.BlockSpec((1, tm, D_RET), lambda b, t: (b, t, 0)),
            pl.BlockSpec((D_MIX, D_MODEL), lambda b, t: (0, 0)),
            pl.BlockSpec((1, tm, D_MODEL), lambda b, t: (b, t, 0)),
            pl.BlockSpec((1, 1, D_MODEL), lambda b, t: (b, 0, 0)),
            pl.BlockSpec((1, D_MODEL), lambda b, t: (0, 0)),
        ],
        out_specs=pl.BlockSpec((1, tm, D_MODEL), lambda b, t: (b, t, 0)),
        out_shape=jax.ShapeDtypeStruct((bsz, length, D_MODEL), F32),
        compiler_params=pltpu.CompilerParams(
            dimension_semantics=("parallel", "parallel"), vmem_limit_bytes=VMEM_LIMIT),
        name="outproj",
    )(yf, yr, w_out, x, gate, g_post)


def kernel(x, c, ctx, c_ctx, w_ada, b_ada, g_pre, g_post, w_in, w_fourier, decay_logit, ret_gn_gain, w_out):
    bsz, seq_len, _ = x.shape
    ctx_len = ctx.shape[1]
    assert w_ada.shape[0] == 1, "single layer only"
    assert bsz + 1 <= ADA_ROWS

    cvec = jnp.concatenate([c, c_ctx[None, :], jnp.zeros((ADA_ROWS - bsz - 1, D_MODEL), F32)], axis=0)
    mod = _adaln(cvec, w_ada[0], b_ada[0])
    shift = mod[:bsz, 0:D_MODEL].reshape(bsz, 1, D_MODEL)
    scale = mod[:bsz, D_MODEL:2 * D_MODEL].reshape(bsz, 1, D_MODEL)
    gate = mod[:bsz, 2 * D_MODEL:].reshape(bsz, 1, D_MODEL)
    shift_c = mod[bsz:bsz + 1, 0:D_MODEL].reshape(1, 1, D_MODEL)
    scale_c = mod[bsz:bsz + 1, D_MODEL:2 * D_MODEL].reshape(1, 1, D_MODEL)

    dm, rd, cd, ab = _tables(decay_logit[0], w_fourier[0])

    w = w_in[0]
    w_perm = jnp.concatenate([
        w[:, :COL_Q],
        _permute_head_columns(w[:, COL_Q:COL_K]),
        _permute_head_columns(w[:, COL_K:COL_V]),
        w[:, COL_V:],
    ], axis=1).astype(BF16)
    cos_t, sin_t = _rotary_tables(seq_len)
    g_pre2 = g_pre[0].reshape(1, D_MODEL)

    proj = _inproj(x, shift, scale, g_pre2, w_perm, cos_t, sin_t,
                   rot_lo=COL_Q, rot_hi=COL_V, tm=TOKEN_TILE)
    proj_c = _inproj(ctx, shift_c, scale_c, g_pre2, w_perm[:, COL_Q:COL_R_GATE], cos_t, sin_t,
                     rot_lo=0, rot_hi=0, tm=ctx_len)

    cs = _seq_dft_matrix(seq_len)
    yf = _fourier(proj, ab, cs, tm=TOKEN_TILE)
    yr = _retention(proj, proj_c, dm, rd, cd, ret_gn_gain[0].reshape(1, D_RET))
    return _outproj(yf, yr, w_out[0].astype(BF16), x, gate, g_post[0].reshape(1, D_MODEL), tm=TOKEN_TILE)
```

```python
import functools

import numpy as np
import jax
import jax.numpy as jnp
from jax import lax
from jax.experimental import pallas as pl
from jax.experimental.pallas import tpu as pltpu

D_MODEL = 1024
GRID_W = 64
D_FOURIER = 512
N_FOURIER_GROUPS = 4
FOURIER_GROUP = D_FOURIER // N_FOURIER_GROUPS
D_RET = 512
N_RET_HEADS = 4
RET_HEAD_DIM = D_RET // N_RET_HEADS
D_MIX = D_FOURIER + D_RET
D_IN = 2 * D_FOURIER + 4 * D_RET
RET_CHUNK = 128
ROPE_BASE = 10000.0
QK_SCALE = RET_HEAD_DIM ** -0.5
EPS = 1e-6

COL_F_IN = 0
COL_F_GATE = D_FOURIER
COL_Q = 2 * D_FOURIER
COL_K = COL_Q + D_RET
COL_V = COL_K + D_RET
COL_R_GATE = COL_V + D_RET

MXU_COLS = 256
ADA_ROWS = 24
TOKEN_TILE = 512
VMEM_LIMIT = 56 * 1024 * 1024

F32 = jnp.float32
BF16 = jnp.bfloat16


def _silu(v):
    return v * jax.nn.sigmoid(v)


def _adaln_kernel(cv_ref, w_ref, b_ref, o_ref):
    s = _silu(cv_ref[...])
    o_ref[...] = jnp.dot(s, w_ref[...], preferred_element_type=F32) + b_ref[...]


def _adaln(cvec, w_ada, b_ada):
    n_out = w_ada.shape[1]
    bn = 512
    return pl.pallas_call(
        _adaln_kernel,
        grid=(n_out // bn,),
        in_specs=[
            pl.BlockSpec((ADA_ROWS, D_MODEL), lambda j: (0, 0)),
            pl.BlockSpec((D_MODEL, bn), lambda j: (0, j)),
            pl.BlockSpec((1, bn), lambda j: (0, j)),
        ],
        out_specs=pl.BlockSpec((ADA_ROWS, bn), lambda j: (0, j)),
        out_shape=jax.ShapeDtypeStruct((ADA_ROWS, n_out), F32),
        name="adaln",
    )(cvec, w_ada, b_ada.reshape(1, n_out))


def _tables_kernel(dl_ref, wf_ref, cc_ref, sc_ref, dm_ref, rd_ref, cd_ref, ab_ref):
    c = RET_CHUNK
    n = lax.broadcasted_iota(jnp.int32, (c, c), 0).astype(F32)
    m = lax.broadcasted_iota(jnp.int32, (c, c), 1).astype(F32)
    diff = n - m
    def log_sigmoid(v):
        return jnp.minimum(v, 0.0) - jnp.log1p(jnp.exp(-jnp.abs(v)))

    for h in range(N_RET_HEADS):
        lg_f = jnp.broadcast_to(log_sigmoid(dl_ref[0, h])[0:1, :], (c, c))
        lg_b = jnp.broadcast_to(log_sigmoid(dl_ref[1, h])[0:1, :], (c, c))
        dm_ref[h] = QK_SCALE * jnp.where(diff >= 0, jnp.exp(lg_f * jnp.maximum(diff, 0.0)),
                                         jnp.exp(lg_b * jnp.maximum(-diff, 0.0)))
        rd_ref[h, 0] = QK_SCALE * jnp.exp(lg_f * (n + 1.0))
        rd_ref[h, 1] = QK_SCALE * jnp.exp(lg_b * (c - n))
        rd_ref[h, 2] = jnp.exp(lg_f * (c - 1.0 - n))
        rd_ref[h, 3] = jnp.exp(lg_b * n)
        cd_ref[h, 0] = jnp.exp(lg_f[0:8, :] * float(c))
        cd_ref[h, 1] = jnp.exp(lg_b[0:8, :] * float(c))
    for g in range(N_FOURIER_GROUPS):
        wf = wf_ref[g]
        a = jnp.dot(cc_ref[...], wf, preferred_element_type=F32, precision=lax.Precision.HIGHEST)
        b = jnp.dot(sc_ref[...], wf, preferred_element_type=F32, precision=lax.Precision.HIGHEST)
        ab_ref[g, :, 0:FOURIER_GROUP] = a.astype(BF16)
        ab_ref[g, :, FOURIER_GROUP:2 * FOURIER_GROUP] = b.astype(BF16)


def _tables(decay_logit, w_fourier):
    c = RET_CHUNK
    dl = jnp.broadcast_to(decay_logit[:, :, None, None], (2, N_RET_HEADS, 8, 128))
    idx = np.arange(FOURIER_GROUP)
    ang = 2.0 * np.pi * ((idx[:, None] * idx[None, :]) % FOURIER_GROUP) / FOURIER_GROUP
    cc = jnp.asarray(np.cos(ang) / np.sqrt(FOURIER_GROUP), F32)
    sc = jnp.asarray(np.sin(ang) / np.sqrt(FOURIER_GROUP), F32)
    return pl.pallas_call(
        _tables_kernel,
        out_shape=(
            jax.ShapeDtypeStruct((N_RET_HEADS, c, c), F32),
            jax.ShapeDtypeStruct((N_RET_HEADS, 4, c, c), F32),
            jax.ShapeDtypeStruct((N_RET_HEADS, 2, 8, 128), F32),
            jax.ShapeDtypeStruct((N_FOURIER_GROUPS, FOURIER_GROUP, 2 * FOURIER_GROUP), BF16),
        ),
        name="tables",
    )(dl, w_fourier, cc, sc)


def _inproj_kernel(x_ref, shift_ref, scale_ref, g_ref, w_ref, cos_ref, sin_ref, o_ref, *, rot_lo, rot_hi):
    x = x_ref[0]
    ms = jnp.mean(x * x, axis=-1, keepdims=True)
    h = x * lax.rsqrt(ms + EPS) * g_ref[...]
    h = h * (1.0 + scale_ref[0]) + shift_ref[0]
    hb = h.astype(BF16)
    n_out = w_ref.shape[1]
    hd = RET_HEAD_DIM
    bn = MXU_COLS
    for j in range(n_out // bn):
        p = jnp.dot(hb, w_ref[:, j * bn:(j + 1) * bn], preferred_element_type=F32)
        for i in range(bn // hd):
            ph = p[:, i * hd:(i + 1) * hd]
            lo = j * bn + i * hd
            if rot_lo <= lo < rot_hi:
                ph = ph * cos_ref[...] + pltpu.roll(ph, hd // 2, axis=1) * sin_ref[...]
            o_ref[0, :, lo:lo + hd] = ph.astype(BF16)


def _inproj(x, shift, scale, g_pre, w, cos_t, sin_t, *, rot_lo, rot_hi, tm):
    bsz, length, _ = x.shape
    n_out = w.shape[1]
    per_batch = shift.shape[0] > 1
    mod_map = (lambda b, t: (b, 0, 0)) if per_batch else (lambda b, t: (0, 0, 0))
    return pl.pallas_call(
        functools.partial(_inproj_kernel, rot_lo=rot_lo, rot_hi=rot_hi),
        grid=(bsz, length // tm),
        in_specs=[
            pl.BlockSpec((1, tm, D_MODEL), lambda b, t: (b, t, 0)),
            pl.BlockSpec((1, 1, D_MODEL), mod_map),
            pl.BlockSpec((1, 1, D_MODEL), mod_map),
            pl.BlockSpec((1, D_MODEL), lambda b, t: (0, 0)),
            pl.BlockSpec((D_MODEL, n_out), lambda b, t: (0, 0)),
            pl.BlockSpec((tm, RET_HEAD_DIM), lambda b, t: (t, 0)),
            pl.BlockSpec((tm, RET_HEAD_DIM), lambda b, t: (t, 0)),
        ],
        out_specs=pl.BlockSpec((1, tm, n_out), lambda b, t: (b, t, 0)),
        out_shape=jax.ShapeDtypeStruct((bsz, length, n_out), BF16),
        compiler_params=pltpu.CompilerParams(
            dimension_semantics=("parallel", "parallel"), vmem_limit_bytes=VMEM_LIMIT),
        name="inproj",
    )(x, shift, scale, g_pre, w, cos_t, sin_t)


def _rotary_tables(seq_len):
    n_freq = RET_HEAD_DIM // 4
    pos = np.arange(seq_len)
    row = (pos // GRID_W).astype(np.float64)
    col = (pos % GRID_W).astype(np.float64)
    inv_freq = ROPE_BASE ** (-np.arange(n_freq, dtype=np.float64) / n_freq)
    ang_r = row[:, None] * inv_freq[None, :]
    ang_c = col[:, None] * inv_freq[None, :]
    cos_t = np.concatenate([np.cos(ang_r), np.cos(ang_c), np.cos(ang_r), np.cos(ang_c)], axis=1)
    sin_t = np.concatenate([-np.sin(ang_r), -np.sin(ang_c), np.sin(ang_r), np.sin(ang_c)], axis=1)
    return jnp.asarray(cos_t, F32), jnp.asarray(sin_t, F32)


def _permute_head_columns(w):
    d = w.shape[0]
    n_freq = RET_HEAD_DIM // 4
    w = w.reshape(d, N_RET_HEADS, 2, 2, n_freq)
    return w.transpose(0, 1, 3, 2, 4).reshape(d, N_RET_HEADS * RET_HEAD_DIM)


def _fourier_kernel(fin_ref, fg_ref, ab_ref, cs_ref, o_ref, pq_ref):
    length = fin_ref.shape[1]
    fg = FOURIER_GROUP

    @pl.when(pl.program_id(1) == 0)
    def _():
        for g in range(N_FOURIER_GROUPS):
            u = fin_ref[0, :, g * fg:(g + 1) * fg]
            pq = jnp.dot(u, ab_ref[g], preferred_element_type=F32)
            pq_ref[0:length, g * fg:(g + 1) * fg] = pq[:, 0:fg].astype(BF16)
            pq_ref[length:2 * length, g * fg:(g + 1) * fg] = pq[:, fg:2 * fg].astype(BF16)

    y = jnp.dot(cs_ref[...], pq_ref[...], preferred_element_type=F32)
    o_ref[0] = (y * _silu(fg_ref[0].astype(F32))).astype(BF16)


def _fourier(proj, ab, cs, *, tm):
    bsz, length, _ = proj.shape
    return pl.pallas_call(
        _fourier_kernel,
        grid=(bsz, length // tm),
        in_specs=[
            pl.BlockSpec((1, length, D_FOURIER), lambda b, t: (b, 0, COL_F_IN // D_FOURIER)),
            pl.BlockSpec((1, tm, D_FOURIER), lambda b, t: (b, t, COL_F_GATE // D_FOURIER)),
            pl.BlockSpec((N_FOURIER_GROUPS, FOURIER_GROUP, 2 * FOURIER_GROUP), lambda b, t: (0, 0, 0)),
            pl.BlockSpec((tm, 2 * length), lambda b, t: (t, 0)),
        ],
        out_specs=pl.BlockSpec((1, tm, D_FOURIER), lambda b, t: (b, t, 0)),
        out_shape=jax.ShapeDtypeStruct((bsz, length, D_FOURIER), BF16),
        scratch_shapes=[pltpu.VMEM((2 * length, D_FOURIER), BF16)],
        compiler_params=pltpu.CompilerParams(
            dimension_semantics=("parallel", "arbitrary"), vmem_limit_bytes=VMEM_LIMIT),
        name="fourier",
    )(proj, proj, ab, cs)


def _seq_dft_matrix(length):
    idx = jnp.arange(length, dtype=jnp.int32)
    ang = ((idx[:, None] * idx[None, :]) % length).astype(F32) * (2.0 * np.pi / length)
    scale = 1.0 / np.sqrt(length)
    return jnp.concatenate([jnp.cos(ang) * scale, jnp.sin(ang) * (-scale)], axis=1).astype(BF16)


def _ret_kernel(q_ref, k_ref, v_ref, rg_ref, kc_ref, vc_ref, dm_ref, rd_ref, cd_ref, gain_ref,
                o_ref, u_ref, sf_ref, sb_ref):
    c = RET_CHUNK
    n_chunks = q_ref.shape[1] // c
    n_ctx = kc_ref.shape[1] // c
    dm = dm_ref[0]
    q_dec_f, q_dec_b, k_dec_f, k_dec_b = rd_ref[0, 0], rd_ref[0, 1], rd_ref[0, 2], rd_ref[0, 3]
    c_dec_f = cd_ref[0, 0, 0:1, :]
    c_dec_b = cd_ref[0, 1, 0:1, :]
    gain = gain_ref[...]

    def kv_outer(k, v):
        kf = k.astype(F32)
        kd = jnp.concatenate([(kf * k_dec_f).astype(BF16), (kf * k_dec_b).astype(BF16)], axis=1)
        return lax.dot_general(kd, v, (((0,), (0,)), ((), ())), preferred_element_type=F32)

    for j in range(n_ctx):
        u_ref[j] = kv_outer(kc_ref[0, j * c:(j + 1) * c, :], vc_ref[0, j * c:(j + 1) * c, :])
    for j in range(n_chunks):
        u_ref[n_ctx + j] = kv_outer(k_ref[0, j * c:(j + 1) * c, :], v_ref[0, j * c:(j + 1) * c, :])

    sf = jnp.zeros((c, c), F32)
    for j in range(n_ctx + n_chunks - 1):
        sf = c_dec_f * sf + u_ref[j, 0:c, :]
        if j + 1 >= n_ctx:
            sf_ref[j + 1 - n_ctx] = sf.astype(BF16)
    sb = jnp.zeros((c, c), F32)
    for j in reversed(range(n_ctx)):
        sb = c_dec_b * sb + u_ref[j, c:2 * c, :]
    for j in reversed(range(n_chunks)):
        sb_ref[j] = sb.astype(BF16)
        if j > 0:
            sb = c_dec_b * sb + u_ref[n_ctx + j, c:2 * c, :]

    for j in range(n_chunks):
        rows = slice(j * c, (j + 1) * c)
        q = q_ref[0, rows, :]
        s = lax.dot_general(q, k_ref[0, rows, :], (((1,), (1,)), ((), ())), preferred_element_type=F32)
        a = (s * dm).astype(BF16)
        qf = q.astype(F32)
        o = jnp.dot(a, v_ref[0, rows, :], preferred_element_type=F32)
        o += jnp.dot((qf * q_dec_f).astype(BF16), sf_ref[j], preferred_element_type=F32)
        o += jnp.dot((qf * q_dec_b).astype(BF16), sb_ref[j], preferred_element_type=F32)
        mu = jnp.mean(o, axis=-1, keepdims=True)
        d = o - mu
        var = jnp.mean(d * d, axis=-1, keepdims=True)
        on = d * lax.rsqrt(var + EPS)
        o_ref[0, rows, :] = (on * gain * _silu(rg_ref[0, rows, :].astype(F32))).astype(BF16)


def _retention(proj, proj_c, dm, rd, cd, gain):
    bsz, length, _ = proj.shape
    ctx_len = proj_c.shape[1]
    hd = RET_HEAD_DIM
    c = RET_CHUNK
    n_chunks = length // c
    n_ctx = ctx_len // c

    def col(base):
        return lambda b, h: (b, 0, base // hd + h)

    return pl.pallas_call(
        _ret_kernel,
        grid=(bsz, N_RET_HEADS),
        in_specs=[
            pl.BlockSpec((1, length, hd), col(COL_Q)),
            pl.BlockSpec((1, length, hd), col(COL_K)),
            pl.BlockSpec((1, length, hd), col(COL_V)),
            pl.BlockSpec((1, length, hd), col(COL_R_GATE)),
            pl.BlockSpec((1, ctx_len, hd), col(0)),
            pl.BlockSpec((1, ctx_len, hd), col(D_RET)),
            pl.BlockSpec((1, c, c), lambda b, h: (h, 0, 0)),
            pl.BlockSpec((1, 4, c, c), lambda b, h: (h, 0, 0, 0)),
            pl.BlockSpec((1, 2, 8, 128), lambda b, h: (h, 0, 0, 0)),
            pl.BlockSpec((1, hd), lambda b, h: (0, h)),
        ],
        out_specs=pl.BlockSpec((1, length, hd), lambda b, h: (b, 0, h)),
        out_shape=jax.ShapeDtypeStruct((bsz, length, D_RET), BF16),
        scratch_shapes=[
            pltpu.VMEM((n_ctx + n_chunks, 2 * c, c), F32),
            pltpu.VMEM((n_chunks, c, c), BF16),
            pltpu.VMEM((n_chunks, c, c), BF16),
        ],
        compiler_params=pltpu.CompilerParams(
            dimension_semantics=("parallel", "parallel"), vmem_limit_bytes=VMEM_LIMIT),
        name="retention",
    )(proj, proj, proj, proj, proj_c, proj_c, dm, rd, cd, gain)


def _outproj_kernel(yf_ref, yr_ref, w_ref, x_ref, gate_ref, g_ref, o_ref):
    y = jnp.dot(yf_ref[0], w_ref[0:D_FOURIER, :], preferred_element_type=F32)
    y += jnp.dot(yr_ref[0], w_ref[D_FOURIER:D_MIX, :], preferred_element_type=F32)
    ms = jnp.mean(y * y, axis=-1, keepdims=True)
    yn = y * lax.rsqrt(ms + EPS) * g_ref[...]
    o_ref[0] = x_ref[0] + gate_ref[0] * yn


def _outproj(yf, yr, w_out, x, gate, g_post, *, tm):
    bsz, length, _ = x.shape
    return pl.pallas_call(
        _outproj_kernel,
        grid=(bsz, length // tm),
        in_specs=[
            pl.BlockSpec((1, tm, D_FOURIER), lambda b, t: (b, t, 0)),
            pl.BlockSpec((1, tm, D_RET), lambda b, t: (b, t, 0)),
            pl.BlockSpec((D_MIX, D_MODEL), lambda b, t: (0, 0)),
            pl.BlockSpec((1, tm, D_MODEL), lambda b, t: (b, t, 0)),
            pl.BlockSpec((1, 1, D_MODEL), lambda b, t: (b, 0, 0)),
            pl.BlockSpec((1, D_MODEL), lambda b, t: (0, 0)),
        ],
        out_specs=pl.BlockSpec((1, tm, D_MODEL), lambda b, t: (b, t, 0)),
        out_shape=jax.ShapeDtypeStruct((bsz, length, D_MODEL), F32),
        compiler_params=pltpu.CompilerParams(
            dimension_semantics=("parallel", "parallel"), vmem_limit_bytes=VMEM_LIMIT),
        name="outproj",
    )(yf, yr, w_out, x, gate, g_post)


def kernel(x, c, ctx, c_ctx, w_ada, b_ada, g_pre, g_post, w_in, w_fourier, decay_logit, ret_gn_gain, w_out):
    bsz, seq_len, _ = x.shape
    ctx_len = ctx.shape[1]
    assert w_ada.shape[0] == 1, "single layer only"
    assert bsz + 1 <= ADA_ROWS

    cvec = jnp.concatenate([c, c_ctx[None, :], jnp.zeros((ADA_ROWS - bsz - 1, D_MODEL), F32)], axis=0)
    mod = _adaln(cvec, w_ada[0], b_ada[0])
    shift = mod[:bsz, 0:D_MODEL].reshape(bsz, 1, D_MODEL)
    scale = mod[:bsz, D_MODEL:2 * D_MODEL].reshape(bsz, 1, D_MODEL)
    gate = mod[:bsz, 2 * D_MODEL:].reshape(bsz, 1, D_MODEL)
    shift_c = mod[bsz:bsz + 1, 0:D_MODEL].reshape(1, 1, D_MODEL)
    scale_c = mod[bsz:bsz + 1, D_MODEL:2 * D_MODEL].reshape(1, 1, D_MODEL)

    dm, rd, cd, ab = _tables(decay_logit[0], w_fourier[0])

    w = w_in[0]
    w_perm = jnp.concatenate([
        w[:, :COL_Q],
        _permute_head_columns(w[:, COL_Q:COL_K]),
        _permute_head_columns(w[:, COL_K:COL_V]),
        w[:, COL_V:],
    ], axis=1).astype(BF16)
    cos_t, sin_t = _rotary_tables(seq_len)
    g_pre2 = g_pre[0].reshape(1, D_MODEL)

    proj = _inproj(x, shift, scale, g_pre2, w_perm, cos_t, sin_t,
                   rot_lo=COL_Q, rot_hi=COL_V, tm=TOKEN_TILE)
    proj_c = _inproj(ctx, shift_c, scale_c, g_pre2, w_perm[:, COL_K:COL_R_GATE], cos_t, sin_t,
                     rot_lo=0, rot_hi=0, tm=ctx_len)

    cs = _seq_dft_matrix(seq_len)
    yf = _fourier(proj, ab, cs, tm=TOKEN_TILE)
    yr = _retention(proj, proj_c, dm, rd, cd, ret_gn_gain[0].reshape(1, D_RET))
    return _outproj(yf, yr, w_out[0].astype(BF16), x, gate, g_post[0].reshape(1, D_MODEL), tm=TOKEN_TILE)
```

```python
import functools

import numpy as np
import jax
import jax.numpy as jnp
from jax import lax
from jax.experimental import pallas as pl
from jax.experimental.pallas import tpu as pltpu

D_MODEL = 1024
GRID_W = 64
D_FOURIER = 512
N_FOURIER_GROUPS = 4
FOURIER_GROUP = D_FOURIER // N_FOURIER_GROUPS
D_RET = 512
N_RET_HEADS = 4
RET_HEAD_DIM = D_RET // N_RET_HEADS
D_MIX = D_FOURIER + D_RET
D_IN = 2 * D_FOURIER + 4 * D_RET
RET_CHUNK = 128
ROPE_BASE = 10000.0
QK_SCALE = RET_HEAD_DIM ** -0.5
EPS = 1e-6

COL_F_IN = 0
COL_F_GATE = D_FOURIER
COL_Q = 2 * D_FOURIER
COL_K = COL_Q + D_RET
COL_V = COL_K + D_RET
COL_R_GATE = COL_V + D_RET

MXU_COLS = 256
FOLD_BLOCK = 256
ADA_ROWS = 24
TOKEN_TILE = 512
VMEM_LIMIT = 56 * 1024 * 1024

F32 = jnp.float32
BF16 = jnp.bfloat16


def _silu(v):
    return v * jax.nn.sigmoid(v)


def _adaln_kernel(cv_ref, w_ref, b_ref, o_ref):
    s = _silu(cv_ref[...])
    o_ref[...] = jnp.dot(s, w_ref[...], preferred_element_type=F32) + b_ref[...]


def _adaln(cvec, w_ada, b_ada):
    n_out = w_ada.shape[1]
    bn = 512
    return pl.pallas_call(
        _adaln_kernel,
        grid=(n_out // bn,),
        in_specs=[
            pl.BlockSpec((ADA_ROWS, D_MODEL), lambda j: (0, 0)),
            pl.BlockSpec((D_MODEL, bn), lambda j: (0, j)),
            pl.BlockSpec((1, bn), lambda j: (0, j)),
        ],
        out_specs=pl.BlockSpec((ADA_ROWS, bn), lambda j: (0, j)),
        out_shape=jax.ShapeDtypeStruct((ADA_ROWS, n_out), F32),
        name="adaln",
    )(cvec, w_ada, b_ada.reshape(1, n_out))


def _tables_kernel(dl_ref, wf_ref, cc_ref, sc_ref, dm_ref, rd_ref, cd_ref, ab_ref):
    c = RET_CHUNK
    n = lax.broadcasted_iota(jnp.int32, (c, c), 0).astype(F32)
    m = lax.broadcasted_iota(jnp.int32, (c, c), 1).astype(F32)
    diff = n - m
    def log_sigmoid(v):
        return jnp.minimum(v, 0.0) - jnp.log1p(jnp.exp(-jnp.abs(v)))

    for h in range(N_RET_HEADS):
        lg_f = jnp.broadcast_to(log_sigmoid(dl_ref[0, h])[0:1, :], (c, c))
        lg_b = jnp.broadcast_to(log_sigmoid(dl_ref[1, h])[0:1, :], (c, c))
        dm_ref[h] = QK_SCALE * jnp.where(diff >= 0, jnp.exp(lg_f * jnp.maximum(diff, 0.0)),
                                         jnp.exp(lg_b * jnp.maximum(-diff, 0.0)))
        rd_ref[h, 0] = QK_SCALE * jnp.exp(lg_f * (n + 1.0))
        rd_ref[h, 1] = QK_SCALE * jnp.exp(lg_b * (c - n))
        rd_ref[h, 2] = jnp.exp(lg_f * (c - 1.0 - n))
        rd_ref[h, 3] = jnp.exp(lg_b * n)
        cd_ref[h, 0] = jnp.exp(lg_f[0:8, :] * float(c))
        cd_ref[h, 1] = jnp.exp(lg_b[0:8, :] * float(c))
    for g in range(N_FOURIER_GROUPS):
        wf = wf_ref[g]
        a = jnp.dot(cc_ref[...], wf, preferred_element_type=F32, precision=lax.Precision.HIGHEST)
        b = jnp.dot(sc_ref[...], wf, preferred_element_type=F32, precision=lax.Precision.HIGHEST)
        ab_ref[g, :, 0:FOURIER_GROUP] = a.astype(BF16)
        ab_ref[g, :, FOURIER_GROUP:2 * FOURIER_GROUP] = b.astype(BF16)


def _tables(decay_logit, w_fourier):
    c = RET_CHUNK
    dl = jnp.broadcast_to(decay_logit[:, :, None, None], (2, N_RET_HEADS, 8, 128))
    idx = np.arange(FOURIER_GROUP)
    ang = 2.0 * np.pi * ((idx[:, None] * idx[None, :]) % FOURIER_GROUP) / FOURIER_GROUP
    cc = jnp.asarray(np.cos(ang) / np.sqrt(FOURIER_GROUP), F32)
    sc = jnp.asarray(np.sin(ang) / np.sqrt(FOURIER_GROUP), F32)
    return pl.pallas_call(
        _tables_kernel,
        out_shape=(
            jax.ShapeDtypeStruct((N_RET_HEADS, c, c), F32),
            jax.ShapeDtypeStruct((N_RET_HEADS, 4, c, c), F32),
            jax.ShapeDtypeStruct((N_RET_HEADS, 2, 8, 128), F32),
            jax.ShapeDtypeStruct((N_FOURIER_GROUPS, FOURIER_GROUP, 2 * FOURIER_GROUP), BF16),
        ),
        name="tables",
    )(dl, w_fourier, cc, sc)


def _inproj_kernel(x_ref, shift_ref, scale_ref, g_ref, w_ref, cos_ref, sin_ref, o_ref, *, rot_lo, rot_hi):
    x = x_ref[0]
    ms = jnp.mean(x * x, axis=-1, keepdims=True)
    h = x * lax.rsqrt(ms + EPS) * g_ref[...]
    h = h * (1.0 + scale_ref[0]) + shift_ref[0]
    hb = h.astype(BF16)
    n_out = w_ref.shape[1]
    hd = RET_HEAD_DIM
    bn = MXU_COLS
    for j in range(n_out // bn):
        p = jnp.dot(hb, w_ref[:, j * bn:(j + 1) * bn], preferred_element_type=F32)
        for i in range(bn // hd):
            ph = p[:, i * hd:(i + 1) * hd]
            lo = j * bn + i * hd
            if rot_lo <= lo < rot_hi:
                ph = ph * cos_ref[...] + pltpu.roll(ph, hd // 2, axis=1) * sin_ref[...]
            o_ref[0, :, lo:lo + hd] = ph.astype(BF16)


def _inproj(x, shift, scale, g_pre, w, cos_t, sin_t, *, rot_lo, rot_hi, tm):
    bsz, length, _ = x.shape
    n_out = w.shape[1]
    per_batch = shift.shape[0] > 1
    mod_map = (lambda b, t: (b, 0, 0)) if per_batch else (lambda b, t: (0, 0, 0))
    return pl.pallas_call(
        functools.partial(_inproj_kernel, rot_lo=rot_lo, rot_hi=rot_hi),
        grid=(bsz, length // tm),
        in_specs=[
            pl.BlockSpec((1, tm, D_MODEL), lambda b, t: (b, t, 0)),
            pl.BlockSpec((1, 1, D_MODEL), mod_map),
            pl.BlockSpec((1, 1, D_MODEL), mod_map),
            pl.BlockSpec((1, D_MODEL), lambda b, t: (0, 0)),
            pl.BlockSpec((D_MODEL, n_out), lambda b, t: (0, 0)),
            pl.BlockSpec((tm, RET_HEAD_DIM), lambda b, t: (t, 0)),
            pl.BlockSpec((tm, RET_HEAD_DIM), lambda b, t: (t, 0)),
        ],
        out_specs=pl.BlockSpec((1, tm, n_out), lambda b, t: (b, t, 0)),
        out_shape=jax.ShapeDtypeStruct((bsz, length, n_out), BF16),
        compiler_params=pltpu.CompilerParams(
            dimension_semantics=("parallel", "parallel"), vmem_limit_bytes=VMEM_LIMIT),
        name="inproj",
    )(x, shift, scale, g_pre, w, cos_t, sin_t)


def _rotary_tables(seq_len):
    n_freq = RET_HEAD_DIM // 4
    pos = np.arange(seq_len)
    row = (pos // GRID_W).astype(np.float64)
    col = (pos % GRID_W).astype(np.float64)
    inv_freq = ROPE_BASE ** (-np.arange(n_freq, dtype=np.float64) / n_freq)
    ang_r = row[:, None] * inv_freq[None, :]
    ang_c = col[:, None] * inv_freq[None, :]
    cos_t = np.concatenate([np.cos(ang_r), np.cos(ang_c), np.cos(ang_r), np.cos(ang_c)], axis=1)
    sin_t = np.concatenate([-np.sin(ang_r), -np.sin(ang_c), np.sin(ang_r), np.sin(ang_c)], axis=1)
    return jnp.asarray(cos_t, F32), jnp.asarray(sin_t, F32)


def _permute_head_columns(w):
    d = w.shape[0]
    n_freq = RET_HEAD_DIM // 4
    w = w.reshape(d, N_RET_HEADS, 2, 2, n_freq)
    return w.transpose(0, 1, 3, 2, 4).reshape(d, N_RET_HEADS * RET_HEAD_DIM)


def _fourier_kernel(u_ref, g_ref, ab_ref, c_ref, s_ref, js_ref, o_ref, pe_ref, qo_ref, gs_ref):
    n = u_ref.shape[1]
    h = n // 2
    fb = FOLD_BLOCK
    nb = h // fb
    fg = FOURIER_GROUP
    inv_sqrt_n = 1.0 / np.sqrt(n)
    js = js_ref[...]
    first_row = lax.broadcasted_iota(jnp.int32, (fb, D_FOURIER), 0) == 0

    def channel_map(v, half):
        return jnp.concatenate(
            [jnp.dot(v[:, g * fg:(g + 1) * fg], ab_ref[g, :, half * fg:(half + 1) * fg],
                     preferred_element_type=F32) for g in range(N_FOURIER_GROUPS)], axis=1)

    for i in range(nb):
        lo = u_ref[0, i * fb:(i + 1) * fb, :].astype(F32)
        src = n - (i + 1) * fb
        t = jnp.dot(js, u_ref[0, src:src + fb, :], preferred_element_type=F32)
        if i > 0:
            t = jnp.where(first_row, u_ref[0, n - i * fb:n - i * fb + 16, :][0:1, :].astype(F32), t)
        pe_ref[i * fb:(i + 1) * fb, :] = channel_map((lo + t).astype(BF16), 0).astype(BF16)
        qo_ref[i * fb:(i + 1) * fb, :] = channel_map((lo - t).astype(BF16), 1).astype(BF16)

    p_mid = channel_map(u_ref[0, h:h + 16, :], 0)[0:1, :] * inv_sqrt_n
    sign = (1 - 2 * (lax.broadcasted_iota(jnp.int32, (h, D_FOURIER), 0) & 1)).astype(F32)
    y_mid = jnp.sum(pe_ref[...].astype(F32) * sign, axis=0, keepdims=True) * inv_sqrt_n + p_mid

    yc = jnp.dot(c_ref[...], pe_ref[...], preferred_element_type=F32) + sign * p_mid
    ys = jnp.dot(s_ref[...], qo_ref[...], preferred_element_type=F32)
    o_ref[0, 0:h, :] = ((yc - ys) * _silu(g_ref[0, 0:h, :].astype(F32))).astype(BF16)
    gs_ref[...] = (yc + ys).astype(BF16)

    for i in range(nb):
        src = h - (i + 1) * fb
        t = jnp.dot(js, gs_ref[src:src + fb, :], preferred_element_type=F32)
        edge = y_mid if i == 0 else gs_ref[h - i * fb:h - i * fb + 16, :][0:1, :].astype(F32)
        t = jnp.where(first_row, edge, t)
        rows = slice(h + i * fb, h + (i + 1) * fb)
        o_ref[0, rows, :] = (t * _silu(g_ref[0, rows, :].astype(F32))).astype(BF16)


def _fourier(proj, ab, c_mat, s_mat, js):
    bsz, length, _ = proj.shape
    h = length // 2
    return pl.pallas_call(
        _fourier_kernel,
        grid=(bsz,),
        in_specs=[
            pl.BlockSpec((1, length, D_FOURIER), lambda b: (b, 0, COL_F_IN // D_FOURIER)),
            pl.BlockSpec((1, length, D_FOURIER), lambda b: (b, 0, COL_F_GATE // D_FOURIER)),
            pl.BlockSpec((N_FOURIER_GROUPS, FOURIER_GROUP, 2 * FOURIER_GROUP), lambda b: (0, 0, 0)),
            pl.BlockSpec((h, h), lambda b: (0, 0)),
            pl.BlockSpec((h, h), lambda b: (0, 0)),
            pl.BlockSpec((FOLD_BLOCK, FOLD_BLOCK), lambda b: (0, 0)),
        ],
        out_specs=pl.BlockSpec((1, length, D_FOURIER), lambda b: (b, 0, 0)),
        out_shape=jax.ShapeDtypeStruct((bsz, length, D_FOURIER), BF16),
        scratch_shapes=[
            pltpu.VMEM((h, D_FOURIER), BF16),
            pltpu.VMEM((h, D_FOURIER), BF16),
            pltpu.VMEM((h, D_FOURIER), BF16),
        ],
        compiler_params=pltpu.CompilerParams(
            dimension_semantics=("parallel",), vmem_limit_bytes=VMEM_LIMIT),
        name="fourier",
    )(proj, proj, ab, c_mat, s_mat, js)


def _seq_dft_matrices(length):
    h = length // 2
    idx = np.arange(h)
    ang = 2.0 * np.pi * ((idx[:, None] * idx[None, :]) % length) / length
    scale = 1.0 / np.sqrt(length)
    c_mat = jnp.asarray((np.cos(ang) * scale).astype(np.float32)).astype(BF16)
    s_mat = jnp.asarray((np.sin(ang) * scale).astype(np.float32)).astype(BF16)
    r = np.arange(1, FOLD_BLOCK)
    js = np.zeros((FOLD_BLOCK, FOLD_BLOCK), np.float32)
    js[r, FOLD_BLOCK - r] = 1.0
    return c_mat, s_mat, jnp.asarray(js).astype(BF16)


def _ret_kernel(q_ref, k_ref, v_ref, rg_ref, kc_ref, vc_ref, dm_ref, rd_ref, cd_ref, gain_ref,
                o_ref, u_ref, sf_ref, sb_ref):
    c = RET_CHUNK
    n_chunks = q_ref.shape[1] // c
    n_ctx = kc_ref.shape[1] // c
    dm = dm_ref[0]
    q_dec_f, q_dec_b, k_dec_f, k_dec_b = rd_ref[0, 0], rd_ref[0, 1], rd_ref[0, 2], rd_ref[0, 3]
    c_dec_f = cd_ref[0, 0, 0:1, :]
    c_dec_b = cd_ref[0, 1, 0:1, :]
    gain = gain_ref[...]

    def kv_outer(k, v):
        kf = k.astype(F32)
        kd = jnp.concatenate([(kf * k_dec_f).astype(BF16), (kf * k_dec_b).astype(BF16)], axis=1)
        return lax.dot_general(kd, v, (((0,), (0,)), ((), ())), preferred_element_type=F32)

    for j in range(n_ctx):
        u_ref[j] = kv_outer(kc_ref[0, j * c:(j + 1) * c, :], vc_ref[0, j * c:(j + 1) * c, :])
    for j in range(n_chunks):
        u_ref[n_ctx + j] = kv_outer(k_ref[0, j * c:(j + 1) * c, :], v_ref[0, j * c:(j + 1) * c, :])

    sf = jnp.zeros((c, c), F32)
    for j in range(n_ctx + n_chunks - 1):
        sf = c_dec_f * sf + u_ref[j, 0:c, :]
        if j + 1 >= n_ctx:
            sf_ref[j + 1 - n_ctx] = sf.astype(BF16)
    sb = jnp.zeros((c, c), F32)
    for j in reversed(range(n_ctx)):
        sb = c_dec_b * sb + u_ref[j, c:2 * c, :]
    for j in reversed(range(n_chunks)):
        sb_ref[j] = sb.astype(BF16)
        if j > 0:
            sb = c_dec_b * sb + u_ref[n_ctx + j, c:2 * c, :]

    for j in range(n_chunks):
        rows = slice(j * c, (j + 1) * c)
        q = q_ref[0, rows, :]
        s = lax.dot_general(q, k_ref[0, rows, :], (((1,), (1,)), ((), ())), preferred_element_type=F32)
        a = (s * dm).astype(BF16)
        qf = q.astype(F32)
        o = jnp.dot(a, v_ref[0, rows, :], preferred_element_type=F32)
        o += jnp.dot((qf * q_dec_f).astype(BF16), sf_ref[j], preferred_element_type=F32)
        o += jnp.dot((qf * q_dec_b).astype(BF16), sb_ref[j], preferred_element_type=F32)
        mu = jnp.mean(o, axis=-1, keepdims=True)
        d = o - mu
        var = jnp.mean(d * d, axis=-1, keepdims=True)
        on = d * lax.rsqrt(var + EPS)
        o_ref[0, rows, :] = (on * gain * _silu(rg_ref[0, rows, :].astype(F32))).astype(BF16)


def _retention(proj, proj_c, dm, rd, cd, gain):
    bsz, length, _ = proj.shape
    ctx_len = proj_c.shape[1]
    hd = RET_HEAD_DIM
    c = RET_CHUNK
    n_chunks = length // c
    n_ctx = ctx_len // c

    def col(base):
        return lambda b, h: (b, 0, base // hd + h)

    return pl.pallas_call(
        _ret_kernel,
        grid=(bsz, N_RET_HEADS),
        in_specs=[
            pl.BlockSpec((1, length, hd), col(COL_Q)),
            pl.BlockSpec((1, length, hd), col(COL_K)),
            pl.BlockSpec((1, length, hd), col(COL_V)),
            pl.BlockSpec((1, length, hd), col(COL_R_GATE)),
            pl.BlockSpec((1, ctx_len, hd), col(0)),
            pl.BlockSpec((1, ctx_len, hd), col(D_RET)),
            pl.BlockSpec((1, c, c), lambda b, h: (h, 0, 0)),
            pl.BlockSpec((1, 4, c, c), lambda b, h: (h, 0, 0, 0)),
            pl.BlockSpec((1, 2, 8, 128), lambda b, h: (h, 0, 0, 0)),
            pl.BlockSpec((1, hd), lambda b, h: (0, h)),
        ],
        out_specs=pl.BlockSpec((1, length, hd), lambda b, h: (b, 0, h)),
        out_shape=jax.ShapeDtypeStruct((bsz, length, D_RET), BF16),
        scratch_shapes=[
            pltpu.VMEM((n_ctx + n_chunks, 2 * c, c), F32),
            pltpu.VMEM((n_chunks, c, c), BF16),
            pltpu.VMEM((n_chunks, c, c), BF16),
        ],
        compiler_params=pltpu.CompilerParams(
            dimension_semantics=("parallel", "parallel"), vmem_limit_bytes=VMEM_LIMIT),
        name="retention",
    )(proj, proj, proj, proj, proj_c, proj_c, dm, rd, cd, gain)


def _outproj_kernel(yf_ref, yr_ref, w_ref, x_ref, gate_ref, g_ref, o_ref):
    y = jnp.dot(yf_ref[0], w_ref[0:D_FOURIER, :], preferred_element_type=F32)
    y += jnp.dot(yr_ref[0], w_ref[D_FOURIER:D_MIX, :], preferred_element_type=F32)
    ms = jnp.mean(y * y, axis=-1, keepdims=True)
    yn = y * lax.rsqrt(ms + EPS) * g_ref[...]
    o_ref[0] = x_ref[0] + gate_ref[0] * yn


def _outproj(yf, yr, w_out, x, gate, g_post, *, tm):
    bsz, length, _ = x.shape
    return pl.pallas_call(
        _outproj_kernel,
        grid=(bsz, length // tm),
        in_specs=[
            pl.BlockSpec((1, tm, D_FOURIER), lambda b, t: (b, t, 0)),
            pl.BlockSpec((1, tm, D_RET), lambda b, t: (b, t, 0)),
            pl.BlockSpec((D_MIX, D_MODEL), lambda b, t: (0, 0)),
            pl.BlockSpec((1, tm, D_MODEL), lambda b, t: (b, t, 0)),
            pl.BlockSpec((1, 1, D_MODEL), lambda b, t: (b, 0, 0)),
            pl.BlockSpec((1, D_MODEL), lambda b, t: (0, 0)),
        ],
        out_specs=pl.BlockSpec((1, tm, D_MODEL), lambda b, t: (b, t, 0)),
        out_shape=jax.ShapeDtypeStruct((bsz, length, D_MODEL), F32),
        compiler_params=pltpu.CompilerParams(
            dimension_semantics=("parallel", "parallel"), vmem_limit_bytes=VMEM_LIMIT),
        name="outproj",
    )(yf, yr, w_out, x, gate, g_post)


def kernel(x, c, ctx, c_ctx, w_ada, b_ada, g_pre, g_post, w_in, w_fourier, decay_logit, ret_gn_gain, w_out):
    bsz, seq_len, _ = x.shape
    ctx_len = ctx.shape[1]
    assert w_ada.shape[0] == 1, "single layer only"
    assert bsz + 1 <= ADA_ROWS

    cvec = jnp.concatenate([c, c_ctx[None, :], jnp.zeros((ADA_ROWS - bsz - 1, D_MODEL), F32)], axis=0)
    mod = _adaln(cvec, w_ada[0], b_ada[0])
    shift = mod[:bsz, 0:D_MODEL].reshape(bsz, 1, D_MODEL)
    scale = mod[:bsz, D_MODEL:2 * D_MODEL].reshape(bsz, 1, D_MODEL)
    gate = mod[:bsz, 2 * D_MODEL:].reshape(bsz, 1, D_MODEL)
    shift_c = mod[bsz:bsz + 1, 0:D_MODEL].reshape(1, 1, D_MODEL)
    scale_c = mod[bsz:bsz + 1, D_MODEL:2 * D_MODEL].reshape(1, 1, D_MODEL)

    dm, rd, cd, ab = _tables(decay_logit[0], w_fourier[0])

    w = w_in[0]
    w_perm = jnp.concatenate([
        w[:, :COL_Q],
        _permute_head_columns(w[:, COL_Q:COL_K]),
        _permute_head_columns(w[:, COL_K:COL_V]),
        w[:, COL_V:],
    ], axis=1).astype(BF16)
    cos_t, sin_t = _rotary_tables(seq_len)
    g_pre2 = g_pre[0].reshape(1, D_MODEL)

    proj = _inproj(x, shift, scale, g_pre2, w_perm, cos_t, sin_t,
                   rot_lo=COL_Q, rot_hi=COL_V, tm=TOKEN_TILE)
    proj_c = _inproj(ctx, shift_c, scale_c, g_pre2, w_perm[:, COL_K:COL_R_GATE], cos_t, sin_t,
                     rot_lo=0, rot_hi=0, tm=ctx_len)

    assert (seq_len // 2) % FOLD_BLOCK == 0
    yf = _fourier(proj, ab, *_seq_dft_matrices(seq_len))
    yr = _retention(proj, proj_c, dm, rd, cd, ret_gn_gain[0].reshape(1, D_RET))
    return _outproj(yf, yr, w_out[0].astype(BF16), x, gate, g_post[0].reshape(1, D_MODEL), tm=TOKEN_TILE)
```

```python
import functools

import numpy as np
import jax
import jax.numpy as jnp
from jax import lax
from jax.experimental import pallas as pl
from jax.experimental.pallas import tpu as pltpu

D_MODEL = 1024
GRID_W = 64
D_FOURIER = 512
N_FOURIER_GROUPS = 4
FOURIER_GROUP = D_FOURIER // N_FOURIER_GROUPS
D_RET = 512
N_RET_HEADS = 4
RET_HEAD_DIM = D_RET // N_RET_HEADS
D_MIX = D_FOURIER + D_RET
D_IN = 2 * D_FOURIER + 4 * D_RET
RET_CHUNK = 128
ROPE_BASE = 10000.0
QK_SCALE = RET_HEAD_DIM ** -0.5
EPS = 1e-6

COL_F_IN = 0
COL_F_GATE = D_FOURIER
COL_Q = 2 * D_FOURIER
COL_K = COL_Q + D_RET
COL_V = COL_K + D_RET
COL_R_GATE = COL_V + D_RET

MXU_COLS = 256
FOLD_BLOCK = 256
ADA_ROWS = 24
TOKEN_TILE = 512
VMEM_LIMIT = 56 * 1024 * 1024

F32 = jnp.float32
BF16 = jnp.bfloat16


def _silu(v):
    return v * jax.nn.sigmoid(v)


def _adaln_kernel(cv_ref, w_ref, b_ref, o_ref):
    s = _silu(cv_ref[...])
    o_ref[...] = jnp.dot(s, w_ref[...], preferred_element_type=F32) + b_ref[...]


def _adaln(cvec, w_ada, b_ada):
    n_out = w_ada.shape[1]
    bn = 512
    return pl.pallas_call(
        _adaln_kernel,
        grid=(n_out // bn,),
        in_specs=[
            pl.BlockSpec((ADA_ROWS, D_MODEL), lambda j: (0, 0)),
            pl.BlockSpec((D_MODEL, bn), lambda j: (0, j)),
            pl.BlockSpec((1, bn), lambda j: (0, j)),
        ],
        out_specs=pl.BlockSpec((ADA_ROWS, bn), lambda j: (0, j)),
        out_shape=jax.ShapeDtypeStruct((ADA_ROWS, n_out), F32),
        name="adaln",
    )(cvec, w_ada, b_ada.reshape(1, n_out))


def _tables_kernel(dl_ref, wf_ref, cc_ref, sc_ref, dm_ref, rd_ref, cd_ref, ab_ref):
    c = RET_CHUNK
    n = lax.broadcasted_iota(jnp.int32, (c, c), 0).astype(F32)
    m = lax.broadcasted_iota(jnp.int32, (c, c), 1).astype(F32)
    diff = n - m

    def log_sigmoid(v):
        return jnp.minimum(v, 0.0) - jnp.log1p(jnp.exp(-jnp.abs(v)))

    for h in range(N_RET_HEADS):
        lg_f = jnp.broadcast_to(log_sigmoid(dl_ref[0, h])[0:1, :], (c, c))
        lg_b = jnp.broadcast_to(log_sigmoid(dl_ref[1, h])[0:1, :], (c, c))
        dm_ref[h] = QK_SCALE * jnp.where(diff >= 0, jnp.exp(lg_f * jnp.maximum(diff, 0.0)),
                                         jnp.exp(lg_b * jnp.maximum(-diff, 0.0)))
        rd_ref[h, 0, :, 0:c] = (QK_SCALE * jnp.exp(lg_f * (n + 1.0))).astype(BF16)
        rd_ref[h, 0, :, c:2 * c] = (QK_SCALE * jnp.exp(lg_b * (c - n))).astype(BF16)
        rd_ref[h, 1, :, 0:c] = jnp.exp(lg_f * (c - 1.0 - n)).astype(BF16)
        rd_ref[h, 1, :, c:2 * c] = jnp.exp(lg_b * n).astype(BF16)
        cd_ref[h, :, 0:c] = jnp.exp(lg_f[0:8, :] * float(c))
        cd_ref[h, :, c:2 * c] = jnp.exp(lg_b[0:8, :] * float(c))
    for g in range(N_FOURIER_GROUPS):
        wf = wf_ref[g]
        a = jnp.dot(cc_ref[...], wf, preferred_element_type=F32, precision=lax.Precision.HIGHEST)
        b = jnp.dot(sc_ref[...], wf, preferred_element_type=F32, precision=lax.Precision.HIGHEST)
        ab_ref[g, :, 0:FOURIER_GROUP] = a.astype(BF16)
        ab_ref[g, :, FOURIER_GROUP:2 * FOURIER_GROUP] = b.astype(BF16)


def _tables(decay_logit, w_fourier):
    c = RET_CHUNK
    dl = jnp.broadcast_to(decay_logit[:, :, None, None], (2, N_RET_HEADS, 8, 128))
    idx = np.arange(FOURIER_GROUP)
    ang = 2.0 * np.pi * ((idx[:, None] * idx[None, :]) % FOURIER_GROUP) / FOURIER_GROUP
    cc = jnp.asarray(np.cos(ang) / np.sqrt(FOURIER_GROUP), F32)
    sc = jnp.asarray(np.sin(ang) / np.sqrt(FOURIER_GROUP), F32)
    return pl.pallas_call(
        _tables_kernel,
        out_shape=(
            jax.ShapeDtypeStruct((N_RET_HEADS, c, c), F32),
            jax.ShapeDtypeStruct((N_RET_HEADS, 2, c, 2 * c), BF16),
            jax.ShapeDtypeStruct((N_RET_HEADS, 8, 2 * c), F32),
            jax.ShapeDtypeStruct((N_FOURIER_GROUPS, FOURIER_GROUP, 2 * FOURIER_GROUP), BF16),
        ),
        name="tables",
    )(dl, w_fourier, cc, sc)


def _inproj_kernel(x_ref, shift_ref, scale_ref, g_ref, w_ref, cos_ref, sin_ref, o_ref, *, rot_lo, rot_hi):
    x = x_ref[0]
    ms = jnp.mean(x * x, axis=-1, keepdims=True)
    h = x * lax.rsqrt(ms + EPS) * g_ref[...]
    h = h * (1.0 + scale_ref[0]) + shift_ref[0]
    hb = h.astype(BF16)
    n_out = w_ref.shape[1]
    hd = RET_HEAD_DIM
    bn = MXU_COLS
    for j in range(n_out // bn):
        p = jnp.dot(hb, w_ref[:, j * bn:(j + 1) * bn], preferred_element_type=F32)
        for i in range(bn // hd):
            ph = p[:, i * hd:(i + 1) * hd]
            lo = j * bn + i * hd
            if rot_lo <= lo < rot_hi:
                ph = ph * cos_ref[...] + pltpu.roll(ph, hd // 2, axis=1) * sin_ref[...]
            o_ref[0, :, lo:lo + hd] = ph.astype(BF16)


def _inproj(x, shift, scale, g_pre, w, cos_t, sin_t, *, rot_lo, rot_hi, tm):
    bsz, length, _ = x.shape
    n_out = w.shape[1]
    per_batch = shift.shape[0] > 1
    mod_map = (lambda b, t: (b, 0, 0)) if per_batch else (lambda b, t: (0, 0, 0))
    return pl.pallas_call(
        functools.partial(_inproj_kernel, rot_lo=rot_lo, rot_hi=rot_hi),
        grid=(bsz, length // tm),
        in_specs=[
            pl.BlockSpec((1, tm, D_MODEL), lambda b, t: (b, t, 0)),
            pl.BlockSpec((1, 1, D_MODEL), mod_map),
            pl.BlockSpec((1, 1, D_MODEL), mod_map),
            pl.BlockSpec((1, D_MODEL), lambda b, t: (0, 0)),
            pl.BlockSpec((D_MODEL, n_out), lambda b, t: (0, 0)),
            pl.BlockSpec((tm, RET_HEAD_DIM), lambda b, t: (t, 0)),
            pl.BlockSpec((tm, RET_HEAD_DIM), lambda b, t: (t, 0)),
        ],
        out_specs=pl.BlockSpec((1, tm, n_out), lambda b, t: (b, t, 0)),
        out_shape=jax.ShapeDtypeStruct((bsz, length, n_out), BF16),
        compiler_params=pltpu.CompilerParams(
            dimension_semantics=("parallel", "parallel"), vmem_limit_bytes=VMEM_LIMIT),
        name="inproj",
    )(x, shift, scale, g_pre, w, cos_t, sin_t)


def _rotary_tables(seq_len):
    n_freq = RET_HEAD_DIM // 4
    pos = np.arange(seq_len)
    row = (pos // GRID_W).astype(np.float64)
    col = (pos % GRID_W).astype(np.float64)
    inv_freq = ROPE_BASE ** (-np.arange(n_freq, dtype=np.float64) / n_freq)
    ang_r = row[:, None] * inv_freq[None, :]
    ang_c = col[:, None] * inv_freq[None, :]
    cos_t = np.concatenate([np.cos(ang_r), np.cos(ang_c), np.cos(ang_r), np.cos(ang_c)], axis=1)
    sin_t = np.concatenate([-np.sin(ang_r), -np.sin(ang_c), np.sin(ang_r), np.sin(ang_c)], axis=1)
    return jnp.asarray(cos_t, F32), jnp.asarray(sin_t, F32)


def _permute_head_columns(w):
    d = w.shape[0]
    n_freq = RET_HEAD_DIM // 4
    w = w.reshape(d, N_RET_HEADS, 2, 2, n_freq)
    return w.transpose(0, 1, 3, 2, 4).reshape(d, N_RET_HEADS * RET_HEAD_DIM)


def _fourier_kernel(u_ref, g_ref, ab_ref, c_ref, s_ref, js_ref, o_ref, pe_ref, qo_ref, gs_ref):
    n = u_ref.shape[1]
    h = n // 2
    fb = FOLD_BLOCK
    nb = h // fb
    fg = FOURIER_GROUP
    inv_sqrt_n = 1.0 / np.sqrt(n)
    js = js_ref[...]
    first_row = lax.broadcasted_iota(jnp.int32, (fb, D_FOURIER), 0) == 0

    def channel_map(v, half):
        return jnp.concatenate(
            [jnp.dot(v[:, g * fg:(g + 1) * fg], ab_ref[g, :, half * fg:(half + 1) * fg],
                     preferred_element_type=F32) for g in range(N_FOURIER_GROUPS)], axis=1)

    for i in range(nb):
        lo = u_ref[0, i * fb:(i + 1) * fb, :].astype(F32)
        src = n - (i + 1) * fb
        t = jnp.dot(js, u_ref[0, src:src + fb, :], preferred_element_type=F32)
        if i > 0:
            t = jnp.where(first_row, u_ref[0, n - i * fb:n - i * fb + 16, :][0:1, :].astype(F32), t)
        pe_ref[i * fb:(i + 1) * fb, :] = channel_map((lo + t).astype(BF16), 0).astype(BF16)
        qo_ref[i * fb:(i + 1) * fb, :] = channel_map((lo - t).astype(BF16), 1).astype(BF16)

    p_mid = channel_map(u_ref[0, h:h + 16, :], 0)[0:1, :] * inv_sqrt_n
    sign = (1 - 2 * (lax.broadcasted_iota(jnp.int32, (h, D_FOURIER), 0) & 1)).astype(F32)
    y_mid = jnp.sum(pe_ref[...].astype(F32) * sign, axis=0, keepdims=True) * inv_sqrt_n + p_mid

    yc = jnp.dot(c_ref[...], pe_ref[...], preferred_element_type=F32) + sign * p_mid
    ys = jnp.dot(s_ref[...], qo_ref[...], preferred_element_type=F32)
    o_ref[0, 0:h, :] = ((yc - ys) * _silu(g_ref[0, 0:h, :].astype(F32))).astype(BF16)
    gs_ref[...] = (yc + ys).astype(BF16)

    for i in range(nb):
        src = h - (i + 1) * fb
        t = jnp.dot(js, gs_ref[src:src + fb, :], preferred_element_type=F32)
        edge = y_mid if i == 0 else gs_ref[h - i * fb:h - i * fb + 16, :][0:1, :].astype(F32)
        t = jnp.where(first_row, edge, t)
        rows = slice(h + i * fb, h + (i + 1) * fb)
        o_ref[0, rows, :] = (t * _silu(g_ref[0, rows, :].astype(F32))).astype(BF16)


def _fourier(proj, ab, c_mat, s_mat, js):
    bsz, length, _ = proj.shape
    h = length // 2
    return pl.pallas_call(
        _fourier_kernel,
        grid=(bsz,),
        in_specs=[
            pl.BlockSpec((1, length, D_FOURIER), lambda b: (b, 0, COL_F_IN // D_FOURIER)),
            pl.BlockSpec((1, length, D_FOURIER), lambda b: (b, 0, COL_F_GATE // D_FOURIER)),
            pl.BlockSpec((N_FOURIER_GROUPS, FOURIER_GROUP, 2 * FOURIER_GROUP), lambda b: (0, 0, 0)),
            pl.BlockSpec((h, h), lambda b: (0, 0)),
            pl.BlockSpec((h, h), lambda b: (0, 0)),
            pl.BlockSpec((FOLD_BLOCK, FOLD_BLOCK), lambda b: (0, 0)),
        ],
        out_specs=pl.BlockSpec((1, length, D_FOURIER), lambda b: (b, 0, 0)),
        out_shape=jax.ShapeDtypeStruct((bsz, length, D_FOURIER), BF16),
        scratch_shapes=[
            pltpu.VMEM((h, D_FOURIER), BF16),
            pltpu.VMEM((h, D_FOURIER), BF16),
            pltpu.VMEM((h, D_FOURIER), BF16),
        ],
        compiler_params=pltpu.CompilerParams(
            dimension_semantics=("parallel",), vmem_limit_bytes=VMEM_LIMIT),
        name="fourier",
    )(proj, proj, ab, c_mat, s_mat, js)


def _seq_dft_matrices(length):
    h = length // 2
    idx = np.arange(h)
    ang = 2.0 * np.pi * ((idx[:, None] * idx[None, :]) % length) / length
    scale = 1.0 / np.sqrt(length)
    c_mat = jnp.asarray((np.cos(ang) * scale).astype(np.float32)).astype(BF16)
    s_mat = jnp.asarray((np.sin(ang) * scale).astype(np.float32)).astype(BF16)
    r = np.arange(1, FOLD_BLOCK)
    js = np.zeros((FOLD_BLOCK, FOLD_BLOCK), np.float32)
    js[r, FOLD_BLOCK - r] = 1.0
    return c_mat, s_mat, jnp.asarray(js).astype(BF16)


def _ret_kernel(q_ref, k_ref, v_ref, rg_ref, kc_ref, vc_ref, dm_ref, rd_ref, cd_ref, gain_ref,
                o_ref, u_ref, st_ref):
    c = RET_CHUNK
    hd = RET_HEAD_DIM
    n_chunks = q_ref.shape[1] // c
    n_ctx = kc_ref.shape[1] // c

    for h in range(N_RET_HEADS):
        cols = slice(h * hd, (h + 1) * hd)
        dm = dm_ref[h]
        q_dec = rd_ref[h, 0]
        k_dec = rd_ref[h, 1]
        c_dec_f = cd_ref[h, 0:1, 0:hd]
        c_dec_b = cd_ref[h, 0:1, hd:2 * hd]
        gain = gain_ref[:, cols]

        def both_decays(t, dec):
            return jnp.concatenate([t * dec[:, 0:hd], t * dec[:, hd:2 * hd]], axis=1)

        def kv_outer(k, v):
            return lax.dot_general(v, both_decays(k, k_dec), (((0,), (0,)), ((), ())), preferred_element_type=F32)

        for j in range(n_ctx):
            rows = slice(j * c, (j + 1) * c)
            u_ref[j] = kv_outer(kc_ref[0, rows, cols], vc_ref[0, rows, cols])
        for j in range(n_chunks):
            rows = slice(j * c, (j + 1) * c)
            u_ref[n_ctx + j] = kv_outer(k_ref[0, rows, cols], v_ref[0, rows, cols])

        sf = jnp.zeros((hd, hd), F32)
        for j in range(n_ctx + n_chunks - 1):
            sf = c_dec_f * sf + u_ref[j, :, 0:hd]
            if j + 1 >= n_ctx:
                st_ref[j + 1 - n_ctx, :, 0:hd] = sf.astype(BF16)
        sb = jnp.zeros((hd, hd), F32)
        for j in reversed(range(n_ctx)):
            sb = c_dec_b * sb + u_ref[j, :, hd:2 * hd]
        for j in reversed(range(n_chunks)):
            st_ref[j, :, hd:2 * hd] = sb.astype(BF16)
            if j > 0:
                sb = c_dec_b * sb + u_ref[n_ctx + j, :, hd:2 * hd]

        for j in range(n_chunks):
            rows = slice(j * c, (j + 1) * c)
            q = q_ref[0, rows, cols]
            s = lax.dot_general(q, k_ref[0, rows, cols], (((1,), (1,)), ((), ())), preferred_element_type=F32)
            o = jnp.dot((s * dm).astype(BF16), v_ref[0, rows, cols], preferred_element_type=F32)
            o += lax.dot_general(both_decays(q, q_dec), st_ref[j], (((1,), (1,)), ((), ())),
                                 preferred_element_type=F32)
            mu = jnp.mean(o, axis=-1, keepdims=True)
            d = o - mu
            var = jnp.mean(d * d, axis=-1, keepdims=True)
            on = d * lax.rsqrt(var + EPS)
            o_ref[0, rows, cols] = (on * gain * _silu(rg_ref[0, rows, cols].astype(F32))).astype(BF16)


def _retention(proj, proj_c, dm, rd, cd, gain):
    bsz, length, _ = proj.shape
    ctx_len = proj_c.shape[1]
    hd = RET_HEAD_DIM
    c = RET_CHUNK
    n_chunks = length // c
    n_ctx = ctx_len // c
    assert n_ctx >= 1

    def col(base):
        return lambda b: (b, 0, base // D_RET)

    return pl.pallas_call(
        _ret_kernel,
        grid=(bsz,),
        in_specs=[
            pl.BlockSpec((1, length, D_RET), col(COL_Q)),
            pl.BlockSpec((1, length, D_RET), col(COL_K)),
            pl.BlockSpec((1, length, D_RET), col(COL_V)),
            pl.BlockSpec((1, length, D_RET), col(COL_R_GATE)),
            pl.BlockSpec((1, ctx_len, D_RET), col(0)),
            pl.BlockSpec((1, ctx_len, D_RET), col(D_RET)),
            pl.BlockSpec((N_RET_HEADS, c, c), lambda b: (0, 0, 0)),
            pl.BlockSpec((N_RET_HEADS, 2, c, 2 * hd), lambda b: (0, 0, 0, 0)),
            pl.BlockSpec((N_RET_HEADS, 8, 2 * hd), lambda b: (0, 0, 0)),
            pl.BlockSpec((1, D_RET), lambda b: (0, 0)),
        ],
        out_specs=pl.BlockSpec((1, length, D_RET), lambda b: (b, 0, 0)),
        out_shape=jax.ShapeDtypeStruct((bsz, length, D_RET), BF16),
        scratch_shapes=[
            pltpu.VMEM((n_ctx + n_chunks, hd, 2 * hd), F32),
            pltpu.VMEM((n_chunks, hd, 2 * hd), BF16),
        ],
        compiler_params=pltpu.CompilerParams(
            dimension_semantics=("parallel",), vmem_limit_bytes=VMEM_LIMIT),
        name="retention",
    )(proj, proj, proj, proj, proj_c, proj_c, dm, rd, cd, gain)


def _outproj_kernel(yf_ref, yr_ref, w_ref, x_ref, gate_ref, g_ref, o_ref):
    y = jnp.dot(yf_ref[0], w_ref[0:D_FOURIER, :], preferred_element_type=F32)
    y += jnp.dot(yr_ref[0], w_ref[D_FOURIER:D_MIX, :], preferred_element_type=F32)
    ms = jnp.mean(y * y, axis=-1, keepdims=True)
    yn = y * lax.rsqrt(ms + EPS) * g_ref[...]
    o_ref[0] = x_ref[0] + gate_ref[0] * yn


def _outproj(yf, yr, w_out, x, gate, g_post, *, tm):
    bsz, length, _ = x.shape
    return pl.pallas_call(
        _outproj_kernel,
        grid=(bsz, length // tm),
        in_specs=[
            pl.BlockSpec((1, tm, D_FOURIER), lambda b, t: (b, t, 0)),
            pl.BlockSpec((1, tm, D_RET), lambda b, t: (b, t, 0)),
            pl.BlockSpec((D_MIX, D_MODEL), lambda b, t: (0, 0)),
            pl.BlockSpec((1, tm, D_MODEL), lambda b, t: (b, t, 0)),
            pl.BlockSpec((1, 1, D_MODEL), lambda b, t: (b, 0, 0)),
            pl.BlockSpec((1, D_MODEL), lambda b, t: (0, 0)),
        ],
        out_specs=pl.BlockSpec((1, tm, D_MODEL), lambda b, t: (b, t, 0)),
        out_shape=jax.ShapeDtypeStruct((bsz, length, D_MODEL), F32),
        compiler_params=pltpu.CompilerParams(
            dimension_semantics=("parallel", "parallel"), vmem_limit_bytes=VMEM_LIMIT),
        name="outproj",
    )(yf, yr, w_out, x, gate, g_post)


def kernel(x, c, ctx, c_ctx, w_ada, b_ada, g_pre, g_post, w_in, w_fourier, decay_logit, ret_gn_gain, w_out):
    bsz, seq_len, _ = x.shape
    ctx_len = ctx.shape[1]
    assert w_ada.shape[0] == 1, "single layer only"
    assert bsz + 1 <= ADA_ROWS

    cvec = jnp.concatenate([c, c_ctx[None, :], jnp.zeros((ADA_ROWS - bsz - 1, D_MODEL), F32)], axis=0)
    mod = _adaln(cvec, w_ada[0], b_ada[0])
    shift = mod[:bsz, 0:D_MODEL].reshape(bsz, 1, D_MODEL)
    scale = mod[:bsz, D_MODEL:2 * D_MODEL].reshape(bsz, 1, D_MODEL)
    gate = mod[:bsz, 2 * D_MODEL:].reshape(bsz, 1, D_MODEL)
    shift_c = mod[bsz:bsz + 1, 0:D_MODEL].reshape(1, 1, D_MODEL)
    scale_c = mod[bsz:bsz + 1, D_MODEL:2 * D_MODEL].reshape(1, 1, D_MODEL)

    dm, rd, cd, ab = _tables(decay_logit[0], w_fourier[0])

    w = w_in[0]
    w_perm = jnp.concatenate([
        w[:, :COL_Q],
        _permute_head_columns(w[:, COL_Q:COL_K]),
        _permute_head_columns(w[:, COL_K:COL_V]),
        w[:, COL_V:],
    ], axis=1).astype(BF16)
    cos_t, sin_t = _rotary_tables(seq_len)
    g_pre2 = g_pre[0].reshape(1, D_MODEL)

    proj = _inproj(x, shift, scale, g_pre2, w_perm, cos_t, sin_t,
                   rot_lo=COL_Q, rot_hi=COL_V, tm=TOKEN_TILE)
    proj_c = _inproj(ctx, shift_c, scale_c, g_pre2, w_perm[:, COL_K:COL_R_GATE], cos_t, sin_t,
                     rot_lo=0, rot_hi=0, tm=ctx_len)

    assert (seq_len // 2) % FOLD_BLOCK == 0
    yf = _fourier(proj, ab, *_seq_dft_matrices(seq_len))
    yr = _retention(proj, proj_c, dm, rd, cd, ret_gn_gain[0].reshape(1, D_RET))
    return _outproj(yf, yr, w_out[0].astype(BF16), x, gate, g_post[0].reshape(1, D_MODEL), tm=TOKEN_TILE)
```

```python
import functools

import numpy as np
import jax
import jax.numpy as jnp
from jax import lax
from jax.experimental import pallas as pl
from jax.experimental.pallas import tpu as pltpu

D_MODEL = 1024
GRID_W = 64
D_FOURIER = 512
N_FOURIER_GROUPS = 4
FOURIER_GROUP = D_FOURIER // N_FOURIER_GROUPS
D_RET = 512
N_RET_HEADS = 4
RET_HEAD_DIM = D_RET // N_RET_HEADS
D_MIX = D_FOURIER + D_RET
D_IN = 2 * D_FOURIER + 4 * D_RET
RET_CHUNK = 128
ROPE_BASE = 10000.0
QK_SCALE = RET_HEAD_DIM ** -0.5
EPS = 1e-6

COL_F_IN = 0
COL_F_GATE = D_FOURIER
COL_Q = 2 * D_FOURIER
COL_K = COL_Q + D_RET
COL_V = COL_K + D_RET
COL_R_GATE = COL_V + D_RET

MXU_COLS = 256
FOLD_BLOCK = 256
ADA_ROWS = 24
TOKEN_TILE = 1024
VMEM_LIMIT = 56 * 1024 * 1024

F32 = jnp.float32
BF16 = jnp.bfloat16


def _silu(v):
    return v * jax.nn.sigmoid(v)


def _adaln_kernel(cv_ref, w_ref, b_ref, o_ref):
    s = _silu(cv_ref[...])
    o_ref[...] = jnp.dot(s, w_ref[...], preferred_element_type=F32) + b_ref[...]


def _adaln(cvec, w_ada, b_ada):
    n_out = w_ada.shape[1]
    bn = D_MODEL
    return pl.pallas_call(
        _adaln_kernel,
        grid=(n_out // bn,),
        in_specs=[
            pl.BlockSpec((ADA_ROWS, D_MODEL), lambda j: (0, 0)),
            pl.BlockSpec((D_MODEL, bn), lambda j: (0, j)),
            pl.BlockSpec((1, bn), lambda j: (0, j)),
        ],
        out_specs=pl.BlockSpec((ADA_ROWS, bn), lambda j: (0, j)),
        out_shape=jax.ShapeDtypeStruct((ADA_ROWS, n_out), F32),
        name="adaln",
    )(cvec, w_ada, b_ada.reshape(1, n_out))


def _tables_kernel(dl_ref, wf_ref, cc_ref, sc_ref, dm_ref, rd_ref, cd_ref, ab_ref):
    c = RET_CHUNK
    n = lax.broadcasted_iota(jnp.int32, (c, c), 0).astype(F32)
    m = lax.broadcasted_iota(jnp.int32, (c, c), 1).astype(F32)
    diff = n - m

    def log_sigmoid(v):
        return jnp.minimum(v, 0.0) - jnp.log1p(jnp.exp(-jnp.abs(v)))

    for h in range(N_RET_HEADS):
        lg_f = jnp.broadcast_to(log_sigmoid(dl_ref[0, h])[0:1, :], (c, c))
        lg_b = jnp.broadcast_to(log_sigmoid(dl_ref[1, h])[0:1, :], (c, c))
        dm_ref[h] = QK_SCALE * jnp.where(diff >= 0, jnp.exp(lg_f * jnp.maximum(diff, 0.0)),
                                         jnp.exp(lg_b * jnp.maximum(-diff, 0.0)))
        rd_ref[h, 0, :, 0:c] = (QK_SCALE * jnp.exp(lg_f * (n + 1.0))).astype(BF16)
        rd_ref[h, 0, :, c:2 * c] = (QK_SCALE * jnp.exp(lg_b * (c - n))).astype(BF16)
        rd_ref[h, 1, :, 0:c] = jnp.exp(lg_f * (c - 1.0 - n)).astype(BF16)
        rd_ref[h, 1, :, c:2 * c] = jnp.exp(lg_b * n).astype(BF16)
        cd_ref[h, :, 0:c] = jnp.exp(lg_f[0:8, :] * float(c))
        cd_ref[h, :, c:2 * c] = jnp.exp(lg_b[0:8, :] * float(c))
    for g in range(N_FOURIER_GROUPS):
        wf = wf_ref[g]
        a = jnp.dot(cc_ref[...], wf, preferred_element_type=F32, precision=lax.Precision.HIGHEST)
        b = jnp.dot(sc_ref[...], wf, preferred_element_type=F32, precision=lax.Precision.HIGHEST)
        ab_ref[g, :, 0:FOURIER_GROUP] = a.astype(BF16)
        ab_ref[g, :, FOURIER_GROUP:2 * FOURIER_GROUP] = b.astype(BF16)


def _tables(decay_logit, w_fourier):
    c = RET_CHUNK
    dl = jnp.broadcast_to(decay_logit[:, :, None, None], (2, N_RET_HEADS, 8, 128))
    idx = np.arange(FOURIER_GROUP)
    ang = 2.0 * np.pi * ((idx[:, None] * idx[None, :]) % FOURIER_GROUP) / FOURIER_GROUP
    cc = jnp.asarray(np.cos(ang) / np.sqrt(FOURIER_GROUP), F32)
    sc = jnp.asarray(np.sin(ang) / np.sqrt(FOURIER_GROUP), F32)
    return pl.pallas_call(
        _tables_kernel,
        out_shape=(
            jax.ShapeDtypeStruct((N_RET_HEADS, c, c), F32),
            jax.ShapeDtypeStruct((N_RET_HEADS, 2, c, 2 * c), BF16),
            jax.ShapeDtypeStruct((N_RET_HEADS, 8, 2 * c), F32),
            jax.ShapeDtypeStruct((N_FOURIER_GROUPS, FOURIER_GROUP, 2 * FOURIER_GROUP), BF16),
        ),
        name="tables",
    )(dl, w_fourier, cc, sc)


def _inproj_kernel(x_ref, shift_ref, scale_ref, g_ref, w_ref, cos_ref, sin_ref, o_ref, *, rot_lo, rot_hi):
    x = x_ref[0]
    ms = jnp.mean(x * x, axis=-1, keepdims=True)
    h = x * lax.rsqrt(ms + EPS) * g_ref[...]
    h = h * (1.0 + scale_ref[0]) + shift_ref[0]
    hb = h.astype(BF16)
    n_out = w_ref.shape[1]
    hd = RET_HEAD_DIM
    bn = MXU_COLS
    for j in range(n_out // bn):
        p = jnp.dot(hb, w_ref[:, j * bn:(j + 1) * bn], preferred_element_type=F32)
        for i in range(bn // hd):
            ph = p[:, i * hd:(i + 1) * hd]
            lo = j * bn + i * hd
            if rot_lo <= lo < rot_hi:
                ph = ph * cos_ref[...] + pltpu.roll(ph, hd // 2, axis=1) * sin_ref[...]
            o_ref[0, :, lo:lo + hd] = ph.astype(BF16)


def _inproj(x, shift, scale, g_pre, w, cos_t, sin_t, *, rot_lo, rot_hi, tm):
    bsz, length, _ = x.shape
    n_out = w.shape[1]
    per_batch = shift.shape[0] > 1
    mod_map = (lambda b, t: (b, 0, 0)) if per_batch else (lambda b, t: (0, 0, 0))
    rot_map = (lambda b, t: (t, 0)) if rot_hi > rot_lo else (lambda b, t: (0, 0))
    return pl.pallas_call(
        functools.partial(_inproj_kernel, rot_lo=rot_lo, rot_hi=rot_hi),
        grid=(bsz, length // tm),
        in_specs=[
            pl.BlockSpec((1, tm, D_MODEL), lambda b, t: (b, t, 0)),
            pl.BlockSpec((1, 1, D_MODEL), mod_map),
            pl.BlockSpec((1, 1, D_MODEL), mod_map),
            pl.BlockSpec((1, D_MODEL), lambda b, t: (0, 0)),
            pl.BlockSpec((D_MODEL, n_out), lambda b, t: (0, 0)),
            pl.BlockSpec((tm, RET_HEAD_DIM), rot_map),
            pl.BlockSpec((tm, RET_HEAD_DIM), rot_map),
        ],
        out_specs=pl.BlockSpec((1, tm, n_out), lambda b, t: (b, t, 0)),
        out_shape=jax.ShapeDtypeStruct((bsz, length, n_out), BF16),
        compiler_params=pltpu.CompilerParams(
            dimension_semantics=("parallel", "parallel"), vmem_limit_bytes=VMEM_LIMIT),
        name="inproj",
    )(x, shift, scale, g_pre, w, cos_t, sin_t)


def _rotary_tables(seq_len):
    n_freq = RET_HEAD_DIM // 4
    pos = np.arange(seq_len)
    row = (pos // GRID_W).astype(np.float64)
    col = (pos % GRID_W).astype(np.float64)
    inv_freq = ROPE_BASE ** (-np.arange(n_freq, dtype=np.float64) / n_freq)
    ang_r = row[:, None] * inv_freq[None, :]
    ang_c = col[:, None] * inv_freq[None, :]
    cos_t = np.concatenate([np.cos(ang_r), np.cos(ang_c), np.cos(ang_r), np.cos(ang_c)], axis=1)
    sin_t = np.concatenate([-np.sin(ang_r), -np.sin(ang_c), np.sin(ang_r), np.sin(ang_c)], axis=1)
    return jnp.asarray(cos_t, F32), jnp.asarray(sin_t, F32)


def _permute_head_columns(w):
    d = w.shape[0]
    n_freq = RET_HEAD_DIM // 4
    w = w.reshape(d, N_RET_HEADS, 2, 2, n_freq)
    return w.transpose(0, 1, 3, 2, 4).reshape(d, N_RET_HEADS * RET_HEAD_DIM)


def _fourier_kernel(u_ref, g_ref, ab_ref, c_ref, s_ref, js_ref, o_ref, pe_ref, qo_ref, gs_ref):
    n = u_ref.shape[1]
    h = n // 2
    fb = FOLD_BLOCK
    nb = h // fb
    fg = FOURIER_GROUP
    inv_sqrt_n = 1.0 / np.sqrt(n)
    js = js_ref[...]
    first_row = lax.broadcasted_iota(jnp.int32, (fb, D_FOURIER), 0) == 0

    def channel_map(v, half):
        return jnp.concatenate(
            [jnp.dot(v[:, g * fg:(g + 1) * fg], ab_ref[g, :, half * fg:(half + 1) * fg],
                     preferred_element_type=F32) for g in range(N_FOURIER_GROUPS)], axis=1)

    for i in range(nb):
        lo = u_ref[0, i * fb:(i + 1) * fb, :].astype(F32)
        src = n - (i + 1) * fb
        t = jnp.dot(js, u_ref[0, src:src + fb, :], preferred_element_type=F32)
        if i > 0:
            t = jnp.where(first_row, u_ref[0, n - i * fb:n - i * fb + 16, :][0:1, :].astype(F32), t)
        pe_ref[i * fb:(i + 1) * fb, :] = channel_map((lo + t).astype(BF16), 0).astype(BF16)
        qo_ref[i * fb:(i + 1) * fb, :] = channel_map((lo - t).astype(BF16), 1).astype(BF16)

    p_mid = channel_map(u_ref[0, h:h + 16, :], 0)[0:1, :] * inv_sqrt_n
    sign = (1 - 2 * (lax.broadcasted_iota(jnp.int32, (h, D_FOURIER), 0) & 1)).astype(F32)
    y_mid = jnp.sum(pe_ref[...].astype(F32) * sign, axis=0, keepdims=True) * inv_sqrt_n + p_mid

    yc = jnp.dot(c_ref[...], pe_ref[...], preferred_element_type=F32) + sign * p_mid
    ys = jnp.dot(s_ref[...], qo_ref[...], preferred_element_type=F32)
    o_ref[0, 0:h, :] = ((yc - ys) * _silu(g_ref[0, 0:h, :].astype(F32))).astype(BF16)
    gs_ref[...] = (yc + ys).astype(BF16)

    for i in range(nb):
        src = h - (i + 1) * fb
        t = jnp.dot(js, gs_ref[src:src + fb, :], preferred_element_type=F32)
        edge = y_mid if i == 0 else gs_ref[h - i * fb:h - i * fb + 16, :][0:1, :].astype(F32)
        t = jnp.where(first_row, edge, t)
        rows = slice(h + i * fb, h + (i + 1) * fb)
        o_ref[0, rows, :] = (t * _silu(g_ref[0, rows, :].astype(F32))).astype(BF16)


def _fourier(proj, ab, c_mat, s_mat, js):
    bsz, length, _ = proj.shape
    h = length // 2
    return pl.pallas_call(
        _fourier_kernel,
        grid=(bsz,),
        in_specs=[
            pl.BlockSpec((1, length, D_FOURIER), lambda b: (b, 0, COL_F_IN // D_FOURIER)),
            pl.BlockSpec((1, length, D_FOURIER), lambda b: (b, 0, COL_F_GATE // D_FOURIER)),
            pl.BlockSpec((N_FOURIER_GROUPS, FOURIER_GROUP, 2 * FOURIER_GROUP), lambda b: (0, 0, 0)),
            pl.BlockSpec((h, h), lambda b: (0, 0)),
            pl.BlockSpec((h, h), lambda b: (0, 0)),
            pl.BlockSpec((FOLD_BLOCK, FOLD_BLOCK), lambda b: (0, 0)),
        ],
        out_specs=pl.BlockSpec((1, length, D_FOURIER), lambda b: (b, 0, 0)),
        out_shape=jax.ShapeDtypeStruct((bsz, length, D_FOURIER), BF16),
        scratch_shapes=[
            pltpu.VMEM((h, D_FOURIER), BF16),
            pltpu.VMEM((h, D_FOURIER), BF16),
            pltpu.VMEM((h, D_FOURIER), BF16),
        ],
        compiler_params=pltpu.CompilerParams(
            dimension_semantics=("parallel",), vmem_limit_bytes=VMEM_LIMIT),
        name="fourier",
    )(proj, proj, ab, c_mat, s_mat, js)


def _seq_dft_matrices(length):
    h = length // 2
    idx = np.arange(h)
    ang = 2.0 * np.pi * ((idx[:, None] * idx[None, :]) % length) / length
    scale = 1.0 / np.sqrt(length)
    c_mat = jnp.asarray((np.cos(ang) * scale).astype(np.float32)).astype(BF16)
    s_mat = jnp.asarray((np.sin(ang) * scale).astype(np.float32)).astype(BF16)
    r = np.arange(1, FOLD_BLOCK)
    js = np.zeros((FOLD_BLOCK, FOLD_BLOCK), np.float32)
    js[r, FOLD_BLOCK - r] = 1.0
    return c_mat, s_mat, jnp.asarray(js).astype(BF16)


def _ret_kernel(q_ref, k_ref, v_ref, rg_ref, kc_ref, vc_ref, dm_ref, rd_ref, cd_ref, gain_ref,
                o_ref, u_ref, st_ref):
    c = RET_CHUNK
    hd = RET_HEAD_DIM
    n_chunks = q_ref.shape[1] // c
    n_ctx = kc_ref.shape[1] // c

    for h in range(N_RET_HEADS):
        cols = slice(h * hd, (h + 1) * hd)
        dm = dm_ref[h]
        q_dec = rd_ref[h, 0]
        k_dec = rd_ref[h, 1]
        c_dec_f = cd_ref[h, 0:1, 0:hd]
        c_dec_b = cd_ref[h, 0:1, hd:2 * hd]
        gain = gain_ref[:, cols]

        def both_decays(t, dec):
            return jnp.concatenate([t * dec[:, 0:hd], t * dec[:, hd:2 * hd]], axis=1)

        def kv_outer(k, v):
            return lax.dot_general(v, both_decays(k, k_dec), (((0,), (0,)), ((), ())), preferred_element_type=F32)

        for j in range(n_ctx):
            rows = slice(j * c, (j + 1) * c)
            u_ref[j] = kv_outer(kc_ref[0, rows, cols], vc_ref[0, rows, cols])
        for j in range(n_chunks):
            rows = slice(j * c, (j + 1) * c)
            u_ref[n_ctx + j] = kv_outer(k_ref[0, rows, cols], v_ref[0, rows, cols])

        sf = jnp.zeros((hd, hd), F32)
        for j in range(n_ctx + n_chunks - 1):
            sf = c_dec_f * sf + u_ref[j, :, 0:hd]
            if j + 1 >= n_ctx:
                st_ref[j + 1 - n_ctx, :, 0:hd] = sf.astype(BF16)
        sb = jnp.zeros((hd, hd), F32)
        for j in reversed(range(n_ctx)):
            sb = c_dec_b * sb + u_ref[j, :, hd:2 * hd]
        for j in reversed(range(n_chunks)):
            st_ref[j, :, hd:2 * hd] = sb.astype(BF16)
            if j > 0:
                sb = c_dec_b * sb + u_ref[n_ctx + j, :, hd:2 * hd]

        for j in range(n_chunks):
            rows = slice(j * c, (j + 1) * c)
            q = q_ref[0, rows, cols]
            s = lax.dot_general(q, k_ref[0, rows, cols], (((1,), (1,)), ((), ())), preferred_element_type=F32)
            o = jnp.dot((s * dm).astype(BF16), v_ref[0, rows, cols], preferred_element_type=F32)
            o += lax.dot_general(both_decays(q, q_dec), st_ref[j], (((1,), (1,)), ((), ())),
                                 preferred_element_type=F32)
            mu = jnp.mean(o, axis=-1, keepdims=True)
            d = o - mu
            var = jnp.mean(d * d, axis=-1, keepdims=True)
            on = d * lax.rsqrt(var + EPS)
            o_ref[0, rows, cols] = (on * gain * _silu(rg_ref[0, rows, cols].astype(F32))).astype(BF16)


def _retention(proj, proj_c, dm, rd, cd, gain):
    bsz, length, _ = proj.shape
    ctx_len = proj_c.shape[1]
    hd = RET_HEAD_DIM
    c = RET_CHUNK
    n_chunks = length // c
    n_ctx = ctx_len // c
    assert n_ctx >= 1

    def col(base):
        return lambda b: (b, 0, base // D_RET)

    return pl.pallas_call(
        _ret_kernel,
        grid=(bsz,),
        in_specs=[
            pl.BlockSpec((1, length, D_RET), col(COL_Q)),
            pl.BlockSpec((1, length, D_RET), col(COL_K)),
            pl.BlockSpec((1, length, D_RET), col(COL_V)),
            pl.BlockSpec((1, length, D_RET), col(COL_R_GATE)),
            pl.BlockSpec((1, ctx_len, D_RET), col(0)),
            pl.BlockSpec((1, ctx_len, D_RET), col(D_RET)),
            pl.BlockSpec((N_RET_HEADS, c, c), lambda b: (0, 0, 0)),
            pl.BlockSpec((N_RET_HEADS, 2, c, 2 * hd), lambda b: (0, 0, 0, 0)),
            pl.BlockSpec((N_RET_HEADS, 8, 2 * hd), lambda b: (0, 0, 0)),
            pl.BlockSpec((1, D_RET), lambda b: (0, 0)),
        ],
        out_specs=pl.BlockSpec((1, length, D_RET), lambda b: (b, 0, 0)),
        out_shape=jax.ShapeDtypeStruct((bsz, length, D_RET), BF16),
        scratch_shapes=[
            pltpu.VMEM((n_ctx + n_chunks, hd, 2 * hd), F32),
            pltpu.VMEM((n_chunks, hd, 2 * hd), BF16),
        ],
        compiler_params=pltpu.CompilerParams(
            dimension_semantics=("parallel",), vmem_limit_bytes=VMEM_LIMIT),
        name="retention",
    )(proj, proj, proj, proj, proj_c, proj_c, dm, rd, cd, gain)


def _outproj_kernel(yf_ref, yr_ref, w_ref, x_ref, gate_ref, g_ref, o_ref):
    y = jnp.dot(yf_ref[0], w_ref[0:D_FOURIER, :], preferred_element_type=F32)
    y += jnp.dot(yr_ref[0], w_ref[D_FOURIER:D_MIX, :], preferred_element_type=F32)
    ms = jnp.mean(y * y, axis=-1, keepdims=True)
    yn = y * lax.rsqrt(ms + EPS) * g_ref[...]
    o_ref[0] = x_ref[0] + gate_ref[0] * yn


def _outproj(yf, yr, w_out, x, gate, g_post, *, tm):
    bsz, length, _ = x.shape
    return pl.pallas_call(
        _outproj_kernel,
        grid=(bsz, length // tm),
        in_specs=[
            pl.BlockSpec((1, tm, D_FOURIER), lambda b, t: (b, t, 0)),
            pl.BlockSpec((1, tm, D_RET), lambda b, t: (b, t, 0)),
            pl.BlockSpec((D_MIX, D_MODEL), lambda b, t: (0, 0)),
            pl.BlockSpec((1, tm, D_MODEL), lambda b, t: (b, t, 0)),
            pl.BlockSpec((1, 1, D_MODEL), lambda b, t: (b, 0, 0)),
            pl.BlockSpec((1, D_MODEL), lambda b, t: (0, 0)),
        ],
        out_specs=pl.BlockSpec((1, tm, D_MODEL), lambda b, t: (b, t, 0)),
        out_shape=jax.ShapeDtypeStruct((bsz, length, D_MODEL), F32),
        compiler_params=pltpu.CompilerParams(
            dimension_semantics=("parallel", "parallel"), vmem_limit_bytes=VMEM_LIMIT),
        name="outproj",
    )(yf, yr, w_out, x, gate, g_post)


def kernel(x, c, ctx, c_ctx, w_ada, b_ada, g_pre, g_post, w_in, w_fourier, decay_logit, ret_gn_gain, w_out):
    bsz, seq_len, _ = x.shape
    ctx_len = ctx.shape[1]
    assert w_ada.shape[0] == 1, "single layer only"
    assert bsz + 1 <= ADA_ROWS

    cvec = jnp.concatenate([c, c_ctx[None, :], jnp.zeros((ADA_ROWS - bsz - 1, D_MODEL), F32)], axis=0)
    mod = _adaln(cvec, w_ada[0], b_ada[0])
    shift = mod[:bsz, 0:D_MODEL].reshape(bsz, 1, D_MODEL)
    scale = mod[:bsz, D_MODEL:2 * D_MODEL].reshape(bsz, 1, D_MODEL)
    gate = mod[:bsz, 2 * D_MODEL:].reshape(bsz, 1, D_MODEL)
    shift_c = mod[bsz:bsz + 1, 0:D_MODEL].reshape(1, 1, D_MODEL)
    scale_c = mod[bsz:bsz + 1, D_MODEL:2 * D_MODEL].reshape(1, 1, D_MODEL)

    dm, rd, cd, ab = _tables(decay_logit[0], w_fourier[0])

    w = w_in[0]
    w_perm = jnp.concatenate([
        w[:, :COL_Q],
        _permute_head_columns(w[:, COL_Q:COL_K]),
        _permute_head_columns(w[:, COL_K:COL_V]),
        w[:, COL_V:],
    ], axis=1).astype(BF16)
    cos_t, sin_t = _rotary_tables(seq_len)
    g_pre2 = g_pre[0].reshape(1, D_MODEL)

    proj = _inproj(x, shift, scale, g_pre2, w_perm, cos_t, sin_t,
                   rot_lo=COL_Q, rot_hi=COL_V, tm=TOKEN_TILE)
    ctx_rows = bsz * ctx_len
    proj_c = _inproj(ctx.reshape(1, ctx_rows, D_MODEL), shift_c, scale_c, g_pre2, w_perm[:, COL_K:COL_R_GATE],
                     cos_t, sin_t, rot_lo=0, rot_hi=0, tm=min(TOKEN_TILE, ctx_rows))
    proj_c = proj_c.reshape(bsz, ctx_len, 2 * D_RET)

    assert (seq_len // 2) % FOLD_BLOCK == 0
    yf = _fourier(proj, ab, *_seq_dft_matrices(seq_len))
    yr = _retention(proj, proj_c, dm, rd, cd, ret_gn_gain[0].reshape(1, D_RET))
    return _outproj(yf, yr, w_out[0].astype(BF16), x, gate, g_post[0].reshape(1, D_MODEL), tm=TOKEN_TILE)
```

```python
import functools

import numpy as np
import jax
import jax.numpy as jnp
from jax import lax
from jax.experimental import pallas as pl
from jax.experimental.pallas import tpu as pltpu

D_MODEL = 1024
GRID_W = 64
D_FOURIER = 512
N_FOURIER_GROUPS = 4
FOURIER_GROUP = D_FOURIER // N_FOURIER_GROUPS
D_RET = 512
N_RET_HEADS = 4
RET_HEAD_DIM = D_RET // N_RET_HEADS
D_MIX = D_FOURIER + D_RET
D_IN = 2 * D_FOURIER + 4 * D_RET
RET_CHUNK = 128
ROPE_BASE = 10000.0
QK_SCALE = RET_HEAD_DIM ** -0.5
EPS = 1e-6

COL_F_IN = 0
COL_F_GATE = D_FOURIER
COL_Q = 2 * D_FOURIER
COL_K = COL_Q + D_RET
COL_V = COL_K + D_RET
COL_R_GATE = COL_V + D_RET

MXU_COLS = 256
FOLD_BLOCK = 256
FOURIER_ROW_BLOCK = 512
ADA_ROWS = 24
TOKEN_TILE = 1024
VMEM_LIMIT = 56 * 1024 * 1024

F32 = jnp.float32
BF16 = jnp.bfloat16


def _silu(v):
    return v * jax.nn.sigmoid(v)


def _adaln_kernel(cv_ref, w_ref, b_ref, o_ref):
    s = _silu(cv_ref[...])
    o_ref[...] = jnp.dot(s, w_ref[...], preferred_element_type=F32) + b_ref[...]


def _adaln(cvec, w_ada, b_ada):
    n_out = w_ada.shape[1]
    bn = D_MODEL
    return pl.pallas_call(
        _adaln_kernel,
        grid=(n_out // bn,),
        in_specs=[
            pl.BlockSpec((ADA_ROWS, D_MODEL), lambda j: (0, 0)),
            pl.BlockSpec((D_MODEL, bn), lambda j: (0, j)),
            pl.BlockSpec((1, bn), lambda j: (0, j)),
        ],
        out_specs=pl.BlockSpec((ADA_ROWS, bn), lambda j: (0, j)),
        out_shape=jax.ShapeDtypeStruct((ADA_ROWS, n_out), F32),
        name="adaln",
    )(cvec, w_ada, b_ada.reshape(1, n_out))


def _tables_kernel(dl_ref, wf_ref, cc_ref, sc_ref, dm_ref, rd_ref, cd_ref, ab_ref):
    c = RET_CHUNK
    n = lax.broadcasted_iota(jnp.int32, (c, c), 0).astype(F32)
    m = lax.broadcasted_iota(jnp.int32, (c, c), 1).astype(F32)
    diff = n - m

    def log_sigmoid(v):
        return jnp.minimum(v, 0.0) - jnp.log1p(jnp.exp(-jnp.abs(v)))

    for h in range(N_RET_HEADS):
        lg_f = jnp.broadcast_to(log_sigmoid(dl_ref[0, h])[0:1, :], (c, c))
        lg_b = jnp.broadcast_to(log_sigmoid(dl_ref[1, h])[0:1, :], (c, c))
        dm_ref[h] = QK_SCALE * jnp.where(diff >= 0, jnp.exp(lg_f * jnp.maximum(diff, 0.0)),
                                         jnp.exp(lg_b * jnp.maximum(-diff, 0.0)))
        rd_ref[h, 0, :, 0:c] = (QK_SCALE * jnp.exp(lg_f * (n + 1.0))).astype(BF16)
        rd_ref[h, 0, :, c:2 * c] = (QK_SCALE * jnp.exp(lg_b * (c - n))).astype(BF16)
        rd_ref[h, 1, :, 0:c] = jnp.exp(lg_f * (c - 1.0 - n)).astype(BF16)
        rd_ref[h, 1, :, c:2 * c] = jnp.exp(lg_b * n).astype(BF16)
        cd_ref[h, :, 0:c] = jnp.exp(lg_f[0:8, :] * float(c))
        cd_ref[h, :, c:2 * c] = jnp.exp(lg_b[0:8, :] * float(c))
    fg = FOURIER_GROUP
    ab_ref[...] = jnp.zeros(ab_ref.shape, BF16)
    for g in range(N_FOURIER_GROUPS):
        wf = wf_ref[g]
        a = jnp.dot(cc_ref[...], wf, preferred_element_type=F32, precision=lax.Precision.HIGHEST)
        b = jnp.dot(sc_ref[...], wf, preferred_element_type=F32, precision=lax.Precision.HIGHEST)
        pair, off = g // 2, (g % 2) * fg
        ab_ref[0, pair, off:off + fg, off:off + fg] = a.astype(BF16)
        ab_ref[1, pair, off:off + fg, off:off + fg] = b.astype(BF16)


def _tables(decay_logit, w_fourier):
    c = RET_CHUNK
    dl = jnp.broadcast_to(decay_logit[:, :, None, None], (2, N_RET_HEADS, 8, 128))
    idx = np.arange(FOURIER_GROUP)
    ang = 2.0 * np.pi * ((idx[:, None] * idx[None, :]) % FOURIER_GROUP) / FOURIER_GROUP
    cc = jnp.asarray(np.cos(ang) / np.sqrt(FOURIER_GROUP), F32)
    sc = jnp.asarray(np.sin(ang) / np.sqrt(FOURIER_GROUP), F32)
    return pl.pallas_call(
        _tables_kernel,
        out_shape=(
            jax.ShapeDtypeStruct((N_RET_HEADS, c, c), F32),
            jax.ShapeDtypeStruct((N_RET_HEADS, 2, c, 2 * c), BF16),
            jax.ShapeDtypeStruct((N_RET_HEADS, 8, 2 * c), F32),
            jax.ShapeDtypeStruct((2, N_FOURIER_GROUPS // 2, 2 * FOURIER_GROUP, 2 * FOURIER_GROUP), BF16),
        ),
        name="tables",
    )(dl, w_fourier, cc, sc)


def _inproj_kernel(x_ref, shift_ref, scale_ref, g_ref, w_ref, cos_ref, sin_ref, o_ref, *, rot_lo, rot_hi, silu_cols):
    x = x_ref[0]
    ms = jnp.mean(x * x, axis=-1, keepdims=True)
    h = x * lax.rsqrt(ms + EPS) * g_ref[...]
    h = h * (1.0 + scale_ref[0]) + shift_ref[0]
    hb = h.astype(BF16)
    n_out = w_ref.shape[1]
    hd = RET_HEAD_DIM
    bn = MXU_COLS
    is_u1 = (lax.broadcasted_iota(jnp.int32, (x.shape[0], hd), 1) & (hd // 4)) == 0
    def epilogue_weight(j):
        lo = j * bn
        return (lo in silu_cols) + (rot_lo <= lo < rot_hi)

    for j in sorted(range(n_out // bn), key=epilogue_weight, reverse=True):
        p = jnp.dot(hb, w_ref[:, j * bn:(j + 1) * bn], preferred_element_type=F32)
        for i in range(bn // hd):
            ph = p[:, i * hd:(i + 1) * hd]
            lo = j * bn + i * hd
            if lo in silu_cols:
                ph = _silu(ph)
            if rot_lo <= lo < rot_hi:
                partner = jnp.where(is_u1, pltpu.roll(ph, hd - hd // 4, axis=1), pltpu.roll(ph, hd // 4, axis=1))
                ph = ph * cos_ref[...] + partner * sin_ref[...]
            o_ref[0, :, lo:lo + hd] = ph.astype(BF16)


def _inproj(x, shift, scale, g_pre, w, cos_t, sin_t, *, rot_lo, rot_hi, silu_cols, tm):
    bsz, length, _ = x.shape
    n_out = w.shape[1]
    per_batch = shift.shape[0] > 1
    mod_map = (lambda b, t: (b, 0, 0)) if per_batch else (lambda b, t: (0, 0, 0))
    rot_map = (lambda b, t: (t, 0)) if rot_hi > rot_lo else (lambda b, t: (0, 0))
    return pl.pallas_call(
        functools.partial(_inproj_kernel, rot_lo=rot_lo, rot_hi=rot_hi, silu_cols=frozenset(silu_cols)),
        grid=(bsz, length // tm),
        in_specs=[
            pl.BlockSpec((1, tm, D_MODEL), lambda b, t: (b, t, 0)),
            pl.BlockSpec((1, 1, D_MODEL), mod_map),
            pl.BlockSpec((1, 1, D_MODEL), mod_map),
            pl.BlockSpec((1, D_MODEL), lambda b, t: (0, 0)),
            pl.BlockSpec((D_MODEL, n_out), lambda b, t: (0, 0)),
            pl.BlockSpec((tm, RET_HEAD_DIM), rot_map),
            pl.BlockSpec((tm, RET_HEAD_DIM), rot_map),
        ],
        out_specs=pl.BlockSpec((1, tm, n_out), lambda b, t: (b, t, 0)),
        out_shape=jax.ShapeDtypeStruct((bsz, length, n_out), BF16),
        compiler_params=pltpu.CompilerParams(
            dimension_semantics=("parallel", "parallel"), vmem_limit_bytes=VMEM_LIMIT),
        name="inproj",
    )(x, shift, scale, g_pre, w, cos_t, sin_t)


def _rotary_tables(seq_len):
    n_freq = RET_HEAD_DIM // 4
    pos = np.arange(seq_len)
    row = (pos // GRID_W).astype(np.float64)
    col = (pos % GRID_W).astype(np.float64)
    inv_freq = ROPE_BASE ** (-np.arange(n_freq, dtype=np.float64) / n_freq)
    ang_r = row[:, None] * inv_freq[None, :]
    ang_c = col[:, None] * inv_freq[None, :]
    cos_t = np.concatenate([np.cos(ang_r), np.cos(ang_r), np.cos(ang_c), np.cos(ang_c)], axis=1)
    sin_t = np.concatenate([-np.sin(ang_r), np.sin(ang_r), -np.sin(ang_c), np.sin(ang_c)], axis=1)
    return jnp.asarray(cos_t, F32), jnp.asarray(sin_t, F32)


def _fourier_kernel(u_ref, g_ref, ab_ref, c_ref, s_ref, js_ref, o_ref, pe_ref, qo_ref):
    n = u_ref.shape[1]
    h = n // 2
    fb = FOLD_BLOCK
    nb = h // fb
    fg = FOURIER_GROUP
    inv_sqrt_n = 1.0 / np.sqrt(n)
    js = js_ref[...]
    first_row = lax.broadcasted_iota(jnp.int32, (fb, D_FOURIER), 0) == 0

    def channel_map(v, half):
        return jnp.concatenate(
            [jnp.dot(v[:, p * 2 * fg:(p + 1) * 2 * fg], ab_ref[half, p], preferred_element_type=F32)
             for p in range(N_FOURIER_GROUPS // 2)], axis=1)

    for i in range(nb):
        lo = u_ref[0, i * fb:(i + 1) * fb, :].astype(F32)
        src = n - (i + 1) * fb
        t = jnp.dot(js, u_ref[0, src:src + fb, :], preferred_element_type=F32)
        if i > 0:
            t = jnp.where(first_row, u_ref[0, n - i * fb:n - i * fb + 16, :][0:1, :].astype(F32), t)
        pe_ref[i * fb:(i + 1) * fb, :] = channel_map((lo + t).astype(BF16), 0).astype(BF16)
        qo_ref[i * fb:(i + 1) * fb, :] = channel_map((lo - t).astype(BF16), 1).astype(BF16)

    p_mid = channel_map(u_ref[0, h:h + 16, :], 0)[0:1, :] * inv_sqrt_n
    sign = (1 - 2 * (lax.broadcasted_iota(jnp.int32, (h, D_FOURIER), 0) & 1)).astype(F32)
    y_mid = jnp.sum(pe_ref[...].astype(F32) * sign, axis=0, keepdims=True) * inv_sqrt_n + p_mid

    mb = FOURIER_ROW_BLOCK
    sign_mb = (1 - 2 * (lax.broadcasted_iota(jnp.int32, (mb, D_FOURIER), 0) & 1)).astype(F32)
    edge = y_mid
    for i in range(h // mb):
        k0 = h - (i + 1) * mb
        yc = jnp.dot(c_ref[k0:k0 + mb, :], pe_ref[...], preferred_element_type=F32) + sign_mb * p_mid
        ys = jnp.dot(s_ref[k0:k0 + mb, :], qo_ref[...], preferred_element_type=F32)
        o_ref[0, k0:k0 + mb, :] = ((yc - ys) * g_ref[0, k0:k0 + mb, :].astype(F32)).astype(BF16)
        gsum = yc + ys
        for j in range(mb // fb):
            src = mb - (j + 1) * fb
            t = jnp.dot(js, gsum[src:src + fb, :].astype(BF16), preferred_element_type=F32)
            t = jnp.where(first_row, edge, t)
            edge = gsum[src:src + 8, :].astype(BF16).astype(F32)[0:1, :]
            r0 = n - k0 - mb + j * fb
            o_ref[0, r0:r0 + fb, :] = (t * g_ref[0, r0:r0 + fb, :].astype(F32)).astype(BF16)


def _fourier(proj, ab, c_mat, s_mat, js):
    bsz, length, _ = proj.shape
    h = length // 2
    return pl.pallas_call(
        _fourier_kernel,
        grid=(bsz,),
        in_specs=[
            pl.BlockSpec((1, length, D_FOURIER), lambda b: (b, 0, COL_F_IN // D_FOURIER)),
            pl.BlockSpec((1, length, D_FOURIER), lambda b: (b, 0, COL_F_GATE // D_FOURIER)),
            pl.BlockSpec((2, N_FOURIER_GROUPS // 2, 2 * FOURIER_GROUP, 2 * FOURIER_GROUP), lambda b: (0, 0, 0, 0)),
            pl.BlockSpec((h, h), lambda b: (0, 0)),
            pl.BlockSpec((h, h), lambda b: (0, 0)),
            pl.BlockSpec((FOLD_BLOCK, FOLD_BLOCK), lambda b: (0, 0)),
        ],
        out_specs=pl.BlockSpec((1, length, D_FOURIER), lambda b: (b, 0, 0)),
        out_shape=jax.ShapeDtypeStruct((bsz, length, D_FOURIER), BF16),
        scratch_shapes=[
            pltpu.VMEM((h, D_FOURIER), BF16),
            pltpu.VMEM((h, D_FOURIER), BF16),
        ],
        compiler_params=pltpu.CompilerParams(
            dimension_semantics=("parallel",), vmem_limit_bytes=VMEM_LIMIT),
        name="fourier",
    )(proj, proj, ab, c_mat, s_mat, js)


def _seq_dft_matrices(length):
    h = length // 2
    idx = np.arange(h)
    ang = 2.0 * np.pi * ((idx[:, None] * idx[None, :]) % length) / length
    scale = 1.0 / np.sqrt(length)
    c_mat = jnp.asarray((np.cos(ang) * scale).astype(np.float32)).astype(BF16)
    s_mat = jnp.asarray((np.sin(ang) * scale).astype(np.float32)).astype(BF16)
    r = np.arange(1, FOLD_BLOCK)
    js = np.zeros((FOLD_BLOCK, FOLD_BLOCK), np.float32)
    js[r, FOLD_BLOCK - r] = 1.0
    return c_mat, s_mat, jnp.asarray(js).astype(BF16)


def _ret_kernel(q_ref, k_ref, v_ref, rg_ref, kc_ref, vc_ref, dm_ref, rd_ref, cd_ref, gain_ref,
                o_ref, u_ref, st_ref):
    c = RET_CHUNK
    hd = RET_HEAD_DIM
    n_chunks = q_ref.shape[1] // c
    n_ctx = kc_ref.shape[1] // c

    for h in range(N_RET_HEADS):
        cols = slice(h * hd, (h + 1) * hd)
        dm = dm_ref[h]
        q_dec = rd_ref[h, 0]
        k_dec = rd_ref[h, 1]
        c_dec_f = cd_ref[h, 0:1, 0:hd]
        c_dec_b = cd_ref[h, 0:1, hd:2 * hd]
        gain = gain_ref[:, cols]

        def both_decays(t, dec):
            return jnp.concatenate([t * dec[:, 0:hd], t * dec[:, hd:2 * hd]], axis=1)

        def kv_outer(k, v):
            return lax.dot_general(v, both_decays(k, k_dec), (((0,), (0,)), ((), ())), preferred_element_type=F32)

        for j in range(n_ctx):
            rows = slice(j * c, (j + 1) * c)
            u_ref[j] = kv_outer(kc_ref[0, rows, cols], vc_ref[0, rows, cols])
        for j in range(n_chunks):
            rows = slice(j * c, (j + 1) * c)
            u_ref[n_ctx + j] = kv_outer(k_ref[0, rows, cols], v_ref[0, rows, cols])

        sf = jnp.zeros((hd, hd), F32)
        for j in range(n_ctx + n_chunks - 1):
            sf = c_dec_f * sf + u_ref[j, :, 0:hd]
            if j + 1 >= n_ctx:
                st_ref[j + 1 - n_ctx, :, 0:hd] = sf.astype(BF16)
        sb = jnp.zeros((hd, hd), F32)
        for j in reversed(range(n_ctx)):
            sb = c_dec_b * sb + u_ref[j, :, hd:2 * hd]
        for j in reversed(range(n_chunks)):
            st_ref[j, :, hd:2 * hd] = sb.astype(BF16)
            if j > 0:
                sb = c_dec_b * sb + u_ref[n_ctx + j, :, hd:2 * hd]

        for j in range(n_chunks):
            rows = slice(j * c, (j + 1) * c)
            q = q_ref[0, rows, cols]
            s = lax.dot_general(q, k_ref[0, rows, cols], (((1,), (1,)), ((), ())), preferred_element_type=F32)
            o = jnp.dot((s * dm).astype(BF16), v_ref[0, rows, cols], preferred_element_type=F32)
            o += lax.dot_general(both_decays(q, q_dec), st_ref[j], (((1,), (1,)), ((), ())),
                                 preferred_element_type=F32)
            mu = jnp.mean(o, axis=-1, keepdims=True)
            d = o - mu
            var = jnp.mean(d * d, axis=-1, keepdims=True)
            on = d * lax.rsqrt(var + EPS)
            o_ref[0, rows, cols] = (on * gain * rg_ref[0, rows, cols].astype(F32)).astype(BF16)


def _retention(proj, proj_c, dm, rd, cd, gain):
    bsz, length, _ = proj.shape
    ctx_len = proj_c.shape[1]
    hd = RET_HEAD_DIM
    c = RET_CHUNK
    n_chunks = length // c
    n_ctx = ctx_len // c
    assert n_ctx >= 1

    def col(base):
        return lambda b: (b, 0, base // D_RET)

    return pl.pallas_call(
        _ret_kernel,
        grid=(bsz,),
        in_specs=[
            pl.BlockSpec((1, length, D_RET), col(COL_Q)),
            pl.BlockSpec((1, length, D_RET), col(COL_K)),
            pl.BlockSpec((1, length, D_RET), col(COL_V)),
            pl.BlockSpec((1, length, D_RET), col(COL_R_GATE)),
            pl.BlockSpec((1, ctx_len, D_RET), col(0)),
            pl.BlockSpec((1, ctx_len, D_RET), col(D_RET)),
            pl.BlockSpec((N_RET_HEADS, c, c), lambda b: (0, 0, 0)),
            pl.BlockSpec((N_RET_HEADS, 2, c, 2 * hd), lambda b: (0, 0, 0, 0)),
            pl.BlockSpec((N_RET_HEADS, 8, 2 * hd), lambda b: (0, 0, 0)),
            pl.BlockSpec((1, D_RET), lambda b: (0, 0)),
        ],
        out_specs=pl.BlockSpec((1, length, D_RET), lambda b: (b, 0, 0)),
        out_shape=jax.ShapeDtypeStruct((bsz, length, D_RET), BF16),
        scratch_shapes=[
            pltpu.VMEM((n_ctx + n_chunks, hd, 2 * hd), F32),
            pltpu.VMEM((n_chunks, hd, 2 * hd), BF16),
        ],
        compiler_params=pltpu.CompilerParams(
            dimension_semantics=("parallel",), vmem_limit_bytes=VMEM_LIMIT),
        name="retention",
    )(proj, proj, proj, proj, proj_c, proj_c, dm, rd, cd, gain)


def _outproj_kernel(yf_ref, yr_ref, w_ref, x_ref, gate_ref, g_ref, o_ref):
    y = jnp.dot(yf_ref[0], w_ref[0:D_FOURIER, :], preferred_element_type=F32)
    y += jnp.dot(yr_ref[0], w_ref[D_FOURIER:D_MIX, :], preferred_element_type=F32)
    ms = jnp.mean(y * y, axis=-1, keepdims=True)
    yn = y * lax.rsqrt(ms + EPS) * g_ref[...]
    o_ref[0] = x_ref[0] + gate_ref[0] * yn


def _outproj(yf, yr, w_out, x, gate, g_post, *, tm):
    bsz, length, _ = x.shape
    return pl.pallas_call(
        _outproj_kernel,
        grid=(bsz, length // tm),
        in_specs=[
            pl.BlockSpec((1, tm, D_FOURIER), lambda b, t: (b, t, 0)),
            pl.BlockSpec((1, tm, D_RET), lambda b, t: (b, t, 0)),
            pl.BlockSpec((D_MIX, D_MODEL), lambda b, t: (0, 0)),
            pl.BlockSpec((1, tm, D_MODEL), lambda b, t: (b, t, 0)),
            pl.BlockSpec((1, 1, D_MODEL), lambda b, t: (b, 0, 0)),
            pl.BlockSpec((1, D_MODEL), lambda b, t: (0, 0)),
        ],
        out_specs=pl.BlockSpec((1, tm, D_MODEL), lambda b, t: (b, t, 0)),
        out_shape=jax.ShapeDtypeStruct((bsz, length, D_MODEL), F32),
        compiler_params=pltpu.CompilerParams(
            dimension_semantics=("parallel", "parallel"), vmem_limit_bytes=VMEM_LIMIT),
        name="outproj",
    )(yf, yr, w_out, x, gate, g_post)


def kernel(x, c, ctx, c_ctx, w_ada, b_ada, g_pre, g_post, w_in, w_fourier, decay_logit, ret_gn_gain, w_out):
    bsz, seq_len, _ = x.shape
    ctx_len = ctx.shape[1]
    assert w_ada.shape[0] == 1, "single layer only"
    assert bsz + 1 <= ADA_ROWS

    cvec = jnp.concatenate([c, c_ctx[None, :], jnp.zeros((ADA_ROWS - bsz - 1, D_MODEL), F32)], axis=0)
    mod = _adaln(cvec, w_ada[0], b_ada[0])
    shift = mod[:bsz, 0:D_MODEL].reshape(bsz, 1, D_MODEL)
    scale = mod[:bsz, D_MODEL:2 * D_MODEL].reshape(bsz, 1, D_MODEL)
    gate = mod[:bsz, 2 * D_MODEL:].reshape(bsz, 1, D_MODEL)
    shift_c = mod[bsz:bsz + 1, 0:D_MODEL].reshape(1, 1, D_MODEL)
    scale_c = mod[bsz:bsz + 1, D_MODEL:2 * D_MODEL].reshape(1, 1, D_MODEL)

    dm, rd, cd, ab = _tables(decay_logit[0], w_fourier[0])

    w_bf = w_in[0].astype(BF16)
    cos_t, sin_t = _rotary_tables(seq_len)
    g_pre2 = g_pre[0].reshape(1, D_MODEL)

    gate_cols = list(range(COL_F_GATE, COL_Q, RET_HEAD_DIM)) + list(range(COL_R_GATE, D_IN, RET_HEAD_DIM))
    proj = _inproj(x, shift, scale, g_pre2, w_bf, cos_t, sin_t,
                   rot_lo=COL_Q, rot_hi=COL_V, silu_cols=gate_cols, tm=TOKEN_TILE)
    ctx_rows = bsz * ctx_len
    proj_c = _inproj(ctx.reshape(1, ctx_rows, D_MODEL), shift_c, scale_c, g_pre2, w_bf[:, COL_K:COL_R_GATE],
                     cos_t, sin_t, rot_lo=0, rot_hi=0, silu_cols=(), tm=min(TOKEN_TILE, ctx_rows))
    proj_c = proj_c.reshape(bsz, ctx_len, 2 * D_RET)

    assert (seq_len // 2) % FOLD_BLOCK == 0
    yf = _fourier(proj, ab, *_seq_dft_matrices(seq_len))
    yr = _retention(proj, proj_c, dm, rd, cd, ret_gn_gain[0].reshape(1, D_RET))
    return _outproj(yf, yr, w_out[0].astype(BF16), x, gate, g_post[0].reshape(1, D_MODEL), tm=TOKEN_TILE)
```

```python
import functools

import numpy as np
import jax
import jax.numpy as jnp
from jax import lax
from jax.experimental import pallas as pl
from jax.experimental.pallas import tpu as pltpu

D_MODEL = 1024
GRID_W = 64
D_FOURIER = 512
N_FOURIER_GROUPS = 4
FOURIER_GROUP = D_FOURIER // N_FOURIER_GROUPS
D_RET = 512
N_RET_HEADS = 4
RET_HEAD_DIM = D_RET // N_RET_HEADS
D_MIX = D_FOURIER + D_RET
D_IN = 2 * D_FOURIER + 4 * D_RET
RET_CHUNK = 128
ROPE_BASE = 10000.0
QK_SCALE = RET_HEAD_DIM ** -0.5
EPS = 1e-6

COL_F_IN = 0
COL_F_GATE = D_FOURIER
COL_Q = 2 * D_FOURIER
COL_K = COL_Q + D_RET
COL_V = COL_K + D_RET
COL_R_GATE = COL_V + D_RET

MXU_COLS = 256
FOLD_BLOCK = 256
FOURIER_ROW_BLOCK = 512
ADA_ROWS = 24
TOKEN_TILE = 1024
VMEM_LIMIT = 56 * 1024 * 1024

F32 = jnp.float32
BF16 = jnp.bfloat16


def _silu(v):
    return v * jax.nn.sigmoid(v)


def _adaln_kernel(cv_ref, w_ref, b_ref, o_ref):
    s = _silu(cv_ref[...])
    o_ref[...] = jnp.dot(s, w_ref[...], preferred_element_type=F32) + b_ref[...]


def _adaln(cvec, w_ada, b_ada):
    n_out = w_ada.shape[1]
    bn = D_MODEL
    return pl.pallas_call(
        _adaln_kernel,
        grid=(n_out // bn,),
        in_specs=[
            pl.BlockSpec((ADA_ROWS, D_MODEL), lambda j: (0, 0)),
            pl.BlockSpec((D_MODEL, bn), lambda j: (0, j)),
            pl.BlockSpec((1, bn), lambda j: (0, j)),
        ],
        out_specs=pl.BlockSpec((ADA_ROWS, bn), lambda j: (0, j)),
        out_shape=jax.ShapeDtypeStruct((ADA_ROWS, n_out), F32),
        name="adaln",
    )(cvec, w_ada, b_ada.reshape(1, n_out))


def _tables_kernel(dl_ref, wf_ref, cc_ref, sc_ref, dm_ref, rd_ref, cd_ref, ab_ref):
    c = RET_CHUNK
    n = lax.broadcasted_iota(jnp.int32, (c, c), 0).astype(F32)
    m = lax.broadcasted_iota(jnp.int32, (c, c), 1).astype(F32)
    diff = n - m

    def log_sigmoid(v):
        return jnp.minimum(v, 0.0) - jnp.log1p(jnp.exp(-jnp.abs(v)))

    for h in range(N_RET_HEADS):
        lg_f = jnp.broadcast_to(log_sigmoid(dl_ref[0, h])[0:1, :], (c, c))
        lg_b = jnp.broadcast_to(log_sigmoid(dl_ref[1, h])[0:1, :], (c, c))
        dm_ref[h] = QK_SCALE * jnp.where(diff >= 0, jnp.exp(lg_f * jnp.maximum(diff, 0.0)),
                                         jnp.exp(lg_b * jnp.maximum(-diff, 0.0)))
        rd_ref[h, 0, :, 0:c] = (QK_SCALE * jnp.exp(lg_f * (n + 1.0))).astype(BF16)
        rd_ref[h, 0, :, c:2 * c] = (QK_SCALE * jnp.exp(lg_b * (c - n))).astype(BF16)
        rd_ref[h, 1, :, 0:c] = jnp.exp(lg_f * (c - 1.0 - n)).astype(BF16)
        rd_ref[h, 1, :, c:2 * c] = jnp.exp(lg_b * n).astype(BF16)
        cd_ref[h, :, 0:c] = jnp.exp(lg_f[0:8, :] * float(c))
        cd_ref[h, :, c:2 * c] = jnp.exp(lg_b[0:8, :] * float(c))
    fg = FOURIER_GROUP
    ab_ref[...] = jnp.zeros(ab_ref.shape, BF16)
    for g in range(N_FOURIER_GROUPS):
        wf = wf_ref[g]
        a = jnp.dot(cc_ref[...], wf, preferred_element_type=F32, precision=lax.Precision.HIGHEST)
        b = jnp.dot(sc_ref[...], wf, preferred_element_type=F32, precision=lax.Precision.HIGHEST)
        pair, off = g // 2, (g % 2) * fg
        ab_ref[0, pair, off:off + fg, off:off + fg] = a.astype(BF16)
        ab_ref[1, pair, off:off + fg, off:off + fg] = b.astype(BF16)


def _tables(decay_logit, w_fourier):
    c = RET_CHUNK
    dl = jnp.broadcast_to(decay_logit[:, :, None, None], (2, N_RET_HEADS, 8, 128))
    idx = np.arange(FOURIER_GROUP)
    ang = 2.0 * np.pi * ((idx[:, None] * idx[None, :]) % FOURIER_GROUP) / FOURIER_GROUP
    cc = jnp.asarray(np.cos(ang) / np.sqrt(FOURIER_GROUP), F32)
    sc = jnp.asarray(np.sin(ang) / np.sqrt(FOURIER_GROUP), F32)
    return pl.pallas_call(
        _tables_kernel,
        out_shape=(
            jax.ShapeDtypeStruct((N_RET_HEADS, c, c), F32),
            jax.ShapeDtypeStruct((N_RET_HEADS, 2, c, 2 * c), BF16),
            jax.ShapeDtypeStruct((N_RET_HEADS, 8, 2 * c), F32),
            jax.ShapeDtypeStruct((2, N_FOURIER_GROUPS // 2, 2 * FOURIER_GROUP, 2 * FOURIER_GROUP), BF16),
        ),
        name="tables",
    )(dl, w_fourier, cc, sc)


def _inproj_kernel(x_ref, shift_ref, scale_ref, g_ref, w_ref, cos_ref, sin_ref, o_ref, *, rot_lo, rot_hi, silu_cols):
    x = x_ref[0]
    ms = jnp.mean(x * x, axis=-1, keepdims=True)
    h = x * lax.rsqrt(ms + EPS) * g_ref[...]
    h = h * (1.0 + scale_ref[0]) + shift_ref[0]
    hb = h.astype(BF16)
    n_out = w_ref.shape[1]
    hd = RET_HEAD_DIM
    bn = MXU_COLS
    is_u1 = (lax.broadcasted_iota(jnp.int32, (x.shape[0], hd), 1) & (hd // 4)) == 0
    def epilogue_weight(j):
        lo = j * bn
        return (lo in silu_cols) + (rot_lo <= lo < rot_hi)

    for j in sorted(range(n_out // bn), key=epilogue_weight, reverse=True):
        p = jnp.dot(hb, w_ref[:, j * bn:(j + 1) * bn], preferred_element_type=F32)
        for i in range(bn // hd):
            ph = p[:, i * hd:(i + 1) * hd]
            lo = j * bn + i * hd
            if lo in silu_cols:
                ph = _silu(ph)
            if rot_lo <= lo < rot_hi:
                partner = jnp.where(is_u1, pltpu.roll(ph, hd - hd // 4, axis=1), pltpu.roll(ph, hd // 4, axis=1))
                ph = ph * cos_ref[...] + partner * sin_ref[...]
            o_ref[0, :, lo:lo + hd] = ph.astype(BF16)


def _inproj(x, shift, scale, g_pre, w, cos_t, sin_t, *, rot_lo, rot_hi, silu_cols, tm):
    bsz, length, _ = x.shape
    n_out = w.shape[1]
    per_batch = shift.shape[0] > 1
    mod_map = (lambda b, t: (b, 0, 0)) if per_batch else (lambda b, t: (0, 0, 0))
    rot_map = (lambda b, t: (t, 0)) if rot_hi > rot_lo else (lambda b, t: (0, 0))
    return pl.pallas_call(
        functools.partial(_inproj_kernel, rot_lo=rot_lo, rot_hi=rot_hi, silu_cols=frozenset(silu_cols)),
        grid=(bsz, length // tm),
        in_specs=[
            pl.BlockSpec((1, tm, D_MODEL), lambda b, t: (b, t, 0)),
            pl.BlockSpec((1, 1, D_MODEL), mod_map),
            pl.BlockSpec((1, 1, D_MODEL), mod_map),
            pl.BlockSpec((1, D_MODEL), lambda b, t: (0, 0)),
            pl.BlockSpec((D_MODEL, n_out), lambda b, t: (0, 0)),
            pl.BlockSpec((tm, RET_HEAD_DIM), rot_map),
            pl.BlockSpec((tm, RET_HEAD_DIM), rot_map),
        ],
        out_specs=pl.BlockSpec((1, tm, n_out), lambda b, t: (b, t, 0)),
        out_shape=jax.ShapeDtypeStruct((bsz, length, n_out), BF16),
        compiler_params=pltpu.CompilerParams(
            dimension_semantics=("parallel", "parallel"), vmem_limit_bytes=VMEM_LIMIT),
        name="inproj",
    )(x, shift, scale, g_pre, w, cos_t, sin_t)


def _rotary_tables(seq_len):
    n_freq = RET_HEAD_DIM // 4
    pos = np.arange(seq_len)
    row = (pos // GRID_W).astype(np.float64)
    col = (pos % GRID_W).astype(np.float64)
    inv_freq = ROPE_BASE ** (-np.arange(n_freq, dtype=np.float64) / n_freq)
    ang_r = row[:, None] * inv_freq[None, :]
    ang_c = col[:, None] * inv_freq[None, :]
    cos_t = np.concatenate([np.cos(ang_r), np.cos(ang_r), np.cos(ang_c), np.cos(ang_c)], axis=1)
    sin_t = np.concatenate([-np.sin(ang_r), np.sin(ang_r), -np.sin(ang_c), np.sin(ang_c)], axis=1)
    return jnp.asarray(cos_t, F32), jnp.asarray(sin_t, F32)


def _fourier_kernel(u_ref, g_ref, ab_ref, c_ref, s_ref, js_ref, o_ref, ue_ref, uo_ref, pe_ref, qo_ref, gs_ref):
    n = u_ref.shape[1]
    h = n // 2
    fb = FOLD_BLOCK
    nb = h // fb
    fg = FOURIER_GROUP
    inv_sqrt_n = 1.0 / np.sqrt(n)
    js = js_ref[...]
    first_row = lax.broadcasted_iota(jnp.int32, (fb, D_FOURIER), 0) == 0

    def channel_map(v, half):
        return jnp.concatenate(
            [jnp.dot(v[:, p * 2 * fg:(p + 1) * 2 * fg], ab_ref[half, p], preferred_element_type=F32)
             for p in range(N_FOURIER_GROUPS // 2)], axis=1)

    for i in range(nb):
        blk = slice(i * fb, (i + 1) * fb)
        lo = u_ref[0, blk, :].astype(F32)
        src = n - (i + 1) * fb
        t = jnp.dot(js, u_ref[0, src:src + fb, :], preferred_element_type=F32)
        if i > 0:
            t = jnp.where(first_row, u_ref[0, n - i * fb:n - i * fb + 16, :][0:1, :].astype(F32), t)
        ue_ref[blk, :] = (lo + t).astype(BF16)
        uo_ref[blk, :] = (lo - t).astype(BF16)

    for i in range(nb):
        blk = slice(i * fb, (i + 1) * fb)
        pe_ref[blk, :] = channel_map(ue_ref[blk, :], 0).astype(BF16)
        qo_ref[blk, :] = channel_map(uo_ref[blk, :], 1).astype(BF16)

    p_mid = channel_map(u_ref[0, h:h + 16, :], 0)[0:1, :] * inv_sqrt_n
    sign = (1 - 2 * (lax.broadcasted_iota(jnp.int32, (h, D_FOURIER), 0) & 1)).astype(F32)
    y_mid = jnp.sum(pe_ref[...].astype(F32) * sign, axis=0, keepdims=True) * inv_sqrt_n + p_mid

    mb = FOURIER_ROW_BLOCK
    sign_mb = (1 - 2 * (lax.broadcasted_iota(jnp.int32, (mb, D_FOURIER), 0) & 1)).astype(F32)
    for i in range(h // mb):
        blk = slice(i * mb, (i + 1) * mb)
        yc = jnp.dot(c_ref[blk, :], pe_ref[...], preferred_element_type=F32) + sign_mb * p_mid
        ys = jnp.dot(s_ref[blk, :], qo_ref[...], preferred_element_type=F32)
        o_ref[0, blk, :] = ((yc - ys) * g_ref[0, blk, :].astype(F32)).astype(BF16)
        gs_ref[blk, :] = (yc + ys).astype(BF16)

    for i in range(nb):
        src = h - (i + 1) * fb
        t = jnp.dot(js, gs_ref[src:src + fb, :], preferred_element_type=F32)
        edge = y_mid if i == 0 else gs_ref[h - i * fb:h - i * fb + 16, :][0:1, :].astype(F32)
        t = jnp.where(first_row, edge, t)
        rows = slice(h + i * fb, h + (i + 1) * fb)
        o_ref[0, rows, :] = (t * g_ref[0, rows, :].astype(F32)).astype(BF16)


def _fourier(proj, ab, c_mat, s_mat, js):
    bsz, length, _ = proj.shape
    h = length // 2
    return pl.pallas_call(
        _fourier_kernel,
        grid=(bsz,),
        in_specs=[
            pl.BlockSpec((1, length, D_FOURIER), lambda b: (b, 0, COL_F_IN // D_FOURIER)),
            pl.BlockSpec((1, length, D_FOURIER), lambda b: (b, 0, COL_F_GATE // D_FOURIER)),
            pl.BlockSpec((2, N_FOURIER_GROUPS // 2, 2 * FOURIER_GROUP, 2 * FOURIER_GROUP), lambda b: (0, 0, 0, 0)),
            pl.BlockSpec((h, h), lambda b: (0, 0)),
            pl.BlockSpec((h, h), lambda b: (0, 0)),
            pl.BlockSpec((FOLD_BLOCK, FOLD_BLOCK), lambda b: (0, 0)),
        ],
        out_specs=pl.BlockSpec((1, length, D_FOURIER), lambda b: (b, 0, 0)),
        out_shape=jax.ShapeDtypeStruct((bsz, length, D_FOURIER), BF16),
        scratch_shapes=[
            pltpu.VMEM((h, D_FOURIER), BF16),
            pltpu.VMEM((h, D_FOURIER), BF16),
            pltpu.VMEM((h, D_FOURIER), BF16),
            pltpu.VMEM((h, D_FOURIER), BF16),
            pltpu.VMEM((h, D_FOURIER), BF16),
        ],
        compiler_params=pltpu.CompilerParams(
            dimension_semantics=("parallel",), vmem_limit_bytes=VMEM_LIMIT),
        name="fourier",
    )(proj, proj, ab, c_mat, s_mat, js)


def _seq_dft_matrices(length):
    h = length // 2
    idx = np.arange(h)
    ang = 2.0 * np.pi * ((idx[:, None] * idx[None, :]) % length) / length
    scale = 1.0 / np.sqrt(length)
    c_mat = jnp.asarray((np.cos(ang) * scale).astype(np.float32)).astype(BF16)
    s_mat = jnp.asarray((np.sin(ang) * scale).astype(np.float32)).astype(BF16)
    r = np.arange(1, FOLD_BLOCK)
    js = np.zeros((FOLD_BLOCK, FOLD_BLOCK), np.float32)
    js[r, FOLD_BLOCK - r] = 1.0
    return c_mat, s_mat, jnp.asarray(js).astype(BF16)


def _ret_kernel(q_ref, k_ref, v_ref, rg_ref, kc_ref, vc_ref, dm_ref, rd_ref, cd_ref, gain_ref,
                o_ref, u_ref, st_ref, a_ref):
    c = RET_CHUNK
    hd = RET_HEAD_DIM
    n_chunks = q_ref.shape[1] // c
    n_ctx = kc_ref.shape[1] // c

    for h in range(N_RET_HEADS):
        cols = slice(h * hd, (h + 1) * hd)
        dm = dm_ref[h]
        q_dec = rd_ref[h, 0]
        k_dec = rd_ref[h, 1]
        c_dec_f = cd_ref[h, 0:1, 0:hd]
        c_dec_b = cd_ref[h, 0:1, hd:2 * hd]
        gain = gain_ref[:, cols]

        def both_decays(t, dec):
            return jnp.concatenate([t * dec[:, 0:hd], t * dec[:, hd:2 * hd]], axis=1)

        def kv_outer(k, v):
            return lax.dot_general(v, both_decays(k, k_dec), (((0,), (0,)), ((), ())), preferred_element_type=F32)

        for j in range(n_ctx):
            rows = slice(j * c, (j + 1) * c)
            u_ref[j] = kv_outer(kc_ref[0, rows, cols], vc_ref[0, rows, cols])
        for j in range(n_chunks):
            rows = slice(j * c, (j + 1) * c)
            u_ref[n_ctx + j] = kv_outer(k_ref[0, rows, cols], v_ref[0, rows, cols])

        sf = jnp.zeros((hd, hd), F32)
        for j in range(n_ctx + n_chunks - 1):
            sf = c_dec_f * sf + u_ref[j, :, 0:hd]
            if j + 1 >= n_ctx:
                st_ref[j + 1 - n_ctx, :, 0:hd] = sf.astype(BF16)
        sb = jnp.zeros((hd, hd), F32)
        for j in reversed(range(n_ctx)):
            sb = c_dec_b * sb + u_ref[j, :, hd:2 * hd]
        for j in reversed(range(n_chunks)):
            st_ref[j, :, hd:2 * hd] = sb.astype(BF16)
            if j > 0:
                sb = c_dec_b * sb + u_ref[n_ctx + j, :, hd:2 * hd]

        for j in range(n_chunks):
            rows = slice(j * c, (j + 1) * c)
            s = lax.dot_general(q_ref[0, rows, cols], k_ref[0, rows, cols], (((1,), (1,)), ((), ())),
                                preferred_element_type=F32)
            a_ref[j] = (s * dm).astype(BF16)

        for j in range(n_chunks):
            rows = slice(j * c, (j + 1) * c)
            o = jnp.dot(a_ref[j], v_ref[0, rows, cols], preferred_element_type=F32)
            o += lax.dot_general(both_decays(q_ref[0, rows, cols], q_dec), st_ref[j], (((1,), (1,)), ((), ())),
                                 preferred_element_type=F32)
            mu = jnp.mean(o, axis=-1, keepdims=True)
            d = o - mu
            var = jnp.mean(d * d, axis=-1, keepdims=True)
            on = d * lax.rsqrt(var + EPS)
            o_ref[0, rows, cols] = (on * gain * rg_ref[0, rows, cols].astype(F32)).astype(BF16)


def _retention(proj, proj_c, dm, rd, cd, gain):
    bsz, length, _ = proj.shape
    ctx_len = proj_c.shape[1]
    hd = RET_HEAD_DIM
    c = RET_CHUNK
    n_chunks = length // c
    n_ctx = ctx_len // c
    assert n_ctx >= 1

    def col(base):
        return lambda b: (b, 0, base // D_RET)

    return pl.pallas_call(
        _ret_kernel,
        grid=(bsz,),
        in_specs=[
            pl.BlockSpec((1, length, D_RET), col(COL_Q)),
            pl.BlockSpec((1, length, D_RET), col(COL_K)),
            pl.BlockSpec((1, length, D_RET), col(COL_V)),
            pl.BlockSpec((1, length, D_RET), col(COL_R_GATE)),
            pl.BlockSpec((1, ctx_len, D_RET), col(0)),
            pl.BlockSpec((1, ctx_len, D_RET), col(D_RET)),
            pl.BlockSpec((N_RET_HEADS, c, c), lambda b: (0, 0, 0)),
            pl.BlockSpec((N_RET_HEADS, 2, c, 2 * hd), lambda b: (0, 0, 0, 0)),
            pl.BlockSpec((N_RET_HEADS, 8, 2 * hd), lambda b: (0, 0, 0)),
            pl.BlockSpec((1, D_RET), lambda b: (0, 0)),
        ],
        out_specs=pl.BlockSpec((1, length, D_RET), lambda b: (b, 0, 0)),
        out_shape=jax.ShapeDtypeStruct((bsz, length, D_RET), BF16),
        scratch_shapes=[
            pltpu.VMEM((n_ctx + n_chunks, hd, 2 * hd), F32),
            pltpu.VMEM((n_chunks, hd, 2 * hd), BF16),
            pltpu.VMEM((n_chunks, c, c), BF16),
        ],
        compiler_params=pltpu.CompilerParams(
            dimension_semantics=("parallel",), vmem_limit_bytes=VMEM_LIMIT),
        name="retention",
    )(proj, proj, proj, proj, proj_c, proj_c, dm, rd, cd, gain)


def _outproj_kernel(yf_ref, yr_ref, w_ref, x_ref, gate_ref, g_ref, o_ref):
    y = jnp.dot(yf_ref[0], w_ref[0:D_FOURIER, :], preferred_element_type=F32)
    y += jnp.dot(yr_ref[0], w_ref[D_FOURIER:D_MIX, :], preferred_element_type=F32)
    ms = jnp.mean(y * y, axis=-1, keepdims=True)
    yn = y * lax.rsqrt(ms + EPS) * g_ref[...]
    o_ref[0] = x_ref[0] + gate_ref[0] * yn


def _outproj(yf, yr, w_out, x, gate, g_post, *, tm):
    bsz, length, _ = x.shape
    return pl.pallas_call(
        _outproj_kernel,
        grid=(bsz, length // tm),
        in_specs=[
            pl.BlockSpec((1, tm, D_FOURIER), lambda b, t: (b, t, 0)),
            pl.BlockSpec((1, tm, D_RET), lambda b, t: (b, t, 0)),
            pl.BlockSpec((D_MIX, D_MODEL), lambda b, t: (0, 0)),
            pl.BlockSpec((1, tm, D_MODEL), lambda b, t: (b, t, 0)),
            pl.BlockSpec((1, 1, D_MODEL), lambda b, t: (b, 0, 0)),
            pl.BlockSpec((1, D_MODEL), lambda b, t: (0, 0)),
        ],
        out_specs=pl.BlockSpec((1, tm, D_MODEL), lambda b, t: (b, t, 0)),
        out_shape=jax.ShapeDtypeStruct((bsz, length, D_MODEL), F32),
        compiler_params=pltpu.CompilerParams(
            dimension_semantics=("parallel", "parallel"), vmem_limit_bytes=VMEM_LIMIT),
        name="outproj",
    )(yf, yr, w_out, x, gate, g_post)


def kernel(x, c, ctx, c_ctx, w_ada, b_ada, g_pre, g_post, w_in, w_fourier, decay_logit, ret_gn_gain, w_out):
    bsz, seq_len, _ = x.shape
    ctx_len = ctx.shape[1]
    assert w_ada.shape[0] == 1, "single layer only"
    assert bsz + 1 <= ADA_ROWS

    cvec = jnp.concatenate([c, c_ctx[None, :], jnp.zeros((ADA_ROWS - bsz - 1, D_MODEL), F32)], axis=0)
    mod = _adaln(cvec, w_ada[0], b_ada[0])
    shift = mod[:bsz, 0:D_MODEL].reshape(bsz, 1, D_MODEL)
    scale = mod[:bsz, D_MODEL:2 * D_MODEL].reshape(bsz, 1, D_MODEL)
    gate = mod[:bsz, 2 * D_MODEL:].reshape(bsz, 1, D_MODEL)
    shift_c = mod[bsz:bsz + 1, 0:D_MODEL].reshape(1, 1, D_MODEL)
    scale_c = mod[bsz:bsz + 1, D_MODEL:2 * D_MODEL].reshape(1, 1, D_MODEL)

    dm, rd, cd, ab = _tables(decay_logit[0], w_fourier[0])

    w_bf = w_in[0].astype(BF16)
    cos_t, sin_t = _rotary_tables(seq_len)
    g_pre2 = g_pre[0].reshape(1, D_MODEL)

    gate_cols = list(range(COL_F_GATE, COL_Q, RET_HEAD_DIM)) + list(range(COL_R_GATE, D_IN, RET_HEAD_DIM))
    proj = _inproj(x, shift, scale, g_pre2, w_bf, cos_t, sin_t,
                   rot_lo=COL_Q, rot_hi=COL_V, silu_cols=gate_cols, tm=TOKEN_TILE)
    ctx_rows = bsz * ctx_len
    proj_c = _inproj(ctx.reshape(1, ctx_rows, D_MODEL), shift_c, scale_c, g_pre2, w_bf[:, COL_K:COL_R_GATE],
                     cos_t, sin_t, rot_lo=0, rot_hi=0, silu_cols=(), tm=min(TOKEN_TILE, ctx_rows))
    proj_c = proj_c.reshape(bsz, ctx_len, 2 * D_RET)

    assert (seq_len // 2) % FOLD_BLOCK == 0
    yf = _fourier(proj, ab, *_seq_dft_matrices(seq_len))
    yr = _retention(proj, proj_c, dm, rd, cd, ret_gn_gain[0].reshape(1, D_RET))
    return _outproj(yf, yr, w_out[0].astype(BF16), x, gate, g_post[0].reshape(1, D_MODEL), tm=TOKEN_TILE)
```

```python
import functools

import numpy as np
import jax
import jax.numpy as jnp
from jax import lax
from jax.experimental import pallas as pl
from jax.experimental.pallas import tpu as pltpu

D_MODEL = 1024
GRID_W = 64
D_FOURIER = 512
N_FOURIER_GROUPS = 4
FOURIER_GROUP = D_FOURIER // N_FOURIER_GROUPS
D_RET = 512
N_RET_HEADS = 4
RET_HEAD_DIM = D_RET // N_RET_HEADS
D_MIX = D_FOURIER + D_RET
D_IN = 2 * D_FOURIER + 4 * D_RET
RET_CHUNK = 128
ROPE_BASE = 10000.0
QK_SCALE = RET_HEAD_DIM ** -0.5
EPS = 1e-6

COL_F_IN = 0
COL_F_GATE = D_FOURIER
COL_Q = 2 * D_FOURIER
COL_K = COL_Q + D_RET
COL_V = COL_K + D_RET
COL_R_GATE = COL_V + D_RET

MXU_COLS = 256
FOLD_BLOCK = 256
FOURIER_ROW_BLOCK = 512
FUSED_STEPS = 4
ADA_ROWS = 24
TOKEN_TILE = 1024
VMEM_LIMIT = 56 * 1024 * 1024

F32 = jnp.float32
BF16 = jnp.bfloat16


def _silu(v):
    return v * jax.nn.sigmoid(v)


def _adaln_kernel(cv_ref, w_ref, b_ref, o_ref):
    s = _silu(cv_ref[...])
    o_ref[...] = jnp.dot(s, w_ref[...], preferred_element_type=F32) + b_ref[...]


def _adaln(cvec, w_ada, b_ada):
    n_out = w_ada.shape[1]
    bn = D_MODEL
    return pl.pallas_call(
        _adaln_kernel,
        grid=(n_out // bn,),
        in_specs=[
            pl.BlockSpec((ADA_ROWS, D_MODEL), lambda j: (0, 0)),
            pl.BlockSpec((D_MODEL, bn), lambda j: (0, j)),
            pl.BlockSpec((1, bn), lambda j: (0, j)),
        ],
        out_specs=pl.BlockSpec((ADA_ROWS, bn), lambda j: (0, j)),
        out_shape=jax.ShapeDtypeStruct((ADA_ROWS, n_out), F32),
        name="adaln",
    )(cvec, w_ada, b_ada.reshape(1, n_out))


def _tables_kernel(dl_ref, wf_ref, cc_ref, sc_ref, dm_ref, rd_ref, cd_ref, ab_ref):
    c = RET_CHUNK
    n = lax.broadcasted_iota(jnp.int32, (c, c), 0).astype(F32)
    m = lax.broadcasted_iota(jnp.int32, (c, c), 1).astype(F32)
    diff = n - m

    def log_sigmoid(v):
        return jnp.minimum(v, 0.0) - jnp.log1p(jnp.exp(-jnp.abs(v)))

    for h in range(N_RET_HEADS):
        lg_f = jnp.broadcast_to(log_sigmoid(dl_ref[0, h])[0:1, :], (c, c))
        lg_b = jnp.broadcast_to(log_sigmoid(dl_ref[1, h])[0:1, :], (c, c))
        dm_ref[h] = QK_SCALE * jnp.where(diff >= 0, jnp.exp(lg_f * jnp.maximum(diff, 0.0)),
                                         jnp.exp(lg_b * jnp.maximum(-diff, 0.0)))
        rd_ref[h, 0, :, 0:c] = (QK_SCALE * jnp.exp(lg_f * (n + 1.0))).astype(BF16)
        rd_ref[h, 0, :, c:2 * c] = (QK_SCALE * jnp.exp(lg_b * (c - n))).astype(BF16)
        rd_ref[h, 1, :, 0:c] = jnp.exp(lg_f * (c - 1.0 - n)).astype(BF16)
        rd_ref[h, 1, :, c:2 * c] = jnp.exp(lg_b * n).astype(BF16)
        cd_ref[h, :, 0:c] = jnp.exp(lg_f[0:8, :] * float(c))
        cd_ref[h, :, c:2 * c] = jnp.exp(lg_b[0:8, :] * float(c))
    fg = FOURIER_GROUP
    ab_ref[...] = jnp.zeros(ab_ref.shape, BF16)
    for g in range(N_FOURIER_GROUPS):
        wf = wf_ref[g]
        a = jnp.dot(cc_ref[...], wf, preferred_element_type=F32, precision=lax.Precision.HIGHEST)
        b = jnp.dot(sc_ref[...], wf, preferred_element_type=F32, precision=lax.Precision.HIGHEST)
        pair, off = g // 2, (g % 2) * fg
        ab_ref[0, pair, off:off + fg, off:off + fg] = a.astype(BF16)
        ab_ref[1, pair, off:off + fg, off:off + fg] = b.astype(BF16)


def _tables(decay_logit, w_fourier):
    c = RET_CHUNK
    dl = jnp.broadcast_to(decay_logit[:, :, None, None], (2, N_RET_HEADS, 8, 128))
    idx = np.arange(FOURIER_GROUP)
    ang = 2.0 * np.pi * ((idx[:, None] * idx[None, :]) % FOURIER_GROUP) / FOURIER_GROUP
    cc = jnp.asarray(np.cos(ang) / np.sqrt(FOURIER_GROUP), F32)
    sc = jnp.asarray(np.sin(ang) / np.sqrt(FOURIER_GROUP), F32)
    return pl.pallas_call(
        _tables_kernel,
        out_shape=(
            jax.ShapeDtypeStruct((N_RET_HEADS, c, c), F32),
            jax.ShapeDtypeStruct((N_RET_HEADS, 2, c, 2 * c), BF16),
            jax.ShapeDtypeStruct((N_RET_HEADS, 8, 2 * c), F32),
            jax.ShapeDtypeStruct((2, N_FOURIER_GROUPS // 2, 2 * FOURIER_GROUP, 2 * FOURIER_GROUP), BF16),
        ),
        name="tables",
    )(dl, w_fourier, cc, sc)


def _inproj_kernel(x_ref, shift_ref, scale_ref, g_ref, w_ref, cos_ref, sin_ref, o_ref, *, rot_lo, rot_hi, silu_cols):
    x = x_ref[0]
    ms = jnp.mean(x * x, axis=-1, keepdims=True)
    h = x * lax.rsqrt(ms + EPS) * g_ref[...]
    h = h * (1.0 + scale_ref[0]) + shift_ref[0]
    hb = h.astype(BF16)
    n_out = w_ref.shape[1]
    hd = RET_HEAD_DIM
    bn = MXU_COLS
    is_u1 = (lax.broadcasted_iota(jnp.int32, (x.shape[0], hd), 1) & (hd // 4)) == 0

    def epilogue_weight(j):
        lo = j * bn
        return (lo in silu_cols) + (rot_lo <= lo < rot_hi)

    for j in sorted(range(n_out // bn), key=epilogue_weight, reverse=True):
        p = jnp.dot(hb, w_ref[:, j * bn:(j + 1) * bn], preferred_element_type=F32)
        for i in range(bn // hd):
            ph = p[:, i * hd:(i + 1) * hd]
            lo = j * bn + i * hd
            if lo in silu_cols:
                ph = _silu(ph)
            if rot_lo <= lo < rot_hi:
                partner = jnp.where(is_u1, pltpu.roll(ph, hd - hd // 4, axis=1), pltpu.roll(ph, hd // 4, axis=1))
                ph = ph * cos_ref[...] + partner * sin_ref[...]
            o_ref[0, :, lo:lo + hd] = ph.astype(BF16)


def _inproj(x, shift, scale, g_pre, w, cos_t, sin_t, *, rot_lo, rot_hi, silu_cols, tm):
    bsz, length, _ = x.shape
    n_out = w.shape[1]
    per_batch = shift.shape[0] > 1
    mod_map = (lambda b, t: (b, 0, 0)) if per_batch else (lambda b, t: (0, 0, 0))
    rot_map = (lambda b, t: (t, 0)) if rot_hi > rot_lo else (lambda b, t: (0, 0))
    return pl.pallas_call(
        functools.partial(_inproj_kernel, rot_lo=rot_lo, rot_hi=rot_hi, silu_cols=frozenset(silu_cols)),
        grid=(bsz, length // tm),
        in_specs=[
            pl.BlockSpec((1, tm, D_MODEL), lambda b, t: (b, t, 0)),
            pl.BlockSpec((1, 1, D_MODEL), mod_map),
            pl.BlockSpec((1, 1, D_MODEL), mod_map),
            pl.BlockSpec((1, D_MODEL), lambda b, t: (0, 0)),
            pl.BlockSpec((D_MODEL, n_out), lambda b, t: (0, 0)),
            pl.BlockSpec((tm, RET_HEAD_DIM), rot_map),
            pl.BlockSpec((tm, RET_HEAD_DIM), rot_map),
        ],
        out_specs=pl.BlockSpec((1, tm, n_out), lambda b, t: (b, t, 0)),
        out_shape=jax.ShapeDtypeStruct((bsz, length, n_out), BF16),
        compiler_params=pltpu.CompilerParams(
            dimension_semantics=("parallel", "parallel"), vmem_limit_bytes=VMEM_LIMIT),
        name="inproj",
    )(x, shift, scale, g_pre, w, cos_t, sin_t)


def _rotary_tables(seq_len):
    n_freq = RET_HEAD_DIM // 4
    pos = np.arange(seq_len)
    row = (pos // GRID_W).astype(np.float64)
    col = (pos % GRID_W).astype(np.float64)
    inv_freq = ROPE_BASE ** (-np.arange(n_freq, dtype=np.float64) / n_freq)
    ang_r = row[:, None] * inv_freq[None, :]
    ang_c = col[:, None] * inv_freq[None, :]
    cos_t = np.concatenate([np.cos(ang_r), np.cos(ang_r), np.cos(ang_c), np.cos(ang_c)], axis=1)
    sin_t = np.concatenate([-np.sin(ang_r), np.sin(ang_r), -np.sin(ang_c), np.sin(ang_c)], axis=1)
    return jnp.asarray(cos_t, F32), jnp.asarray(sin_t, F32)


def _fourier_outproj_kernel(u_ref, g_ref, ab_ref, c_ref, s_ref, js_ref, yr_ref, w_ref, x_ref, gate_ref, gp_ref,
                            o_ref, ue_ref, uo_ref, pe_ref, qo_ref, gs_ref, mid_ref, yf_ref):
    b = pl.program_id(0)
    t = pl.program_id(1)
    n_batch = pl.num_programs(0) - 1
    slot = b % 2
    n = u_ref.shape[1]
    h = n // 2
    fb = FOLD_BLOCK
    nb = h // fb
    mb = FOURIER_ROW_BLOCK
    fg = FOURIER_GROUP
    inv_sqrt_n = 1.0 / np.sqrt(n)
    first_row = lax.broadcasted_iota(jnp.int32, (fb, D_FOURIER), 0) == 0

    def channel_map(v, half):
        return jnp.concatenate(
            [jnp.dot(v[:, p * 2 * fg:(p + 1) * 2 * fg], ab_ref[half, p], preferred_element_type=F32)
             for p in range(N_FOURIER_GROUPS // 2)], axis=1)

    def fold_and_channel_map():
        js = js_ref[...]
        for i in range(nb):
            blk = slice(i * fb, (i + 1) * fb)
            lo = u_ref[0, blk, :].astype(F32)
            src = n - (i + 1) * fb
            r = jnp.dot(js, u_ref[0, src:src + fb, :], preferred_element_type=F32)
            if i > 0:
                r = jnp.where(first_row, u_ref[0, n - i * fb:n - i * fb + 16, :][0:1, :].astype(F32), r)
            ue_ref[blk, :] = (lo + r).astype(BF16)
            uo_ref[blk, :] = (lo - r).astype(BF16)
        for i in range(nb):
            blk = slice(i * fb, (i + 1) * fb)
            pe_ref[blk, :] = channel_map(ue_ref[blk, :], 0).astype(BF16)
            qo_ref[blk, :] = channel_map(uo_ref[blk, :], 1).astype(BF16)
        p_mid = channel_map(u_ref[0, h:h + 16, :], 0)[0:1, :] * inv_sqrt_n
        sign = (1 - 2 * (lax.broadcasted_iota(jnp.int32, (h, D_FOURIER), 0) & 1)).astype(F32)
        y_mid = jnp.sum(pe_ref[...].astype(F32) * sign, axis=0, keepdims=True) * inv_sqrt_n + p_mid
        mid_ref[0] = jnp.broadcast_to(p_mid, (8, D_FOURIER))
        mid_ref[1] = jnp.broadcast_to(y_mid, (8, D_FOURIER))

    def dft_block(i):
        blk = slice(i * mb, (i + 1) * mb)
        sign = (1 - 2 * (lax.broadcasted_iota(jnp.int32, (mb, D_FOURIER), 0) & 1)).astype(F32)
        yc = jnp.dot(c_ref[blk, :], pe_ref[...], preferred_element_type=F32) + sign * mid_ref[0, 0:1, :]
        ys = jnp.dot(s_ref[blk, :], qo_ref[...], preferred_element_type=F32)
        yf_ref[slot, blk, :] = ((yc - ys) * g_ref[0, blk, :].astype(F32)).astype(BF16)
        gs_ref[blk, :] = (yc + ys).astype(BF16)

    def reflect_upper_half():
        js = js_ref[...]
        for i in range(nb):
            src = h - (i + 1) * fb
            r = jnp.dot(js, gs_ref[src:src + fb, :], preferred_element_type=F32)
            edge = mid_ref[1, 0:1, :] if i == 0 else gs_ref[h - i * fb:h - i * fb + 16, :][0:1, :].astype(F32)
            r = jnp.where(first_row, edge, r)
            rows = slice(h + i * fb, h + (i + 1) * fb)
            yf_ref[slot, rows, :] = (r * g_ref[0, rows, :].astype(F32)).astype(BF16)

    phases = [fold_and_channel_map] + [functools.partial(dft_block, i) for i in range(h // mb)] + [reflect_upper_half]
    assert len(phases) == FUSED_STEPS
    for k, phase in enumerate(phases):
        pl.when(jnp.logical_and(b < n_batch, t == k))(phase)

    @pl.when(b > 0)
    def _():
        tm = x_ref.shape[1]
        rows = pl.ds(pl.multiple_of(t * tm, tm), tm)
        y = jnp.dot(yf_ref[1 - slot, rows, :], w_ref[0:D_FOURIER, :], preferred_element_type=F32)
        y += jnp.dot(yr_ref[0], w_ref[D_FOURIER:D_MIX, :], preferred_element_type=F32)
        ms = jnp.mean(y * y, axis=-1, keepdims=True)
        yn = y * lax.rsqrt(ms + EPS) * gp_ref[...]
        o_ref[0] = x_ref[0] + gate_ref[0] * yn


def _fourier_outproj(proj, yr, ab, c_mat, s_mat, js, w_out, x, gate, g_post):
    bsz, length, _ = x.shape
    h = length // 2
    assert h % FOLD_BLOCK == 0 and h % FOURIER_ROW_BLOCK == 0 and length % FUSED_STEPS == 0
    tm = length // FUSED_STEPS

    def cur(col):
        return lambda b, t: (jnp.minimum(b, bsz - 1), 0, col)

    def prev_tile(b, t):
        return (jnp.maximum(b - 1, 0), jnp.where(b == 0, 0, t), 0)

    const2 = lambda b, t: (0, 0)
    return pl.pallas_call(
        _fourier_outproj_kernel,
        grid=(bsz + 1, FUSED_STEPS),
        in_specs=[
            pl.BlockSpec((1, length, D_FOURIER), cur(COL_F_IN // D_FOURIER)),
            pl.BlockSpec((1, length, D_FOURIER), cur(COL_F_GATE // D_FOURIER)),
            pl.BlockSpec((2, N_FOURIER_GROUPS // 2, 2 * FOURIER_GROUP, 2 * FOURIER_GROUP), lambda b, t: (0, 0, 0, 0)),
            pl.BlockSpec((h, h), const2),
            pl.BlockSpec((h, h), const2),
            pl.BlockSpec((FOLD_BLOCK, FOLD_BLOCK), const2),
            pl.BlockSpec((1, tm, D_RET), prev_tile),
            pl.BlockSpec((D_MIX, D_MODEL), const2),
            pl.BlockSpec((1, tm, D_MODEL), prev_tile),
            pl.BlockSpec((1, 1, D_MODEL), lambda b, t: (jnp.maximum(b - 1, 0), 0, 0)),
            pl.BlockSpec((1, D_MODEL), const2),
        ],
        out_specs=pl.BlockSpec((1, tm, D_MODEL), prev_tile),
        out_shape=jax.ShapeDtypeStruct((bsz, length, D_MODEL), F32),
        scratch_shapes=[
            pltpu.VMEM((h, D_FOURIER), BF16),
            pltpu.VMEM((h, D_FOURIER), BF16),
            pltpu.VMEM((h, D_FOURIER), BF16),
            pltpu.VMEM((h, D_FOURIER), BF16),
            pltpu.VMEM((h, D_FOURIER), BF16),
            pltpu.VMEM((2, 8, D_FOURIER), F32),
            pltpu.VMEM((2, length, D_FOURIER), BF16),
        ],
        compiler_params=pltpu.CompilerParams(
            dimension_semantics=("arbitrary", "arbitrary"), vmem_limit_bytes=VMEM_LIMIT),
        name="fourier_outproj",
    )(proj, proj, ab, c_mat, s_mat, js, yr, w_out, x, gate, g_post)


def _seq_dft_matrices(length):
    h = length // 2
    idx = np.arange(h)
    ang = 2.0 * np.pi * ((idx[:, None] * idx[None, :]) % length) / length
    scale = 1.0 / np.sqrt(length)
    c_mat = jnp.asarray((np.cos(ang) * scale).astype(np.float32)).astype(BF16)
    s_mat = jnp.asarray((np.sin(ang) * scale).astype(np.float32)).astype(BF16)
    r = np.arange(1, FOLD_BLOCK)
    js = np.zeros((FOLD_BLOCK, FOLD_BLOCK), np.float32)
    js[r, FOLD_BLOCK - r] = 1.0
    return c_mat, s_mat, jnp.asarray(js).astype(BF16)


def _ret_kernel(q_ref, k_ref, v_ref, rg_ref, kc_ref, vc_ref, dm_ref, rd_ref, cd_ref, gain_ref,
                o_ref, u_ref, st_ref, a_ref):
    c = RET_CHUNK
    hd = RET_HEAD_DIM
    n_chunks = q_ref.shape[1] // c
    n_ctx = kc_ref.shape[1] // c

    for h in range(N_RET_HEADS):
        cols = slice(h * hd, (h + 1) * hd)
        dm = dm_ref[h]
        q_dec = rd_ref[h, 0]
        k_dec = rd_ref[h, 1]
        c_dec_f = cd_ref[h, 0:1, 0:hd]
        c_dec_b = cd_ref[h, 0:1, hd:2 * hd]
        gain = gain_ref[:, cols]

        def both_decays(t, dec):
            return jnp.concatenate([t * dec[:, 0:hd], t * dec[:, hd:2 * hd]], axis=1)

        def kv_outer(k, v):
            return lax.dot_general(v, both_decays(k, k_dec), (((0,), (0,)), ((), ())), preferred_element_type=F32)

        for j in range(n_ctx):
            rows = slice(j * c, (j + 1) * c)
            u_ref[j] = kv_outer(kc_ref[0, rows, cols], vc_ref[0, rows, cols])
        for j in range(n_chunks):
            rows = slice(j * c, (j + 1) * c)
            u_ref[n_ctx + j] = kv_outer(k_ref[0, rows, cols], v_ref[0, rows, cols])

        sf = jnp.zeros((hd, hd), F32)
        for j in range(n_ctx + n_chunks - 1):
            sf = c_dec_f * sf + u_ref[j, :, 0:hd]
            if j + 1 >= n_ctx:
                st_ref[j + 1 - n_ctx, :, 0:hd] = sf.astype(BF16)
        sb = jnp.zeros((hd, hd), F32)
        for j in reversed(range(n_ctx)):
            sb = c_dec_b * sb + u_ref[j, :, hd:2 * hd]
        for j in reversed(range(n_chunks)):
            st_ref[j, :, hd:2 * hd] = sb.astype(BF16)
            if j > 0:
                sb = c_dec_b * sb + u_ref[n_ctx + j, :, hd:2 * hd]

        for j in range(n_chunks):
            rows = slice(j * c, (j + 1) * c)
            s = lax.dot_general(q_ref[0, rows, cols], k_ref[0, rows, cols], (((1,), (1,)), ((), ())),
                                preferred_element_type=F32)
            a_ref[j] = (s * dm).astype(BF16)

        for j in range(n_chunks):
            rows = slice(j * c, (j + 1) * c)
            o = jnp.dot(a_ref[j], v_ref[0, rows, cols], preferred_element_type=F32)
            o += lax.dot_general(both_decays(q_ref[0, rows, cols], q_dec), st_ref[j], (((1,), (1,)), ((), ())),
                                 preferred_element_type=F32)
            mu = jnp.mean(o, axis=-1, keepdims=True)
            d = o - mu
            var = jnp.mean(d * d, axis=-1, keepdims=True)
            on = d * lax.rsqrt(var + EPS)
            o_ref[0, rows, cols] = (on * gain * rg_ref[0, rows, cols].astype(F32)).astype(BF16)


def _retention(proj, proj_c, dm, rd, cd, gain):
    bsz, length, _ = proj.shape
    ctx_len = proj_c.shape[1]
    hd = RET_HEAD_DIM
    c = RET_CHUNK
    n_chunks = length // c
    n_ctx = ctx_len // c
    assert n_ctx >= 1

    def col(base):
        return lambda b: (b, 0, base // D_RET)

    return pl.pallas_call(
        _ret_kernel,
        grid=(bsz,),
        in_specs=[
            pl.BlockSpec((1, length, D_RET), col(COL_Q)),
            pl.BlockSpec((1, length, D_RET), col(COL_K)),
            pl.BlockSpec((1, length, D_RET), col(COL_V)),
            pl.BlockSpec((1, length, D_RET), col(COL_R_GATE)),
            pl.BlockSpec((1, ctx_len, D_RET), col(0)),
            pl.BlockSpec((1, ctx_len, D_RET), col(D_RET)),
            pl.BlockSpec((N_RET_HEADS, c, c), lambda b: (0, 0, 0)),
            pl.BlockSpec((N_RET_HEADS, 2, c, 2 * hd), lambda b: (0, 0, 0, 0)),
            pl.BlockSpec((N_RET_HEADS, 8, 2 * hd), lambda b: (0, 0, 0)),
            pl.BlockSpec((1, D_RET), lambda b: (0, 0)),
        ],
        out_specs=pl.BlockSpec((1, length, D_RET), lambda b: (b, 0, 0)),
        out_shape=jax.ShapeDtypeStruct((bsz, length, D_RET), BF16),
        scratch_shapes=[
            pltpu.VMEM((n_ctx + n_chunks, hd, 2 * hd), F32),
            pltpu.VMEM((n_chunks, hd, 2 * hd), BF16),
            pltpu.VMEM((n_chunks, c, c), BF16),
        ],
        compiler_params=pltpu.CompilerParams(
            dimension_semantics=("parallel",), vmem_limit_bytes=VMEM_LIMIT),
        name="retention",
    )(proj, proj, proj, proj, proj_c, proj_c, dm, rd, cd, gain)


def kernel(x, c, ctx, c_ctx, w_ada, b_ada, g_pre, g_post, w_in, w_fourier, decay_logit, ret_gn_gain, w_out):
    bsz, seq_len, _ = x.shape
    ctx_len = ctx.shape[1]
    assert w_ada.shape[0] == 1, "single layer only"
    assert bsz + 1 <= ADA_ROWS

    cvec = jnp.concatenate([c, c_ctx[None, :], jnp.zeros((ADA_ROWS - bsz - 1, D_MODEL), F32)], axis=0)
    mod = _adaln(cvec, w_ada[0], b_ada[0])
    shift = mod[:bsz, 0:D_MODEL].reshape(bsz, 1, D_MODEL)
    scale = mod[:bsz, D_MODEL:2 * D_MODEL].reshape(bsz, 1, D_MODEL)
    gate = mod[:bsz, 2 * D_MODEL:].reshape(bsz, 1, D_MODEL)
    shift_c = mod[bsz:bsz + 1, 0:D_MODEL].reshape(1, 1, D_MODEL)
    scale_c = mod[bsz:bsz + 1, D_MODEL:2 * D_MODEL].reshape(1, 1, D_MODEL)

    dm, rd, cd, ab = _tables(decay_logit[0], w_fourier[0])

    w_bf = w_in[0].astype(BF16)
    cos_t, sin_t = _rotary_tables(seq_len)
    g_pre2 = g_pre[0].reshape(1, D_MODEL)

    gate_cols = list(range(COL_F_GATE, COL_Q, RET_HEAD_DIM)) + list(range(COL_R_GATE, D_IN, RET_HEAD_DIM))
    proj = _inproj(x, shift, scale, g_pre2, w_bf, cos_t, sin_t,
                   rot_lo=COL_Q, rot_hi=COL_V, silu_cols=gate_cols, tm=TOKEN_TILE)
    ctx_rows = bsz * ctx_len
    proj_c = _inproj(ctx.reshape(1, ctx_rows, D_MODEL), shift_c, scale_c, g_pre2, w_bf[:, COL_K:COL_R_GATE],
                     cos_t, sin_t, rot_lo=0, rot_hi=0, silu_cols=(), tm=min(TOKEN_TILE, ctx_rows))
    proj_c = proj_c.reshape(bsz, ctx_len, 2 * D_RET)

    yr = _retention(proj, proj_c, dm, rd, cd, ret_gn_gain[0].reshape(1, D_RET))
    return _fourier_outproj(proj, yr, ab, *_seq_dft_matrices(seq_len), w_out[0].astype(BF16), x, gate,
                            g_post[0].reshape(1, D_MODEL))
```

```python
import functools

import numpy as np
import jax
import jax.numpy as jnp
from jax import lax
from jax.experimental import pallas as pl
from jax.experimental.pallas import tpu as pltpu

D_MODEL = 1024
GRID_W = 64
D_FOURIER = 512
N_FOURIER_GROUPS = 4
FOURIER_GROUP = D_FOURIER // N_FOURIER_GROUPS
D_RET = 512
N_RET_HEADS = 4
RET_HEAD_DIM = D_RET // N_RET_HEADS
D_MIX = D_FOURIER + D_RET
D_IN = 2 * D_FOURIER + 4 * D_RET
RET_CHUNK = 128
ROPE_BASE = 10000.0
QK_SCALE = RET_HEAD_DIM ** -0.5
EPS = 1e-6

COL_F_IN = 0
COL_F_GATE = D_FOURIER
COL_Q = 2 * D_FOURIER
COL_K = COL_Q + D_RET
COL_V = COL_K + D_RET
COL_R_GATE = COL_V + D_RET

MXU_COLS = 256
FOLD_BLOCK = 256
FOURIER_ROW_BLOCK = 512
ADA_ROWS = 24
TOKEN_TILE = 1024
VMEM_LIMIT = 56 * 1024 * 1024

F32 = jnp.float32
BF16 = jnp.bfloat16


def _silu(v):
    return v * jax.nn.sigmoid(v)


def _adaln_kernel(cv_ref, w_ref, b_ref, add_ref, mul_ref, o_ref):
    s = _silu(cv_ref[...])
    m = jnp.dot(s, w_ref[...], preferred_element_type=F32) + b_ref[...]
    o_ref[...] = (m + add_ref[...]) * mul_ref[...]


def _adaln(cvec, w_ada, b_ada, g_pre, g_post):
    n_out = w_ada.shape[1]
    bn = D_MODEL
    assert n_out == 3 * bn
    zeros, ones = jnp.zeros((bn,), F32), jnp.ones((bn,), F32)
    add = jnp.concatenate([zeros, ones, zeros]).reshape(1, n_out)
    mul = jnp.concatenate([ones, g_pre, g_post]).reshape(1, n_out)
    row = pl.BlockSpec((1, bn), lambda j: (0, j))
    return pl.pallas_call(
        _adaln_kernel,
        grid=(n_out // bn,),
        in_specs=[
            pl.BlockSpec((ADA_ROWS, D_MODEL), lambda j: (0, 0)),
            pl.BlockSpec((D_MODEL, bn), lambda j: (0, j)),
            row, row, row,
        ],
        out_specs=pl.BlockSpec((ADA_ROWS, bn), lambda j: (0, j)),
        out_shape=jax.ShapeDtypeStruct((ADA_ROWS, n_out), F32),
        name="adaln",
    )(cvec, w_ada, b_ada.reshape(1, n_out), add, mul)


def _tables_kernel(dl_ref, wf_ref, cc_ref, sc_ref, dm_ref, rd_ref, cd_ref, ab_ref):
    c = RET_CHUNK
    n = lax.broadcasted_iota(jnp.int32, (c, c), 0).astype(F32)
    m = lax.broadcasted_iota(jnp.int32, (c, c), 1).astype(F32)
    diff = n - m

    def log_sigmoid(v):
        return jnp.minimum(v, 0.0) - jnp.log1p(jnp.exp(-jnp.abs(v)))

    for h in range(N_RET_HEADS):
        lg_f = jnp.broadcast_to(log_sigmoid(dl_ref[0, h])[0:1, :], (c, c))
        lg_b = jnp.broadcast_to(log_sigmoid(dl_ref[1, h])[0:1, :], (c, c))
        dm_ref[h] = QK_SCALE * jnp.where(diff >= 0, jnp.exp(lg_f * jnp.maximum(diff, 0.0)),
                                         jnp.exp(lg_b * jnp.maximum(-diff, 0.0)))
        rd_ref[h, 0, :, 0:c] = (QK_SCALE * jnp.exp(lg_f * (n + 1.0))).astype(BF16)
        rd_ref[h, 0, :, c:2 * c] = (QK_SCALE * jnp.exp(lg_b * (c - n))).astype(BF16)
        rd_ref[h, 1, :, 0:c] = jnp.exp(lg_f * (c - 1.0 - n)).astype(BF16)
        rd_ref[h, 1, :, c:2 * c] = jnp.exp(lg_b * n).astype(BF16)
        cd_ref[h, :, 0:c] = jnp.exp(lg_f[0:8, :] * float(c))
        cd_ref[h, :, c:2 * c] = jnp.exp(lg_b[0:8, :] * float(c))
    fg = FOURIER_GROUP
    ab_ref[...] = jnp.zeros(ab_ref.shape, BF16)
    for g in range(N_FOURIER_GROUPS):
        wf = wf_ref[g]
        a = jnp.dot(cc_ref[...], wf, preferred_element_type=F32, precision=lax.Precision.HIGHEST)
        b = jnp.dot(sc_ref[...], wf, preferred_element_type=F32, precision=lax.Precision.HIGHEST)
        pair, off = g // 2, (g % 2) * fg
        ab_ref[0, pair, off:off + fg, off:off + fg] = a.astype(BF16)
        ab_ref[1, pair, off:off + fg, off:off + fg] = b.astype(BF16)


def _tables(decay_logit, w_fourier):
    c = RET_CHUNK
    dl = jnp.broadcast_to(decay_logit[:, :, None, None], (2, N_RET_HEADS, 8, 128))
    idx = np.arange(FOURIER_GROUP)
    ang = 2.0 * np.pi * ((idx[:, None] * idx[None, :]) % FOURIER_GROUP) / FOURIER_GROUP
    cc = jnp.asarray(np.cos(ang) / np.sqrt(FOURIER_GROUP), F32)
    sc = jnp.asarray(np.sin(ang) / np.sqrt(FOURIER_GROUP), F32)
    return pl.pallas_call(
        _tables_kernel,
        out_shape=(
            jax.ShapeDtypeStruct((N_RET_HEADS, c, c), F32),
            jax.ShapeDtypeStruct((N_RET_HEADS, 2, c, 2 * c), BF16),
            jax.ShapeDtypeStruct((N_RET_HEADS, 8, 2 * c), F32),
            jax.ShapeDtypeStruct((2, N_FOURIER_GROUPS // 2, 2 * FOURIER_GROUP, 2 * FOURIER_GROUP), BF16),
        ),
        name="tables",
    )(dl, w_fourier, cc, sc)


def _inproj_kernel(x_ref, shift_ref, mult_ref, w_ref, cos_ref, sin_ref, o_ref, *, rot_lo, rot_hi, silu_cols):
    x = x_ref[0]
    ms = jnp.mean(x * x, axis=-1, keepdims=True)
    hb = (x * lax.rsqrt(ms + EPS) * mult_ref[0] + shift_ref[0]).astype(BF16)
    n_out = w_ref.shape[1]
    hd = RET_HEAD_DIM
    bn = MXU_COLS
    is_u1 = (lax.broadcasted_iota(jnp.int32, (x.shape[0], hd), 1) & (hd // 4)) == 0
    def epilogue_weight(j):
        lo = j * bn
        return (lo in silu_cols) + (rot_lo <= lo < rot_hi)

    for j in sorted(range(n_out // bn), key=epilogue_weight, reverse=True):
        p = jnp.dot(hb, w_ref[:, j * bn:(j + 1) * bn], preferred_element_type=F32)
        for i in range(bn // hd):
            ph = p[:, i * hd:(i + 1) * hd]
            lo = j * bn + i * hd
            if lo in silu_cols:
                ph = _silu(ph)
            if rot_lo <= lo < rot_hi:
                partner = jnp.where(is_u1, pltpu.roll(ph, hd - hd // 4, axis=1), pltpu.roll(ph, hd // 4, axis=1))
                ph = ph * cos_ref[...] + partner * sin_ref[...]
            o_ref[0, :, lo:lo + hd] = ph.astype(BF16)


def _inproj(x, shift, mult, w, cos_t, sin_t, *, rot_lo, rot_hi, silu_cols, tm):
    bsz, length, _ = x.shape
    n_out = w.shape[1]
    per_batch = shift.shape[0] > 1
    mod_map = (lambda b, t: (b, 0, 0)) if per_batch else (lambda b, t: (0, 0, 0))
    rot_map = (lambda b, t: (t, 0)) if rot_hi > rot_lo else (lambda b, t: (0, 0))
    return pl.pallas_call(
        functools.partial(_inproj_kernel, rot_lo=rot_lo, rot_hi=rot_hi, silu_cols=frozenset(silu_cols)),
        grid=(bsz, length // tm),
        in_specs=[
            pl.BlockSpec((1, tm, D_MODEL), lambda b, t: (b, t, 0)),
            pl.BlockSpec((1, 1, D_MODEL), mod_map),
            pl.BlockSpec((1, 1, D_MODEL), mod_map),
            pl.BlockSpec((D_MODEL, n_out), lambda b, t: (0, 0)),
            pl.BlockSpec((tm, RET_HEAD_DIM), rot_map),
            pl.BlockSpec((tm, RET_HEAD_DIM), rot_map),
        ],
        out_specs=pl.BlockSpec((1, tm, n_out), lambda b, t: (b, t, 0)),
        out_shape=jax.ShapeDtypeStruct((bsz, length, n_out), BF16),
        compiler_params=pltpu.CompilerParams(
            dimension_semantics=("parallel", "parallel"), vmem_limit_bytes=VMEM_LIMIT),
        name="inproj",
    )(x, shift, mult, w, cos_t, sin_t)


def _rotary_tables(seq_len):
    n_freq = RET_HEAD_DIM // 4
    pos = np.arange(seq_len)
    row = (pos // GRID_W).astype(np.float64)
    col = (pos % GRID_W).astype(np.float64)
    inv_freq = ROPE_BASE ** (-np.arange(n_freq, dtype=np.float64) / n_freq)
    ang_r = row[:, None] * inv_freq[None, :]
    ang_c = col[:, None] * inv_freq[None, :]
    cos_t = np.concatenate([np.cos(ang_r), np.cos(ang_r), np.cos(ang_c), np.cos(ang_c)], axis=1)
    sin_t = np.concatenate([-np.sin(ang_r), np.sin(ang_r), -np.sin(ang_c), np.sin(ang_c)], axis=1)
    return jnp.asarray(cos_t, F32), jnp.asarray(sin_t, F32)


def _fourier_kernel(u_ref, g_ref, ab_ref, c_ref, s_ref, js_ref, o_ref, ue_ref, uo_ref, pe_ref, qo_ref, gs_ref):
    n = u_ref.shape[1]
    h = n // 2
    fb = FOLD_BLOCK
    nb = h // fb
    fg = FOURIER_GROUP
    inv_sqrt_n = 1.0 / np.sqrt(n)
    js = js_ref[...]
    first_row = lax.broadcasted_iota(jnp.int32, (fb, D_FOURIER), 0) == 0

    def channel_map(v, half):
        return jnp.concatenate(
            [jnp.dot(v[:, p * 2 * fg:(p + 1) * 2 * fg], ab_ref[half, p], preferred_element_type=F32)
             for p in range(N_FOURIER_GROUPS // 2)], axis=1)

    for i in range(nb):
        blk = slice(i * fb, (i + 1) * fb)
        lo = u_ref[0, blk, :].astype(F32)
        src = n - (i + 1) * fb
        t = jnp.dot(js, u_ref[0, src:src + fb, :], preferred_element_type=F32)
        if i > 0:
            t = jnp.where(first_row, u_ref[0, n - i * fb:n - i * fb + 16, :][0:1, :].astype(F32), t)
        ue_ref[blk, :] = (lo + t).astype(BF16)
        uo_ref[blk, :] = (lo - t).astype(BF16)

    for i in range(nb):
        blk = slice(i * fb, (i + 1) * fb)
        pe_ref[blk, :] = channel_map(ue_ref[blk, :], 0).astype(BF16)
        qo_ref[blk, :] = channel_map(uo_ref[blk, :], 1).astype(BF16)

    p_mid = channel_map(u_ref[0, h:h + 16, :], 0)[0:1, :] * inv_sqrt_n
    sign = (1 - 2 * (lax.broadcasted_iota(jnp.int32, (h, D_FOURIER), 0) & 1)).astype(F32)
    y_mid = jnp.sum(pe_ref[...].astype(F32) * sign, axis=0, keepdims=True) * inv_sqrt_n + p_mid

    mb = FOURIER_ROW_BLOCK
    sign_mb = (1 - 2 * (lax.broadcasted_iota(jnp.int32, (mb, D_FOURIER), 0) & 1)).astype(F32)
    for i in range(h // mb):
        blk = slice(i * mb, (i + 1) * mb)
        yc = jnp.dot(c_ref[blk, :], pe_ref[...], preferred_element_type=F32) + sign_mb * p_mid
        ys = jnp.dot(s_ref[blk, :], qo_ref[...], preferred_element_type=F32)
        o_ref[0, blk, :] = ((yc - ys) * g_ref[0, blk, :].astype(F32)).astype(BF16)
        gs_ref[blk, :] = (yc + ys).astype(BF16)

    for i in range(nb):
        src = h - (i + 1) * fb
        t = jnp.dot(js, gs_ref[src:src + fb, :], preferred_element_type=F32)
        edge = y_mid if i == 0 else gs_ref[h - i * fb:h - i * fb + 16, :][0:1, :].astype(F32)
        t = jnp.where(first_row, edge, t)
        rows = slice(h + i * fb, h + (i + 1) * fb)
        o_ref[0, rows, :] = (t * g_ref[0, rows, :].astype(F32)).astype(BF16)


def _fourier(proj, ab, c_mat, s_mat, js):
    bsz, length, _ = proj.shape
    h = length // 2
    return pl.pallas_call(
        _fourier_kernel,
        grid=(bsz,),
        in_specs=[
            pl.BlockSpec((1, length, D_FOURIER), lambda b: (b, 0, COL_F_IN // D_FOURIER)),
            pl.BlockSpec((1, length, D_FOURIER), lambda b: (b, 0, COL_F_GATE // D_FOURIER)),
            pl.BlockSpec((2, N_FOURIER_GROUPS // 2, 2 * FOURIER_GROUP, 2 * FOURIER_GROUP), lambda b: (0, 0, 0, 0)),
            pl.BlockSpec((h, h), lambda b: (0, 0)),
            pl.BlockSpec((h, h), lambda b: (0, 0)),
            pl.BlockSpec((FOLD_BLOCK, FOLD_BLOCK), lambda b: (0, 0)),
        ],
        out_specs=pl.BlockSpec((1, length, D_FOURIER), lambda b: (b, 0, 0)),
        out_shape=jax.ShapeDtypeStruct((bsz, length, D_FOURIER), BF16),
        scratch_shapes=[
            pltpu.VMEM((h, D_FOURIER), BF16),
            pltpu.VMEM((h, D_FOURIER), BF16),
            pltpu.VMEM((h, D_FOURIER), BF16),
            pltpu.VMEM((h, D_FOURIER), BF16),
            pltpu.VMEM((h, D_FOURIER), BF16),
        ],
        compiler_params=pltpu.CompilerParams(
            dimension_semantics=("parallel",), vmem_limit_bytes=VMEM_LIMIT),
        name="fourier",
    )(proj, proj, ab, c_mat, s_mat, js)


def _seq_dft_matrices(length):
    h = length // 2
    idx = np.arange(h)
    ang = 2.0 * np.pi * ((idx[:, None] * idx[None, :]) % length) / length
    scale = 1.0 / np.sqrt(length)
    c_mat = jnp.asarray((np.cos(ang) * scale).astype(np.float32)).astype(BF16)
    s_mat = jnp.asarray((np.sin(ang) * scale).astype(np.float32)).astype(BF16)
    r = np.arange(1, FOLD_BLOCK)
    js = np.zeros((FOLD_BLOCK, FOLD_BLOCK), np.float32)
    js[r, FOLD_BLOCK - r] = 1.0
    return c_mat, s_mat, jnp.asarray(js).astype(BF16)


def _ret_kernel(q_ref, k_ref, v_ref, rg_ref, kc_ref, vc_ref, dm_ref, rd_ref, cd_ref, gain_ref,
                o_ref, u_ref, st_ref, a_ref):
    c = RET_CHUNK
    hd = RET_HEAD_DIM
    n_chunks = q_ref.shape[1] // c
    n_ctx = kc_ref.shape[1] // c

    for h in range(N_RET_HEADS):
        cols = slice(h * hd, (h + 1) * hd)
        dm = dm_ref[h]
        q_dec = rd_ref[h, 0]
        k_dec = rd_ref[h, 1]
        c_dec_f = cd_ref[h, 0:1, 0:hd]
        c_dec_b = cd_ref[h, 0:1, hd:2 * hd]
        gain = gain_ref[:, cols]

        def both_decays(t, dec):
            return jnp.concatenate([t * dec[:, 0:hd], t * dec[:, hd:2 * hd]], axis=1)

        def kv_outer(k, v):
            return lax.dot_general(v, both_decays(k, k_dec), (((0,), (0,)), ((), ())), preferred_element_type=F32)

        for j in range(n_ctx):
            rows = slice(j * c, (j + 1) * c)
            u_ref[j] = kv_outer(kc_ref[0, rows, cols], vc_ref[0, rows, cols])
        for j in range(n_chunks):
            rows = slice(j * c, (j + 1) * c)
            u_ref[n_ctx + j] = kv_outer(k_ref[0, rows, cols], v_ref[0, rows, cols])

        sf = jnp.zeros((hd, hd), F32)
        for j in range(n_ctx + n_chunks - 1):
            sf = c_dec_f * sf + u_ref[j, :, 0:hd]
            if j + 1 >= n_ctx:
                st_ref[j + 1 - n_ctx, :, 0:hd] = sf.astype(BF16)
        sb = jnp.zeros((hd, hd), F32)
        for j in reversed(range(n_ctx)):
            sb = c_dec_b * sb + u_ref[j, :, hd:2 * hd]
        for j in reversed(range(n_chunks)):
            st_ref[j, :, hd:2 * hd] = sb.astype(BF16)
            if j > 0:
                sb = c_dec_b * sb + u_ref[n_ctx + j, :, hd:2 * hd]

        for j in range(n_chunks):
            rows = slice(j * c, (j + 1) * c)
            s = lax.dot_general(q_ref[0, rows, cols], k_ref[0, rows, cols], (((1,), (1,)), ((), ())),
                                preferred_element_type=F32)
            a_ref[j] = (s * dm).astype(BF16)

        for j in range(n_chunks):
            rows = slice(j * c, (j + 1) * c)
            o = jnp.dot(a_ref[j], v_ref[0, rows, cols], preferred_element_type=F32)
            o += lax.dot_general(both_decays(q_ref[0, rows, cols], q_dec), st_ref[j], (((1,), (1,)), ((), ())),
                                 preferred_element_type=F32)
            mu = jnp.mean(o, axis=-1, keepdims=True)
            d = o - mu
            var = jnp.mean(d * d, axis=-1, keepdims=True)
            on = d * lax.rsqrt(var + EPS)
            o_ref[0, rows, cols] = (on * gain * rg_ref[0, rows, cols].astype(F32)).astype(BF16)


def _retention(proj, proj_c, dm, rd, cd, gain):
    bsz, length, _ = proj.shape
    ctx_len = proj_c.shape[1]
    hd = RET_HEAD_DIM
    c = RET_CHUNK
    n_chunks = length // c
    n_ctx = ctx_len // c
    assert n_ctx >= 1

    def col(base):
        return lambda b: (b, 0, base // D_RET)

    return pl.pallas_call(
        _ret_kernel,
        grid=(bsz,),
        in_specs=[
            pl.BlockSpec((1, length, D_RET), col(COL_Q)),
            pl.BlockSpec((1, length, D_RET), col(COL_K)),
            pl.BlockSpec((1, length, D_RET), col(COL_V)),
            pl.BlockSpec((1, length, D_RET), col(COL_R_GATE)),
            pl.BlockSpec((1, ctx_len, D_RET), col(0)),
            pl.BlockSpec((1, ctx_len, D_RET), col(D_RET)),
            pl.BlockSpec((N_RET_HEADS, c, c), lambda b: (0, 0, 0)),
            pl.BlockSpec((N_RET_HEADS, 2, c, 2 * hd), lambda b: (0, 0, 0, 0)),
            pl.BlockSpec((N_RET_HEADS, 8, 2 * hd), lambda b: (0, 0, 0)),
            pl.BlockSpec((1, D_RET), lambda b: (0, 0)),
        ],
        out_specs=pl.BlockSpec((1, length, D_RET), lambda b: (b, 0, 0)),
        out_shape=jax.ShapeDtypeStruct((bsz, length, D_RET), BF16),
        scratch_shapes=[
            pltpu.VMEM((n_ctx + n_chunks, hd, 2 * hd), F32),
            pltpu.VMEM((n_chunks, hd, 2 * hd), BF16),
            pltpu.VMEM((n_chunks, c, c), BF16),
        ],
        compiler_params=pltpu.CompilerParams(
            dimension_semantics=("parallel",), vmem_limit_bytes=VMEM_LIMIT),
        name="retention",
    )(proj, proj, proj, proj, proj_c, proj_c, dm, rd, cd, gain)


def _outproj_kernel(yf_ref, yr_ref, w_ref, x_ref, gate_ref, o_ref):
    y = jnp.dot(yf_ref[0], w_ref[0:D_FOURIER, :], preferred_element_type=F32)
    y += jnp.dot(yr_ref[0], w_ref[D_FOURIER:D_MIX, :], preferred_element_type=F32)
    ms = jnp.mean(y * y, axis=-1, keepdims=True)
    o_ref[0] = x_ref[0] + y * lax.rsqrt(ms + EPS) * gate_ref[0]


def _outproj(yf, yr, w_out, x, gate, *, tm):
    bsz, length, _ = x.shape
    return pl.pallas_call(
        _outproj_kernel,
        grid=(bsz, length // tm),
        in_specs=[
            pl.BlockSpec((1, tm, D_FOURIER), lambda b, t: (b, t, 0)),
            pl.BlockSpec((1, tm, D_RET), lambda b, t: (b, t, 0)),
            pl.BlockSpec((D_MIX, D_MODEL), lambda b, t: (0, 0)),
            pl.BlockSpec((1, tm, D_MODEL), lambda b, t: (b, t, 0)),
            pl.BlockSpec((1, 1, D_MODEL), lambda b, t: (b, 0, 0)),
        ],
        out_specs=pl.BlockSpec((1, tm, D_MODEL), lambda b, t: (b, t, 0)),
        out_shape=jax.ShapeDtypeStruct((bsz, length, D_MODEL), F32),
        compiler_params=pltpu.CompilerParams(
            dimension_semantics=("parallel", "parallel"), vmem_limit_bytes=VMEM_LIMIT),
        name="outproj",
    )(yf, yr, w_out, x, gate)


def kernel(x, c, ctx, c_ctx, w_ada, b_ada, g_pre, g_post, w_in, w_fourier, decay_logit, ret_gn_gain, w_out):
    bsz, seq_len, _ = x.shape
    ctx_len = ctx.shape[1]
    assert w_ada.shape[0] == 1, "single layer only"
    assert bsz + 1 <= ADA_ROWS

    cvec = jnp.concatenate([c, c_ctx[None, :], jnp.zeros((ADA_ROWS - bsz - 1, D_MODEL), F32)], axis=0)
    mod = _adaln(cvec, w_ada[0], b_ada[0], g_pre[0], g_post[0])
    shift = mod[:bsz, 0:D_MODEL].reshape(bsz, 1, D_MODEL)
    mult = mod[:bsz, D_MODEL:2 * D_MODEL].reshape(bsz, 1, D_MODEL)
    gate = mod[:bsz, 2 * D_MODEL:].reshape(bsz, 1, D_MODEL)
    shift_c = mod[bsz:bsz + 1, 0:D_MODEL].reshape(1, 1, D_MODEL)
    mult_c = mod[bsz:bsz + 1, D_MODEL:2 * D_MODEL].reshape(1, 1, D_MODEL)

    dm, rd, cd, ab = _tables(decay_logit[0], w_fourier[0])

    w_bf = w_in[0].astype(BF16)
    cos_t, sin_t = _rotary_tables(seq_len)

    gate_cols = list(range(COL_F_GATE, COL_Q, RET_HEAD_DIM)) + list(range(COL_R_GATE, D_IN, RET_HEAD_DIM))
    proj = _inproj(x, shift, mult, w_bf, cos_t, sin_t,
                   rot_lo=COL_Q, rot_hi=COL_V, silu_cols=gate_cols, tm=TOKEN_TILE)
    ctx_rows = bsz * ctx_len
    proj_c = _inproj(ctx.reshape(1, ctx_rows, D_MODEL), shift_c, mult_c, w_bf[:, COL_K:COL_R_GATE],
                     cos_t, sin_t, rot_lo=0, rot_hi=0, silu_cols=(), tm=min(TOKEN_TILE, ctx_rows))
    proj_c = proj_c.reshape(bsz, ctx_len, 2 * D_RET)

    assert (seq_len // 2) % FOLD_BLOCK == 0
    yf = _fourier(proj, ab, *_seq_dft_matrices(seq_len))
    yr = _retention(proj, proj_c, dm, rd, cd, ret_gn_gain[0].reshape(1, D_RET))
    return _outproj(yf, yr, w_out[0].astype(BF16), x, gate, tm=seq_len)
```

```python
import functools

import numpy as np
import jax
import jax.numpy as jnp
from jax import lax
from jax.experimental import pallas as pl
from jax.experimental.pallas import tpu as pltpu

D_MODEL = 1024
GRID_W = 64
D_FOURIER = 512
N_FOURIER_GROUPS = 4
FOURIER_GROUP = D_FOURIER // N_FOURIER_GROUPS
D_RET = 512
N_RET_HEADS = 4
RET_HEAD_DIM = D_RET // N_RET_HEADS
D_MIX = D_FOURIER + D_RET
D_IN = 2 * D_FOURIER + 4 * D_RET
RET_CHUNK = 128
ROPE_BASE = 10000.0
QK_SCALE = RET_HEAD_DIM ** -0.5
EPS = 1e-6

COL_F_IN = 0
COL_F_GATE = D_FOURIER
COL_Q = 2 * D_FOURIER
COL_K = COL_Q + D_RET
COL_V = COL_K + D_RET
COL_R_GATE = COL_V + D_RET

MXU_COLS = 256
FOLD_BLOCK = 256
FOURIER_ROW_BLOCK = 512
ADA_ROWS = 24
TOKEN_TILE = 1024
VMEM_LIMIT = 56 * 1024 * 1024

F32 = jnp.float32
BF16 = jnp.bfloat16


def _silu(v):
    return v * jax.nn.sigmoid(v)


def _adaln_kernel(cv_ref, w_ref, b_ref, add_ref, mul_ref, o_ref):
    s = _silu(cv_ref[...])
    m = jnp.dot(s, w_ref[...], preferred_element_type=F32) + b_ref[...]
    o_ref[...] = (m + add_ref[...]) * mul_ref[...]


def _adaln(cvec, w_ada, b_ada, g_pre, g_post):
    n_out = w_ada.shape[1]
    bn = D_MODEL
    assert n_out == 3 * bn
    zeros, ones = jnp.zeros((bn,), F32), jnp.ones((bn,), F32)
    add = jnp.concatenate([zeros, ones, zeros]).reshape(1, n_out)
    mul = jnp.concatenate([ones, g_pre, g_post]).reshape(1, n_out)
    row = pl.BlockSpec((1, bn), lambda j: (0, j))
    return pl.pallas_call(
        _adaln_kernel,
        grid=(n_out // bn,),
        in_specs=[
            pl.BlockSpec((ADA_ROWS, D_MODEL), lambda j: (0, 0)),
            pl.BlockSpec((D_MODEL, bn), lambda j: (0, j)),
            row, row, row,
        ],
        out_specs=pl.BlockSpec((ADA_ROWS, bn), lambda j: (0, j)),
        out_shape=jax.ShapeDtypeStruct((ADA_ROWS, n_out), F32),
        name="adaln",
    )(cvec, w_ada, b_ada.reshape(1, n_out), add, mul)


def _tables_kernel(dl_ref, wf_ref, cc_ref, sc_ref, dm_ref, rd_ref, cd_ref, ab_ref):
    c = RET_CHUNK
    n = lax.broadcasted_iota(jnp.int32, (c, c), 0).astype(F32)
    m = lax.broadcasted_iota(jnp.int32, (c, c), 1).astype(F32)
    diff = n - m

    def log_sigmoid(v):
        return jnp.minimum(v, 0.0) - jnp.log1p(jnp.exp(-jnp.abs(v)))

    for h in range(N_RET_HEADS):
        lg_f = jnp.broadcast_to(log_sigmoid(dl_ref[0, h])[0:1, :], (c, c))
        lg_b = jnp.broadcast_to(log_sigmoid(dl_ref[1, h])[0:1, :], (c, c))
        dm_ref[h] = QK_SCALE * jnp.where(diff >= 0, jnp.exp(lg_f * jnp.maximum(diff, 0.0)),
                                         jnp.exp(lg_b * jnp.maximum(-diff, 0.0)))
        rd_ref[h, 0, :, 0:c] = (QK_SCALE * jnp.exp(lg_f * (n + 1.0))).astype(BF16)
        rd_ref[h, 0, :, c:2 * c] = (QK_SCALE * jnp.exp(lg_b * (c - n))).astype(BF16)
        rd_ref[h, 1, :, 0:c] = jnp.exp(lg_f * (c - 1.0 - n)).astype(BF16)
        rd_ref[h, 1, :, c:2 * c] = jnp.exp(lg_b * n).astype(BF16)
        cd_ref[h, :, 0:c] = jnp.exp(lg_f[0:8, :] * float(c))
        cd_ref[h, :, c:2 * c] = jnp.exp(lg_b[0:8, :] * float(c))
    fg = FOURIER_GROUP
    ab_ref[...] = jnp.zeros(ab_ref.shape, BF16)
    for g in range(N_FOURIER_GROUPS):
        wf = wf_ref[g]
        a = jnp.dot(cc_ref[...], wf, preferred_element_type=F32, precision=lax.Precision.HIGHEST)
        b = jnp.dot(sc_ref[...], wf, preferred_element_type=F32, precision=lax.Precision.HIGHEST)
        pair, off = g // 2, (g % 2) * fg
        ab_ref[0, pair, off:off + fg, off:off + fg] = a.astype(BF16)
        ab_ref[1, pair, off:off + fg, off:off + fg] = b.astype(BF16)


def _tables(decay_logit, w_fourier):
    c = RET_CHUNK
    dl = jnp.broadcast_to(decay_logit[:, :, None, None], (2, N_RET_HEADS, 8, 128))
    idx = np.arange(FOURIER_GROUP)
    ang = 2.0 * np.pi * ((idx[:, None] * idx[None, :]) % FOURIER_GROUP) / FOURIER_GROUP
    cc = jnp.asarray(np.cos(ang) / np.sqrt(FOURIER_GROUP), F32)
    sc = jnp.asarray(np.sin(ang) / np.sqrt(FOURIER_GROUP), F32)
    return pl.pallas_call(
        _tables_kernel,
        out_shape=(
            jax.ShapeDtypeStruct((N_RET_HEADS, c, c), F32),
            jax.ShapeDtypeStruct((N_RET_HEADS, 2, c, 2 * c), BF16),
            jax.ShapeDtypeStruct((N_RET_HEADS, 8, 2 * c), F32),
            jax.ShapeDtypeStruct((2, N_FOURIER_GROUPS // 2, 2 * FOURIER_GROUP, 2 * FOURIER_GROUP), BF16),
        ),
        name="tables",
    )(dl, w_fourier, cc, sc)


def _inproj_kernel(x_ref, shift_ref, mult_ref, w_ref, cos_ref, sin_ref, o_ref, wb_ref, *,
                   rot_lo, rot_hi, silu_cols, cast_once):
    n_out = w_ref.shape[1]
    hd = RET_HEAD_DIM
    bn = MXU_COLS

    def cast_weights():
        for j in range(n_out // bn):
            wb_ref[:, j * bn:(j + 1) * bn] = w_ref[:, j * bn:(j + 1) * bn].astype(BF16)

    if cast_once:
        first = functools.reduce(jnp.logical_and, [pl.program_id(a) == 0 for a in range(3)])
        pl.when(first)(cast_weights)
    else:
        cast_weights()

    x = x_ref[0]
    ms = jnp.mean(x * x, axis=-1, keepdims=True)
    hb = (x * lax.rsqrt(ms + EPS) * mult_ref[0] + shift_ref[0]).astype(BF16)
    is_u1 = (lax.broadcasted_iota(jnp.int32, (x.shape[0], hd), 1) & (hd // 4)) == 0

    def epilogue_weight(j):
        lo = j * bn
        return (lo in silu_cols) + (rot_lo <= lo < rot_hi)

    for j in sorted(range(n_out // bn), key=epilogue_weight, reverse=True):
        p = jnp.dot(hb, wb_ref[:, j * bn:(j + 1) * bn], preferred_element_type=F32)
        for i in range(bn // hd):
            ph = p[:, i * hd:(i + 1) * hd]
            lo = j * bn + i * hd
            if lo in silu_cols:
                ph = _silu(ph)
            if rot_lo <= lo < rot_hi:
                partner = jnp.where(is_u1, pltpu.roll(ph, hd - hd // 4, axis=1), pltpu.roll(ph, hd // 4, axis=1))
                ph = ph * cos_ref[...] + partner * sin_ref[...]
            o_ref[0, :, lo:lo + hd] = ph.astype(BF16)


def _inproj(x, shift, mult, w, cos_t, sin_t, *, col_block, first_block, n_blocks, rot_lo, rot_hi, silu_cols, tm):
    bsz, length, _ = x.shape
    assert n_blocks == 1 or (rot_hi == rot_lo and not silu_cols)
    per_batch = shift.shape[0] > 1
    mod_map = (lambda b, t, j: (b, 0, 0)) if per_batch else (lambda b, t, j: (0, 0, 0))
    rot_map = (lambda b, t, j: (t, 0)) if rot_hi > rot_lo else (lambda b, t, j: (0, 0))
    cast_once = n_blocks == 1
    return pl.pallas_call(
        functools.partial(_inproj_kernel, rot_lo=rot_lo, rot_hi=rot_hi, silu_cols=frozenset(silu_cols),
                          cast_once=cast_once),
        grid=(bsz, length // tm, n_blocks),
        in_specs=[
            pl.BlockSpec((1, tm, D_MODEL), lambda b, t, j: (b, t, 0)),
            pl.BlockSpec((1, 1, D_MODEL), mod_map),
            pl.BlockSpec((1, 1, D_MODEL), mod_map),
            pl.BlockSpec((D_MODEL, col_block), lambda b, t, j: (0, first_block + j),
                         pipeline_mode=pl.Buffered(1) if cast_once else None),
            pl.BlockSpec((tm, RET_HEAD_DIM), rot_map),
            pl.BlockSpec((tm, RET_HEAD_DIM), rot_map),
        ],
        out_specs=pl.BlockSpec((1, tm, col_block), lambda b, t, j: (b, t, j)),
        out_shape=jax.ShapeDtypeStruct((bsz, length, n_blocks * col_block), BF16),
        scratch_shapes=[pltpu.VMEM((D_MODEL, col_block), BF16)],
        compiler_params=pltpu.CompilerParams(
            dimension_semantics=("arbitrary", "arbitrary", "arbitrary"), vmem_limit_bytes=VMEM_LIMIT),
        name="inproj",
    )(x, shift, mult, w, cos_t, sin_t)


def _rotary_tables(seq_len):
    n_freq = RET_HEAD_DIM // 4
    pos = np.arange(seq_len)
    row = (pos // GRID_W).astype(np.float64)
    col = (pos % GRID_W).astype(np.float64)
    inv_freq = ROPE_BASE ** (-np.arange(n_freq, dtype=np.float64) / n_freq)
    ang_r = row[:, None] * inv_freq[None, :]
    ang_c = col[:, None] * inv_freq[None, :]
    cos_t = np.concatenate([np.cos(ang_r), np.cos(ang_r), np.cos(ang_c), np.cos(ang_c)], axis=1)
    sin_t = np.concatenate([-np.sin(ang_r), np.sin(ang_r), -np.sin(ang_c), np.sin(ang_c)], axis=1)
    return jnp.asarray(cos_t, F32), jnp.asarray(sin_t, F32)


def _fourier_kernel(u_ref, g_ref, ab_ref, c_ref, s_ref, js_ref, o_ref, ue_ref, uo_ref, pe_ref, qo_ref, gs_ref):
    n = u_ref.shape[1]
    h = n // 2
    fb = FOLD_BLOCK
    nb = h // fb
    fg = FOURIER_GROUP
    inv_sqrt_n = 1.0 / np.sqrt(n)
    js = js_ref[...]
    first_row = lax.broadcasted_iota(jnp.int32, (fb, D_FOURIER), 0) == 0

    def channel_map(v, half):
        return jnp.concatenate(
            [jnp.dot(v[:, p * 2 * fg:(p + 1) * 2 * fg], ab_ref[half, p], preferred_element_type=F32)
             for p in range(N_FOURIER_GROUPS // 2)], axis=1)

    for i in range(nb):
        blk = slice(i * fb, (i + 1) * fb)
        lo = u_ref[0, blk, :].astype(F32)
        src = n - (i + 1) * fb
        t = jnp.dot(js, u_ref[0, src:src + fb, :], preferred_element_type=F32)
        if i > 0:
            t = jnp.where(first_row, u_ref[0, n - i * fb:n - i * fb + 16, :][0:1, :].astype(F32), t)
        ue_ref[blk, :] = (lo + t).astype(BF16)
        uo_ref[blk, :] = (lo - t).astype(BF16)

    for i in range(nb):
        blk = slice(i * fb, (i + 1) * fb)
        pe_ref[blk, :] = channel_map(ue_ref[blk, :], 0).astype(BF16)
        qo_ref[blk, :] = channel_map(uo_ref[blk, :], 1).astype(BF16)

    p_mid = channel_map(u_ref[0, h:h + 16, :], 0)[0:1, :] * inv_sqrt_n
    sign = (1 - 2 * (lax.broadcasted_iota(jnp.int32, (h, D_FOURIER), 0) & 1)).astype(F32)
    y_mid = jnp.sum(pe_ref[...].astype(F32) * sign, axis=0, keepdims=True) * inv_sqrt_n + p_mid

    mb = FOURIER_ROW_BLOCK
    sign_mb = (1 - 2 * (lax.broadcasted_iota(jnp.int32, (mb, D_FOURIER), 0) & 1)).astype(F32)
    for i in range(h // mb):
        blk = slice(i * mb, (i + 1) * mb)
        yc = jnp.dot(c_ref[blk, :], pe_ref[...], preferred_element_type=F32) + sign_mb * p_mid
        ys = jnp.dot(s_ref[blk, :], qo_ref[...], preferred_element_type=F32)
        o_ref[0, blk, :] = ((yc - ys) * g_ref[0, blk, :].astype(F32)).astype(BF16)
        gs_ref[blk, :] = (yc + ys).astype(BF16)

    for i in range(nb):
        src = h - (i + 1) * fb
        t = jnp.dot(js, gs_ref[src:src + fb, :], preferred_element_type=F32)
        edge = y_mid if i == 0 else gs_ref[h - i * fb:h - i * fb + 16, :][0:1, :].astype(F32)
        t = jnp.where(first_row, edge, t)
        rows = slice(h + i * fb, h + (i + 1) * fb)
        o_ref[0, rows, :] = (t * g_ref[0, rows, :].astype(F32)).astype(BF16)


def _fourier(proj, ab, c_mat, s_mat, js):
    bsz, length, _ = proj.shape
    h = length // 2
    return pl.pallas_call(
        _fourier_kernel,
        grid=(bsz,),
        in_specs=[
            pl.BlockSpec((1, length, D_FOURIER), lambda b: (b, 0, COL_F_IN // D_FOURIER)),
            pl.BlockSpec((1, length, D_FOURIER), lambda b: (b, 0, COL_F_GATE // D_FOURIER)),
            pl.BlockSpec((2, N_FOURIER_GROUPS // 2, 2 * FOURIER_GROUP, 2 * FOURIER_GROUP), lambda b: (0, 0, 0, 0)),
            pl.BlockSpec((h, h), lambda b: (0, 0)),
            pl.BlockSpec((h, h), lambda b: (0, 0)),
            pl.BlockSpec((FOLD_BLOCK, FOLD_BLOCK), lambda b: (0, 0)),
        ],
        out_specs=pl.BlockSpec((1, length, D_FOURIER), lambda b: (b, 0, 0)),
        out_shape=jax.ShapeDtypeStruct((bsz, length, D_FOURIER), BF16),
        scratch_shapes=[
            pltpu.VMEM((h, D_FOURIER), BF16),
            pltpu.VMEM((h, D_FOURIER), BF16),
            pltpu.VMEM((h, D_FOURIER), BF16),
            pltpu.VMEM((h, D_FOURIER), BF16),
            pltpu.VMEM((h, D_FOURIER), BF16),
        ],
        compiler_params=pltpu.CompilerParams(
            dimension_semantics=("parallel",), vmem_limit_bytes=VMEM_LIMIT),
        name="fourier",
    )(proj, proj, ab, c_mat, s_mat, js)


def _seq_dft_matrices(length):
    h = length // 2
    idx = np.arange(h)
    ang = 2.0 * np.pi * ((idx[:, None] * idx[None, :]) % length) / length
    scale = 1.0 / np.sqrt(length)
    c_mat = jnp.asarray((np.cos(ang) * scale).astype(np.float32)).astype(BF16)
    s_mat = jnp.asarray((np.sin(ang) * scale).astype(np.float32)).astype(BF16)
    r = np.arange(1, FOLD_BLOCK)
    js = np.zeros((FOLD_BLOCK, FOLD_BLOCK), np.float32)
    js[r, FOLD_BLOCK - r] = 1.0
    return c_mat, s_mat, jnp.asarray(js).astype(BF16)


def _ret_kernel(q_ref, k_ref, v_ref, rg_ref, kc_ref, vc_ref, dm_ref, rd_ref, cd_ref, gain_ref,
                o_ref, u_ref, st_ref, a_ref):
    c = RET_CHUNK
    hd = RET_HEAD_DIM
    n_chunks = q_ref.shape[1] // c
    n_ctx = kc_ref.shape[1] // c

    for h in range(N_RET_HEADS):
        cols = slice(h * hd, (h + 1) * hd)
        dm = dm_ref[h]
        q_dec = rd_ref[h, 0]
        k_dec = rd_ref[h, 1]
        c_dec_f = cd_ref[h, 0:1, 0:hd]
        c_dec_b = cd_ref[h, 0:1, hd:2 * hd]
        gain = gain_ref[:, cols]

        def both_decays(t, dec):
            return jnp.concatenate([t * dec[:, 0:hd], t * dec[:, hd:2 * hd]], axis=1)

        def kv_outer(k, v):
            return lax.dot_general(v, both_decays(k, k_dec), (((0,), (0,)), ((), ())), preferred_element_type=F32)

        for j in range(n_ctx):
            rows = slice(j * c, (j + 1) * c)
            u_ref[j] = kv_outer(kc_ref[0, rows, cols], vc_ref[0, rows, cols])
        for j in range(n_chunks):
            rows = slice(j * c, (j + 1) * c)
            u_ref[n_ctx + j] = kv_outer(k_ref[0, rows, cols], v_ref[0, rows, cols])

        sf = jnp.zeros((hd, hd), F32)
        for j in range(n_ctx + n_chunks - 1):
            sf = c_dec_f * sf + u_ref[j, :, 0:hd]
            if j + 1 >= n_ctx:
                st_ref[j + 1 - n_ctx, :, 0:hd] = sf.astype(BF16)
        sb = jnp.zeros((hd, hd), F32)
        for j in reversed(range(n_ctx)):
            sb = c_dec_b * sb + u_ref[j, :, hd:2 * hd]
        for j in reversed(range(n_chunks)):
            st_ref[j, :, hd:2 * hd] = sb.astype(BF16)
            if j > 0:
                sb = c_dec_b * sb + u_ref[n_ctx + j, :, hd:2 * hd]

        for j in range(n_chunks):
            rows = slice(j * c, (j + 1) * c)
            s = lax.dot_general(q_ref[0, rows, cols], k_ref[0, rows, cols], (((1,), (1,)), ((), ())),
                                preferred_element_type=F32)
            a_ref[j] = (s * dm).astype(BF16)

        for j in range(n_chunks):
            rows = slice(j * c, (j + 1) * c)
            o = jnp.dot(a_ref[j], v_ref[0, rows, cols], preferred_element_type=F32)
            o += lax.dot_general(both_decays(q_ref[0, rows, cols], q_dec), st_ref[j], (((1,), (1,)), ((), ())),
                                 preferred_element_type=F32)
            mu = jnp.mean(o, axis=-1, keepdims=True)
            d = o - mu
            var = jnp.mean(d * d, axis=-1, keepdims=True)
            on = d * lax.rsqrt(var + EPS)
            o_ref[0, rows, cols] = (on * gain * rg_ref[0, rows, cols].astype(F32)).astype(BF16)


def _retention(proj, proj_c, dm, rd, cd, gain):
    bsz, length, _ = proj.shape
    ctx_len = proj_c.shape[1]
    hd = RET_HEAD_DIM
    c = RET_CHUNK
    n_chunks = length // c
    n_ctx = ctx_len // c
    assert n_ctx >= 1

    def col(base):
        return lambda b: (b, 0, base // D_RET)

    return pl.pallas_call(
        _ret_kernel,
        grid=(bsz,),
        in_specs=[
            pl.BlockSpec((1, length, D_RET), col(COL_Q)),
            pl.BlockSpec((1, length, D_RET), col(COL_K)),
            pl.BlockSpec((1, length, D_RET), col(COL_V)),
            pl.BlockSpec((1, length, D_RET), col(COL_R_GATE)),
            pl.BlockSpec((1, ctx_len, D_RET), col(0)),
            pl.BlockSpec((1, ctx_len, D_RET), col(D_RET)),
            pl.BlockSpec((N_RET_HEADS, c, c), lambda b: (0, 0, 0)),
            pl.BlockSpec((N_RET_HEADS, 2, c, 2 * hd), lambda b: (0, 0, 0, 0)),
            pl.BlockSpec((N_RET_HEADS, 8, 2 * hd), lambda b: (0, 0, 0)),
            pl.BlockSpec((1, D_RET), lambda b: (0, 0)),
        ],
        out_specs=pl.BlockSpec((1, length, D_RET), lambda b: (b, 0, 0)),
        out_shape=jax.ShapeDtypeStruct((bsz, length, D_RET), BF16),
        scratch_shapes=[
            pltpu.VMEM((n_ctx + n_chunks, hd, 2 * hd), F32),
            pltpu.VMEM((n_chunks, hd, 2 * hd), BF16),
            pltpu.VMEM((n_chunks, c, c), BF16),
        ],
        compiler_params=pltpu.CompilerParams(
            dimension_semantics=("parallel",), vmem_limit_bytes=VMEM_LIMIT),
        name="retention",
    )(proj, proj, proj, proj, proj_c, proj_c, dm, rd, cd, gain)


def _outproj_kernel(yf_ref, yr_ref, w_ref, x_ref, gate_ref, o_ref):
    y = jnp.dot(yf_ref[0], w_ref[0:D_FOURIER, :], preferred_element_type=F32)
    y += jnp.dot(yr_ref[0], w_ref[D_FOURIER:D_MIX, :], preferred_element_type=F32)
    ms = jnp.mean(y * y, axis=-1, keepdims=True)
    o_ref[0] = x_ref[0] + y * lax.rsqrt(ms + EPS) * gate_ref[0]


def _outproj(yf, yr, w_out, x, gate, *, tm):
    bsz, length, _ = x.shape
    return pl.pallas_call(
        _outproj_kernel,
        grid=(bsz, length // tm),
        in_specs=[
            pl.BlockSpec((1, tm, D_FOURIER), lambda b, t: (b, t, 0)),
            pl.BlockSpec((1, tm, D_RET), lambda b, t: (b, t, 0)),
            pl.BlockSpec((D_MIX, D_MODEL), lambda b, t: (0, 0)),
            pl.BlockSpec((1, tm, D_MODEL), lambda b, t: (b, t, 0)),
            pl.BlockSpec((1, 1, D_MODEL), lambda b, t: (b, 0, 0)),
        ],
        out_specs=pl.BlockSpec((1, tm, D_MODEL), lambda b, t: (b, t, 0)),
        out_shape=jax.ShapeDtypeStruct((bsz, length, D_MODEL), F32),
        compiler_params=pltpu.CompilerParams(
            dimension_semantics=("parallel", "parallel"), vmem_limit_bytes=VMEM_LIMIT),
        name="outproj",
    )(yf, yr, w_out, x, gate)


def kernel(x, c, ctx, c_ctx, w_ada, b_ada, g_pre, g_post, w_in, w_fourier, decay_logit, ret_gn_gain, w_out):
    bsz, seq_len, _ = x.shape
    ctx_len = ctx.shape[1]
    assert w_ada.shape[0] == 1, "single layer only"
    assert bsz + 1 <= ADA_ROWS

    cvec = jnp.concatenate([c, c_ctx[None, :], jnp.zeros((ADA_ROWS - bsz - 1, D_MODEL), F32)], axis=0)
    mod = _adaln(cvec, w_ada[0], b_ada[0], g_pre[0], g_post[0])
    shift = mod[:bsz, 0:D_MODEL].reshape(bsz, 1, D_MODEL)
    mult = mod[:bsz, D_MODEL:2 * D_MODEL].reshape(bsz, 1, D_MODEL)
    gate = mod[:bsz, 2 * D_MODEL:].reshape(bsz, 1, D_MODEL)
    shift_c = mod[bsz:bsz + 1, 0:D_MODEL].reshape(1, 1, D_MODEL)
    mult_c = mod[bsz:bsz + 1, D_MODEL:2 * D_MODEL].reshape(1, 1, D_MODEL)

    dm, rd, cd, ab = _tables(decay_logit[0], w_fourier[0])

    cos_t, sin_t = _rotary_tables(seq_len)

    gate_cols = list(range(COL_F_GATE, COL_Q, RET_HEAD_DIM)) + list(range(COL_R_GATE, D_IN, RET_HEAD_DIM))
    proj = _inproj(x, shift, mult, w_in[0], cos_t, sin_t, col_block=D_IN, first_block=0, n_blocks=1,
                   rot_lo=COL_Q, rot_hi=COL_V, silu_cols=gate_cols, tm=TOKEN_TILE)
    ctx_rows = bsz * ctx_len
    assert COL_V == COL_K + D_RET and COL_K % D_RET == 0
    proj_c = _inproj(ctx.reshape(1, ctx_rows, D_MODEL), shift_c, mult_c, w_in[0], cos_t, sin_t,
                     col_block=D_RET, first_block=COL_K // D_RET, n_blocks=2,
                     rot_lo=0, rot_hi=0, silu_cols=(), tm=min(TOKEN_TILE, ctx_rows))
    proj_c = proj_c.reshape(bsz, ctx_len, 2 * D_RET)

    assert (seq_len // 2) % FOLD_BLOCK == 0
    yf = _fourier(proj, ab, *_seq_dft_matrices(seq_len))
    yr = _retention(proj, proj_c, dm, rd, cd, ret_gn_gain[0].reshape(1, D_RET))
    return _outproj(yf, yr, w_out[0].astype(BF16), x, gate, tm=seq_len)
```

```python
import functools

import numpy as np
import jax
import jax.numpy as jnp
from jax import lax
from jax.experimental import pallas as pl
from jax.experimental.pallas import tpu as pltpu

D_MODEL = 1024
GRID_W = 64
D_FOURIER = 512
N_FOURIER_GROUPS = 4
FOURIER_GROUP = D_FOURIER // N_FOURIER_GROUPS
D_RET = 512
N_RET_HEADS = 4
RET_HEAD_DIM = D_RET // N_RET_HEADS
D_MIX = D_FOURIER + D_RET
D_IN = 2 * D_FOURIER + 4 * D_RET
RET_CHUNK = 128
ROPE_BASE = 10000.0
QK_SCALE = RET_HEAD_DIM ** -0.5
EPS = 1e-6

COL_F_IN = 0
COL_F_GATE = D_FOURIER
COL_Q = 2 * D_FOURIER
COL_K = COL_Q + D_RET
COL_V = COL_K + D_RET
COL_R_GATE = COL_V + D_RET

MXU_COLS = 256
FOLD_BLOCK = 256
FOURIER_ROW_BLOCK = 512
ADA_ROWS = 24
TOKEN_TILE = 1024
VMEM_LIMIT = 56 * 1024 * 1024

F32 = jnp.float32
BF16 = jnp.bfloat16


def _silu(v):
    return v * jax.nn.sigmoid(v)


def _adaln_kernel(cv_ref, w_ref, b_ref, add_ref, mul_ref, o_ref):
    s = _silu(cv_ref[...])
    m = jnp.dot(s, w_ref[...], preferred_element_type=F32) + b_ref[...]
    o_ref[...] = (m + add_ref[...]) * mul_ref[...]


def _adaln(cvec, w_ada, b_ada, g_pre, g_post):
    n_out = w_ada.shape[1]
    bn = D_MODEL
    assert n_out == 3 * bn
    zeros, ones = jnp.zeros((bn,), F32), jnp.ones((bn,), F32)
    add = jnp.concatenate([zeros, ones, zeros]).reshape(1, n_out)
    mul = jnp.concatenate([ones, g_pre, g_post]).reshape(1, n_out)
    row = pl.BlockSpec((1, bn), lambda j: (0, j))
    return pl.pallas_call(
        _adaln_kernel,
        grid=(n_out // bn,),
        in_specs=[
            pl.BlockSpec((ADA_ROWS, D_MODEL), lambda j: (0, 0)),
            pl.BlockSpec((D_MODEL, bn), lambda j: (0, j)),
            row, row, row,
        ],
        out_specs=pl.BlockSpec((ADA_ROWS, bn), lambda j: (0, j)),
        out_shape=jax.ShapeDtypeStruct((ADA_ROWS, n_out), F32),
        name="adaln",
    )(cvec, w_ada, b_ada.reshape(1, n_out), add, mul)


def _tables_kernel(dl_ref, wf_ref, cc_ref, sc_ref, dm_ref, rd_ref, cd_ref, ab_ref):
    c = RET_CHUNK
    n = lax.broadcasted_iota(jnp.int32, (c, c), 0).astype(F32)
    m = lax.broadcasted_iota(jnp.int32, (c, c), 1).astype(F32)
    diff = n - m

    def log_sigmoid(v):
        return jnp.minimum(v, 0.0) - jnp.log1p(jnp.exp(-jnp.abs(v)))

    for h in range(N_RET_HEADS):
        lg_f = jnp.broadcast_to(log_sigmoid(dl_ref[0, h])[0:1, :], (c, c))
        lg_b = jnp.broadcast_to(log_sigmoid(dl_ref[1, h])[0:1, :], (c, c))
        dm_ref[h] = QK_SCALE * jnp.where(diff >= 0, jnp.exp(lg_f * jnp.maximum(diff, 0.0)),
                                         jnp.exp(lg_b * jnp.maximum(-diff, 0.0)))
        rd_ref[h, 0, :, 0:c] = (QK_SCALE * jnp.exp(lg_f * (n + 1.0))).astype(BF16)
        rd_ref[h, 0, :, c:2 * c] = (QK_SCALE * jnp.exp(lg_b * (c - n))).astype(BF16)
        rd_ref[h, 1, :, 0:c] = jnp.exp(lg_f * (c - 1.0 - n)).astype(BF16)
        rd_ref[h, 1, :, c:2 * c] = jnp.exp(lg_b * n).astype(BF16)
        cd_ref[h, :, 0:c] = jnp.exp(lg_f[0:8, :] * float(c))
        cd_ref[h, :, c:2 * c] = jnp.exp(lg_b[0:8, :] * float(c))
    fg = FOURIER_GROUP
    ab_ref[...] = jnp.zeros(ab_ref.shape, BF16)
    for g in range(N_FOURIER_GROUPS):
        wf = wf_ref[g]
        a = jnp.dot(cc_ref[...], wf, preferred_element_type=F32, precision=lax.Precision.HIGHEST)
        b = jnp.dot(sc_ref[...], wf, preferred_element_type=F32, precision=lax.Precision.HIGHEST)
        pair, off = g // 2, (g % 2) * fg
        ab_ref[0, pair, off:off + fg, off:off + fg] = a.astype(BF16)
        ab_ref[1, pair, off:off + fg, off:off + fg] = b.astype(BF16)


def _tables(decay_logit, w_fourier):
    c = RET_CHUNK
    dl = jnp.broadcast_to(decay_logit[:, :, None, None], (2, N_RET_HEADS, 8, 128))
    idx = np.arange(FOURIER_GROUP)
    ang = 2.0 * np.pi * ((idx[:, None] * idx[None, :]) % FOURIER_GROUP) / FOURIER_GROUP
    cc = jnp.asarray(np.cos(ang) / np.sqrt(FOURIER_GROUP), F32)
    sc = jnp.asarray(np.sin(ang) / np.sqrt(FOURIER_GROUP), F32)
    return pl.pallas_call(
        _tables_kernel,
        out_shape=(
            jax.ShapeDtypeStruct((N_RET_HEADS, c, c), F32),
            jax.ShapeDtypeStruct((N_RET_HEADS, 2, c, 2 * c), BF16),
            jax.ShapeDtypeStruct((N_RET_HEADS, 8, 2 * c), F32),
            jax.ShapeDtypeStruct((2, N_FOURIER_GROUPS // 2, 2 * FOURIER_GROUP, 2 * FOURIER_GROUP), BF16),
        ),
        name="tables",
    )(dl, w_fourier, cc, sc)


def _inproj_kernel(x_ref, shift_ref, mult_ref, cos_ref, sin_ref, *refs, n_w, rot_lo, rot_hi, silu_cols):
    w_refs, o_ref, wb_ref = refs[:n_w], refs[n_w], refs[n_w + 1]
    n_out = wb_ref.shape[1]
    hd = RET_HEAD_DIM
    bn = MXU_COLS

    @pl.when(jnp.logical_and(pl.program_id(0) == 0, pl.program_id(1) == 0))
    def _():
        for i, w_ref in enumerate(w_refs):
            width = w_ref.shape[1]
            for j in range(width // bn):
                wb_ref[:, i * width + j * bn:i * width + (j + 1) * bn] = w_ref[:, j * bn:(j + 1) * bn].astype(BF16)

    x = x_ref[0]
    ms = jnp.mean(x * x, axis=-1, keepdims=True)
    hb = (x * lax.rsqrt(ms + EPS) * mult_ref[0] + shift_ref[0]).astype(BF16)
    is_u1 = (lax.broadcasted_iota(jnp.int32, (x.shape[0], hd), 1) & (hd // 4)) == 0

    def epilogue_weight(j):
        lo = j * bn
        return (lo in silu_cols) + (rot_lo <= lo < rot_hi)

    for j in sorted(range(n_out // bn), key=epilogue_weight, reverse=True):
        p = jnp.dot(hb, wb_ref[:, j * bn:(j + 1) * bn], preferred_element_type=F32)
        for i in range(bn // hd):
            ph = p[:, i * hd:(i + 1) * hd]
            lo = j * bn + i * hd
            if lo in silu_cols:
                ph = _silu(ph)
            if rot_lo <= lo < rot_hi:
                partner = jnp.where(is_u1, pltpu.roll(ph, hd - hd // 4, axis=1), pltpu.roll(ph, hd // 4, axis=1))
                ph = ph * cos_ref[...] + partner * sin_ref[...]
            o_ref[0, :, lo:lo + hd] = ph.astype(BF16)


def _inproj(x, shift, mult, w, cos_t, sin_t, *, col_block, blocks, rot_lo, rot_hi, silu_cols, tm):
    bsz, length, _ = x.shape
    n_out = len(blocks) * col_block
    per_batch = shift.shape[0] > 1
    mod_map = (lambda b, t: (b, 0, 0)) if per_batch else (lambda b, t: (0, 0, 0))
    rot_map = (lambda b, t: (t, 0)) if rot_hi > rot_lo else (lambda b, t: (0, 0))
    w_specs = [pl.BlockSpec((D_MODEL, col_block), functools.partial(lambda b, t, blk: (0, blk), blk=blk),
                            pipeline_mode=pl.Buffered(1)) for blk in blocks]
    return pl.pallas_call(
        functools.partial(_inproj_kernel, n_w=len(blocks), rot_lo=rot_lo, rot_hi=rot_hi,
                          silu_cols=frozenset(silu_cols)),
        grid=(bsz, length // tm),
        in_specs=[
            pl.BlockSpec((1, tm, D_MODEL), lambda b, t: (b, t, 0)),
            pl.BlockSpec((1, 1, D_MODEL), mod_map),
            pl.BlockSpec((1, 1, D_MODEL), mod_map),
            pl.BlockSpec((tm, RET_HEAD_DIM), rot_map),
            pl.BlockSpec((tm, RET_HEAD_DIM), rot_map),
        ] + w_specs,
        out_specs=pl.BlockSpec((1, tm, n_out), lambda b, t: (b, t, 0)),
        out_shape=jax.ShapeDtypeStruct((bsz, length, n_out), BF16),
        scratch_shapes=[pltpu.VMEM((D_MODEL, n_out), BF16)],
        compiler_params=pltpu.CompilerParams(
            dimension_semantics=("arbitrary", "arbitrary"), vmem_limit_bytes=VMEM_LIMIT),
        name="inproj",
    )(x, shift, mult, cos_t, sin_t, *([w] * len(blocks)))


def _rotary_tables(seq_len):
    n_freq = RET_HEAD_DIM // 4
    pos = np.arange(seq_len)
    row = (pos // GRID_W).astype(np.float64)
    col = (pos % GRID_W).astype(np.float64)
    inv_freq = ROPE_BASE ** (-np.arange(n_freq, dtype=np.float64) / n_freq)
    ang_r = row[:, None] * inv_freq[None, :]
    ang_c = col[:, None] * inv_freq[None, :]
    cos_t = np.concatenate([np.cos(ang_r), np.cos(ang_r), np.cos(ang_c), np.cos(ang_c)], axis=1)
    sin_t = np.concatenate([-np.sin(ang_r), np.sin(ang_r), -np.sin(ang_c), np.sin(ang_c)], axis=1)
    return jnp.asarray(cos_t, F32), jnp.asarray(sin_t, F32)


def _fourier_kernel(u_ref, g_ref, ab_ref, c_ref, s_ref, js_ref, o_ref, ue_ref, uo_ref, pe_ref, qo_ref, gs_ref):
    n = u_ref.shape[1]
    h = n // 2
    fb = FOLD_BLOCK
    nb = h // fb
    fg = FOURIER_GROUP
    inv_sqrt_n = 1.0 / np.sqrt(n)
    js = js_ref[...]
    first_row = lax.broadcasted_iota(jnp.int32, (fb, D_FOURIER), 0) == 0

    def channel_map(v, half):
        return jnp.concatenate(
            [jnp.dot(v[:, p * 2 * fg:(p + 1) * 2 * fg], ab_ref[half, p], preferred_element_type=F32)
             for p in range(N_FOURIER_GROUPS // 2)], axis=1)

    for i in range(nb):
        blk = slice(i * fb, (i + 1) * fb)
        lo = u_ref[0, blk, :].astype(F32)
        src = n - (i + 1) * fb
        t = jnp.dot(js, u_ref[0, src:src + fb, :], preferred_element_type=F32)
        if i > 0:
            t = jnp.where(first_row, u_ref[0, n - i * fb:n - i * fb + 16, :][0:1, :].astype(F32), t)
        ue_ref[blk, :] = (lo + t).astype(BF16)
        uo_ref[blk, :] = (lo - t).astype(BF16)

    for i in range(nb):
        blk = slice(i * fb, (i + 1) * fb)
        pe_ref[blk, :] = channel_map(ue_ref[blk, :], 0).astype(BF16)
        qo_ref[blk, :] = channel_map(uo_ref[blk, :], 1).astype(BF16)

    p_mid = channel_map(u_ref[0, h:h + 16, :], 0)[0:1, :] * inv_sqrt_n
    sign = (1 - 2 * (lax.broadcasted_iota(jnp.int32, (h, D_FOURIER), 0) & 1)).astype(F32)
    y_mid = jnp.sum(pe_ref[...].astype(F32) * sign, axis=0, keepdims=True) * inv_sqrt_n + p_mid

    mb = FOURIER_ROW_BLOCK
    sign_mb = (1 - 2 * (lax.broadcasted_iota(jnp.int32, (mb, D_FOURIER), 0) & 1)).astype(F32)
    for i in range(h // mb):
        blk = slice(i * mb, (i + 1) * mb)
        yc = jnp.dot(c_ref[blk, :], pe_ref[...], preferred_element_type=F32) + sign_mb * p_mid
        ys = jnp.dot(s_ref[blk, :], qo_ref[...], preferred_element_type=F32)
        o_ref[0, blk, :] = ((yc - ys) * g_ref[0, blk, :].astype(F32)).astype(BF16)
        gs_ref[blk, :] = (yc + ys).astype(BF16)

    for i in range(nb):
        src = h - (i + 1) * fb
        t = jnp.dot(js, gs_ref[src:src + fb, :], preferred_element_type=F32)
        edge = y_mid if i == 0 else gs_ref[h - i * fb:h - i * fb + 16, :][0:1, :].astype(F32)
        t = jnp.where(first_row, edge, t)
        rows = slice(h + i * fb, h + (i + 1) * fb)
        o_ref[0, rows, :] = (t * g_ref[0, rows, :].astype(F32)).astype(BF16)


def _fourier(proj, ab, c_mat, s_mat, js):
    bsz, length, _ = proj.shape
    h = length // 2
    return pl.pallas_call(
        _fourier_kernel,
        grid=(bsz,),
        in_specs=[
            pl.BlockSpec((1, length, D_FOURIER), lambda b: (b, 0, COL_F_IN // D_FOURIER)),
            pl.BlockSpec((1, length, D_FOURIER), lambda b: (b, 0, COL_F_GATE // D_FOURIER)),
            pl.BlockSpec((2, N_FOURIER_GROUPS // 2, 2 * FOURIER_GROUP, 2 * FOURIER_GROUP), lambda b: (0, 0, 0, 0)),
            pl.BlockSpec((h, h), lambda b: (0, 0)),
            pl.BlockSpec((h, h), lambda b: (0, 0)),
            pl.BlockSpec((FOLD_BLOCK, FOLD_BLOCK), lambda b: (0, 0)),
        ],
        out_specs=pl.BlockSpec((1, length, D_FOURIER), lambda b: (b, 0, 0)),
        out_shape=jax.ShapeDtypeStruct((bsz, length, D_FOURIER), BF16),
        scratch_shapes=[
            pltpu.VMEM((h, D_FOURIER), BF16),
            pltpu.VMEM((h, D_FOURIER), BF16),
            pltpu.VMEM((h, D_FOURIER), BF16),
            pltpu.VMEM((h, D_FOURIER), BF16),
            pltpu.VMEM((h, D_FOURIER), BF16),
        ],
        compiler_params=pltpu.CompilerParams(
            dimension_semantics=("parallel",), vmem_limit_bytes=VMEM_LIMIT),
        name="fourier",
    )(proj, proj, ab, c_mat, s_mat, js)


def _seq_dft_matrices(length):
    h = length // 2
    idx = np.arange(h)
    ang = 2.0 * np.pi * ((idx[:, None] * idx[None, :]) % length) / length
    scale = 1.0 / np.sqrt(length)
    c_mat = jnp.asarray((np.cos(ang) * scale).astype(np.float32)).astype(BF16)
    s_mat = jnp.asarray((np.sin(ang) * scale).astype(np.float32)).astype(BF16)
    r = np.arange(1, FOLD_BLOCK)
    js = np.zeros((FOLD_BLOCK, FOLD_BLOCK), np.float32)
    js[r, FOLD_BLOCK - r] = 1.0
    return c_mat, s_mat, jnp.asarray(js).astype(BF16)


def _ret_kernel(q_ref, k_ref, v_ref, rg_ref, kc_ref, vc_ref, dm_ref, rd_ref, cd_ref, gain_ref,
                o_ref, u_ref, st_ref, a_ref):
    c = RET_CHUNK
    hd = RET_HEAD_DIM
    n_chunks = q_ref.shape[1] // c
    n_ctx = kc_ref.shape[1] // c

    for h in range(N_RET_HEADS):
        cols = slice(h * hd, (h + 1) * hd)
        dm = dm_ref[h]
        q_dec = rd_ref[h, 0]
        k_dec = rd_ref[h, 1]
        c_dec_f = cd_ref[h, 0:1, 0:hd]
        c_dec_b = cd_ref[h, 0:1, hd:2 * hd]
        gain = gain_ref[:, cols]

        def both_decays(t, dec):
            return jnp.concatenate([t * dec[:, 0:hd], t * dec[:, hd:2 * hd]], axis=1)

        def kv_outer(k, v):
            return lax.dot_general(v, both_decays(k, k_dec), (((0,), (0,)), ((), ())), preferred_element_type=F32)

        for j in range(n_ctx):
            rows = slice(j * c, (j + 1) * c)
            u_ref[j] = kv_outer(kc_ref[0, rows, cols], vc_ref[0, rows, cols])
        for j in range(n_chunks):
            rows = slice(j * c, (j + 1) * c)
            u_ref[n_ctx + j] = kv_outer(k_ref[0, rows, cols], v_ref[0, rows, cols])

        sf = jnp.zeros((hd, hd), F32)
        for j in range(n_ctx + n_chunks - 1):
            sf = c_dec_f * sf + u_ref[j, :, 0:hd]
            if j + 1 >= n_ctx:
                st_ref[j + 1 - n_ctx, :, 0:hd] = sf.astype(BF16)
        sb = jnp.zeros((hd, hd), F32)
        for j in reversed(range(n_ctx)):
            sb = c_dec_b * sb + u_ref[j, :, hd:2 * hd]
        for j in reversed(range(n_chunks)):
            st_ref[j, :, hd:2 * hd] = sb.astype(BF16)
            if j > 0:
                sb = c_dec_b * sb + u_ref[n_ctx + j, :, hd:2 * hd]

        for j in range(n_chunks):
            rows = slice(j * c, (j + 1) * c)
            s = lax.dot_general(q_ref[0, rows, cols], k_ref[0, rows, cols], (((1,), (1,)), ((), ())),
                                preferred_element_type=F32)
            a_ref[j] = (s * dm).astype(BF16)

        for j in range(n_chunks):
            rows = slice(j * c, (j + 1) * c)
            o = jnp.dot(a_ref[j], v_ref[0, rows, cols], preferred_element_type=F32)
            o += lax.dot_general(both_decays(q_ref[0, rows, cols], q_dec), st_ref[j], (((1,), (1,)), ((), ())),
                                 preferred_element_type=F32)
            mu = jnp.mean(o, axis=-1, keepdims=True)
            d = o - mu
            var = jnp.mean(d * d, axis=-1, keepdims=True)
            on = d * lax.rsqrt(var + EPS)
            o_ref[0, rows, cols] = (on * gain * rg_ref[0, rows, cols].astype(F32)).astype(BF16)


def _retention(proj, proj_c, dm, rd, cd, gain):
    bsz, length, _ = proj.shape
    ctx_len = proj_c.shape[1]
    hd = RET_HEAD_DIM
    c = RET_CHUNK
    n_chunks = length // c
    n_ctx = ctx_len // c
    assert n_ctx >= 1

    def col(base):
        return lambda b: (b, 0, base // D_RET)

    return pl.pallas_call(
        _ret_kernel,
        grid=(bsz,),
        in_specs=[
            pl.BlockSpec((1, length, D_RET), col(COL_Q)),
            pl.BlockSpec((1, length, D_RET), col(COL_K)),
            pl.BlockSpec((1, length, D_RET), col(COL_V)),
            pl.BlockSpec((1, length, D_RET), col(COL_R_GATE)),
            pl.BlockSpec((1, ctx_len, D_RET), col(0)),
            pl.BlockSpec((1, ctx_len, D_RET), col(D_RET)),
            pl.BlockSpec((N_RET_HEADS, c, c), lambda b: (0, 0, 0)),
            pl.BlockSpec((N_RET_HEADS, 2, c, 2 * hd), lambda b: (0, 0, 0, 0)),
            pl.BlockSpec((N_RET_HEADS, 8, 2 * hd), lambda b: (0, 0, 0)),
            pl.BlockSpec((1, D_RET), lambda b: (0, 0)),
        ],
        out_specs=pl.BlockSpec((1, length, D_RET), lambda b: (b, 0, 0)),
        out_shape=jax.ShapeDtypeStruct((bsz, length, D_RET), BF16),
        scratch_shapes=[
            pltpu.VMEM((n_ctx + n_chunks, hd, 2 * hd), F32),
            pltpu.VMEM((n_chunks, hd, 2 * hd), BF16),
            pltpu.VMEM((n_chunks, c, c), BF16),
        ],
        compiler_params=pltpu.CompilerParams(
            dimension_semantics=("parallel",), vmem_limit_bytes=VMEM_LIMIT),
        name="retention",
    )(proj, proj, proj, proj, proj_c, proj_c, dm, rd, cd, gain)


def _outproj_kernel(yf_ref, yr_ref, w_ref, x_ref, gate_ref, o_ref):
    y = jnp.dot(yf_ref[0], w_ref[0:D_FOURIER, :], preferred_element_type=F32)
    y += jnp.dot(yr_ref[0], w_ref[D_FOURIER:D_MIX, :], preferred_element_type=F32)
    ms = jnp.mean(y * y, axis=-1, keepdims=True)
    o_ref[0] = x_ref[0] + y * lax.rsqrt(ms + EPS) * gate_ref[0]


def _outproj(yf, yr, w_out, x, gate, *, tm):
    bsz, length, _ = x.shape
    return pl.pallas_call(
        _outproj_kernel,
        grid=(bsz, length // tm),
        in_specs=[
            pl.BlockSpec((1, tm, D_FOURIER), lambda b, t: (b, t, 0)),
            pl.BlockSpec((1, tm, D_RET), lambda b, t: (b, t, 0)),
            pl.BlockSpec((D_MIX, D_MODEL), lambda b, t: (0, 0)),
            pl.BlockSpec((1, tm, D_MODEL), lambda b, t: (b, t, 0)),
            pl.BlockSpec((1, 1, D_MODEL), lambda b, t: (b, 0, 0)),
        ],
        out_specs=pl.BlockSpec((1, tm, D_MODEL), lambda b, t: (b, t, 0)),
        out_shape=jax.ShapeDtypeStruct((bsz, length, D_MODEL), F32),
        compiler_params=pltpu.CompilerParams(
            dimension_semantics=("parallel", "parallel"), vmem_limit_bytes=VMEM_LIMIT),
        name="outproj",
    )(yf, yr, w_out, x, gate)


def kernel(x, c, ctx, c_ctx, w_ada, b_ada, g_pre, g_post, w_in, w_fourier, decay_logit, ret_gn_gain, w_out):
    bsz, seq_len, _ = x.shape
    ctx_len = ctx.shape[1]
    assert w_ada.shape[0] == 1, "single layer only"
    assert bsz + 1 <= ADA_ROWS

    cvec = jnp.concatenate([c, c_ctx[None, :], jnp.zeros((ADA_ROWS - bsz - 1, D_MODEL), F32)], axis=0)
    mod = _adaln(cvec, w_ada[0], b_ada[0], g_pre[0], g_post[0])
    shift = mod[:bsz, 0:D_MODEL].reshape(bsz, 1, D_MODEL)
    mult = mod[:bsz, D_MODEL:2 * D_MODEL].reshape(bsz, 1, D_MODEL)
    gate = mod[:bsz, 2 * D_MODEL:].reshape(bsz, 1, D_MODEL)
    shift_c = mod[bsz:bsz + 1, 0:D_MODEL].reshape(1, 1, D_MODEL)
    mult_c = mod[bsz:bsz + 1, D_MODEL:2 * D_MODEL].reshape(1, 1, D_MODEL)

    dm, rd, cd, ab = _tables(decay_logit[0], w_fourier[0])

    cos_t, sin_t = _rotary_tables(seq_len)

    gate_cols = list(range(COL_F_GATE, COL_Q, RET_HEAD_DIM)) + list(range(COL_R_GATE, D_IN, RET_HEAD_DIM))
    proj = _inproj(x, shift, mult, w_in[0], cos_t, sin_t, col_block=D_IN, blocks=(0,),
                   rot_lo=COL_Q, rot_hi=COL_V, silu_cols=gate_cols, tm=TOKEN_TILE)
    ctx_rows = bsz * ctx_len
    assert COL_V == COL_K + D_RET and COL_K % D_RET == 0
    proj_c = _inproj(ctx.reshape(1, ctx_rows, D_MODEL), shift_c, mult_c, w_in[0], cos_t, sin_t,
                     col_block=D_RET, blocks=(COL_K // D_RET, COL_V // D_RET),
                     rot_lo=0, rot_hi=0, silu_cols=(), tm=min(TOKEN_TILE, ctx_rows))
    proj_c = proj_c.reshape(bsz, ctx_len, 2 * D_RET)

    assert (seq_len // 2) % FOLD_BLOCK == 0
    yf = _fourier(proj, ab, *_seq_dft_matrices(seq_len))
    yr = _retention(proj, proj_c, dm, rd, cd, ret_gn_gain[0].reshape(1, D_RET))
    return _outproj(yf, yr, w_out[0].astype(BF16), x, gate, tm=seq_len)
```

```python
import functools

import numpy as np
import jax
import jax.numpy as jnp
from jax import lax
from jax.experimental import pallas as pl
from jax.experimental.pallas import tpu as pltpu

D_MODEL = 1024
GRID_W = 64
D_FOURIER = 512
N_FOURIER_GROUPS = 4
FOURIER_GROUP = D_FOURIER // N_FOURIER_GROUPS
D_RET = 512
N_RET_HEADS = 4
RET_HEAD_DIM = D_RET // N_RET_HEADS
D_MIX = D_FOURIER + D_RET
D_IN = 2 * D_FOURIER + 4 * D_RET
RET_CHUNK = 128
ROPE_BASE = 10000.0
QK_SCALE = RET_HEAD_DIM ** -0.5
EPS = 1e-6

COL_F_IN = 0
COL_F_GATE = D_FOURIER
COL_Q = 2 * D_FOURIER
COL_K = COL_Q + D_RET
COL_V = COL_K + D_RET
COL_R_GATE = COL_V + D_RET

MXU_COLS = 256
FOLD_BLOCK = 256
FOURIER_ROW_BLOCK = 512
OUTPROJ_ROW_BLOCK = 256
ADA_ROWS = 24
TOKEN_TILE = 1024
VMEM_LIMIT = 56 * 1024 * 1024

F32 = jnp.float32
BF16 = jnp.bfloat16


def _silu(v):
    return v * jax.nn.sigmoid(v)


def _adaln_kernel(cv_ref, w_ref, b_ref, add_ref, mul_ref, o_ref):
    s = _silu(cv_ref[...])
    m = jnp.dot(s, w_ref[...], preferred_element_type=F32) + b_ref[...]
    o_ref[...] = (m + add_ref[...]) * mul_ref[...]


def _adaln(cvec, w_ada, b_ada, g_pre, g_post):
    n_out = w_ada.shape[1]
    bn = D_MODEL
    assert n_out == 3 * bn
    zeros, ones = jnp.zeros((bn,), F32), jnp.ones((bn,), F32)
    add = jnp.concatenate([zeros, ones, zeros]).reshape(1, n_out)
    mul = jnp.concatenate([ones, g_pre, g_post]).reshape(1, n_out)
    row = pl.BlockSpec((1, bn), lambda j: (0, j))
    return pl.pallas_call(
        _adaln_kernel,
        grid=(n_out // bn,),
        in_specs=[
            pl.BlockSpec((ADA_ROWS, D_MODEL), lambda j: (0, 0)),
            pl.BlockSpec((D_MODEL, bn), lambda j: (0, j)),
            row, row, row,
        ],
        out_specs=pl.BlockSpec((ADA_ROWS, bn), lambda j: (0, j)),
        out_shape=jax.ShapeDtypeStruct((ADA_ROWS, n_out), F32),
        name="adaln",
    )(cvec, w_ada, b_ada.reshape(1, n_out), add, mul)


def _tables_kernel(dl_ref, wf_ref, cc_ref, sc_ref, dm_ref, rd_ref, cd_ref, ab_ref):
    c = RET_CHUNK
    n = lax.broadcasted_iota(jnp.int32, (c, c), 0).astype(F32)
    m = lax.broadcasted_iota(jnp.int32, (c, c), 1).astype(F32)
    diff = n - m

    def log_sigmoid(v):
        return jnp.minimum(v, 0.0) - jnp.log1p(jnp.exp(-jnp.abs(v)))

    for h in range(N_RET_HEADS):
        lg_f = jnp.broadcast_to(log_sigmoid(dl_ref[0, h])[0:1, :], (c, c))
        lg_b = jnp.broadcast_to(log_sigmoid(dl_ref[1, h])[0:1, :], (c, c))
        dm_ref[h] = QK_SCALE * jnp.where(diff >= 0, jnp.exp(lg_f * jnp.maximum(diff, 0.0)),
                                         jnp.exp(lg_b * jnp.maximum(-diff, 0.0)))
        rd_ref[h, 0, :, 0:c] = (QK_SCALE * jnp.exp(lg_f * (n + 1.0))).astype(BF16)
        rd_ref[h, 0, :, c:2 * c] = (QK_SCALE * jnp.exp(lg_b * (c - n))).astype(BF16)
        rd_ref[h, 1, :, 0:c] = jnp.exp(lg_f * (c - 1.0 - n)).astype(BF16)
        rd_ref[h, 1, :, c:2 * c] = jnp.exp(lg_b * n).astype(BF16)
        cd_ref[h, :, 0:c] = jnp.exp(lg_f[0:8, :] * float(c))
        cd_ref[h, :, c:2 * c] = jnp.exp(lg_b[0:8, :] * float(c))
    fg = FOURIER_GROUP
    ab_ref[...] = jnp.zeros(ab_ref.shape, BF16)
    for g in range(N_FOURIER_GROUPS):
        wf = wf_ref[g]
        a = jnp.dot(cc_ref[...], wf, preferred_element_type=F32, precision=lax.Precision.HIGHEST)
        b = jnp.dot(sc_ref[...], wf, preferred_element_type=F32, precision=lax.Precision.HIGHEST)
        pair, off = g // 2, (g % 2) * fg
        ab_ref[0, pair, off:off + fg, off:off + fg] = a.astype(BF16)
        ab_ref[1, pair, off:off + fg, off:off + fg] = b.astype(BF16)


def _tables(decay_logit, w_fourier):
    c = RET_CHUNK
    dl = jnp.broadcast_to(decay_logit[:, :, None, None], (2, N_RET_HEADS, 8, 128))
    idx = np.arange(FOURIER_GROUP)
    ang = 2.0 * np.pi * ((idx[:, None] * idx[None, :]) % FOURIER_GROUP) / FOURIER_GROUP
    cc = jnp.asarray(np.cos(ang) / np.sqrt(FOURIER_GROUP), F32)
    sc = jnp.asarray(np.sin(ang) / np.sqrt(FOURIER_GROUP), F32)
    return pl.pallas_call(
        _tables_kernel,
        out_shape=(
            jax.ShapeDtypeStruct((N_RET_HEADS, c, c), F32),
            jax.ShapeDtypeStruct((N_RET_HEADS, 2, c, 2 * c), BF16),
            jax.ShapeDtypeStruct((N_RET_HEADS, 8, 2 * c), F32),
            jax.ShapeDtypeStruct((2, N_FOURIER_GROUPS // 2, 2 * FOURIER_GROUP, 2 * FOURIER_GROUP), BF16),
        ),
        name="tables",
    )(dl, w_fourier, cc, sc)


def _inproj_kernel(x_ref, shift_ref, mult_ref, cos_ref, sin_ref, *refs, n_w, rot_lo, rot_hi, silu_cols):
    w_refs, o_ref, wb_ref = refs[:n_w], refs[n_w], refs[n_w + 1]
    n_out = wb_ref.shape[1]
    hd = RET_HEAD_DIM
    bn = MXU_COLS

    @pl.when(jnp.logical_and(pl.program_id(0) == 0, pl.program_id(1) == 0))
    def _():
        for i, w_ref in enumerate(w_refs):
            width = w_ref.shape[1]
            for j in range(width // bn):
                wb_ref[:, i * width + j * bn:i * width + (j + 1) * bn] = w_ref[:, j * bn:(j + 1) * bn].astype(BF16)

    x = x_ref[0]
    ms = jnp.mean(x * x, axis=-1, keepdims=True)
    hb = (x * lax.rsqrt(ms + EPS) * mult_ref[0] + shift_ref[0]).astype(BF16)
    is_u1 = (lax.broadcasted_iota(jnp.int32, (x.shape[0], hd), 1) & (hd // 4)) == 0

    def epilogue_weight(j):
        lo = j * bn
        return (lo in silu_cols) + (rot_lo <= lo < rot_hi)

    for j in sorted(range(n_out // bn), key=epilogue_weight, reverse=True):
        p = jnp.dot(hb, wb_ref[:, j * bn:(j + 1) * bn], preferred_element_type=F32)
        for i in range(bn // hd):
            ph = p[:, i * hd:(i + 1) * hd]
            lo = j * bn + i * hd
            if lo in silu_cols:
                ph = _silu(ph)
            if rot_lo <= lo < rot_hi:
                partner = jnp.where(is_u1, pltpu.roll(ph, hd - hd // 4, axis=1), pltpu.roll(ph, hd // 4, axis=1))
                ph = ph * cos_ref[...] + partner * sin_ref[...]
            o_ref[0, :, lo:lo + hd] = ph.astype(BF16)


def _inproj(x, shift, mult, w, cos_t, sin_t, *, col_block, blocks, rot_lo, rot_hi, silu_cols, tm):
    bsz, length, _ = x.shape
    n_out = len(blocks) * col_block
    per_batch = shift.shape[0] > 1
    mod_map = (lambda b, t: (b, 0, 0)) if per_batch else (lambda b, t: (0, 0, 0))
    rot_map = (lambda b, t: (t, 0)) if rot_hi > rot_lo else (lambda b, t: (0, 0))
    w_specs = [pl.BlockSpec((D_MODEL, col_block), functools.partial(lambda b, t, blk: (0, blk), blk=blk),
                            pipeline_mode=pl.Buffered(1)) for blk in blocks]
    return pl.pallas_call(
        functools.partial(_inproj_kernel, n_w=len(blocks), rot_lo=rot_lo, rot_hi=rot_hi,
                          silu_cols=frozenset(silu_cols)),
        grid=(bsz, length // tm),
        in_specs=[
            pl.BlockSpec((1, tm, D_MODEL), lambda b, t: (b, t, 0)),
            pl.BlockSpec((1, 1, D_MODEL), mod_map),
            pl.BlockSpec((1, 1, D_MODEL), mod_map),
            pl.BlockSpec((tm, RET_HEAD_DIM), rot_map),
            pl.BlockSpec((tm, RET_HEAD_DIM), rot_map),
        ] + w_specs,
        out_specs=pl.BlockSpec((1, tm, n_out), lambda b, t: (b, t, 0)),
        out_shape=jax.ShapeDtypeStruct((bsz, length, n_out), BF16),
        scratch_shapes=[pltpu.VMEM((D_MODEL, n_out), BF16)],
        compiler_params=pltpu.CompilerParams(
            dimension_semantics=("arbitrary", "arbitrary"), vmem_limit_bytes=VMEM_LIMIT),
        name="inproj",
    )(x, shift, mult, cos_t, sin_t, *([w] * len(blocks)))


def _rotary_tables(seq_len):
    n_freq = RET_HEAD_DIM // 4
    pos = np.arange(seq_len)
    row = (pos // GRID_W).astype(np.float64)
    col = (pos % GRID_W).astype(np.float64)
    inv_freq = ROPE_BASE ** (-np.arange(n_freq, dtype=np.float64) / n_freq)
    ang_r = row[:, None] * inv_freq[None, :]
    ang_c = col[:, None] * inv_freq[None, :]
    cos_t = np.concatenate([np.cos(ang_r), np.cos(ang_r), np.cos(ang_c), np.cos(ang_c)], axis=1)
    sin_t = np.concatenate([-np.sin(ang_r), np.sin(ang_r), -np.sin(ang_c), np.sin(ang_c)], axis=1)
    return jnp.asarray(cos_t, F32), jnp.asarray(sin_t, F32)


def _fourier_kernel(u_ref, g_ref, ab_ref, c_ref, s_ref, js_ref, o_ref, ue_ref, uo_ref, pe_ref, qo_ref, gs_ref):
    n = u_ref.shape[1]
    h = n // 2
    fb = FOLD_BLOCK
    nb = h // fb
    fg = FOURIER_GROUP
    inv_sqrt_n = 1.0 / np.sqrt(n)
    js = js_ref[...]
    first_row = lax.broadcasted_iota(jnp.int32, (fb, D_FOURIER), 0) == 0

    def channel_map(v, half):
        return jnp.concatenate(
            [jnp.dot(v[:, p * 2 * fg:(p + 1) * 2 * fg], ab_ref[half, p], preferred_element_type=F32)
             for p in range(N_FOURIER_GROUPS // 2)], axis=1)

    for i in range(nb):
        blk = slice(i * fb, (i + 1) * fb)
        lo = u_ref[0, blk, :].astype(F32)
        src = n - (i + 1) * fb
        t = jnp.dot(js, u_ref[0, src:src + fb, :], preferred_element_type=F32)
        if i > 0:
            t = jnp.where(first_row, u_ref[0, n - i * fb:n - i * fb + 16, :][0:1, :].astype(F32), t)
        ue_ref[blk, :] = (lo + t).astype(BF16)
        uo_ref[blk, :] = (lo - t).astype(BF16)

    for i in range(nb):
        blk = slice(i * fb, (i + 1) * fb)
        pe_ref[blk, :] = channel_map(ue_ref[blk, :], 0).astype(BF16)
        qo_ref[blk, :] = channel_map(uo_ref[blk, :], 1).astype(BF16)

    p_mid = channel_map(u_ref[0, h:h + 16, :], 0)[0:1, :] * inv_sqrt_n
    sign = (1 - 2 * (lax.broadcasted_iota(jnp.int32, (h, D_FOURIER), 0) & 1)).astype(F32)
    y_mid = jnp.sum(pe_ref[...].astype(F32) * sign, axis=0, keepdims=True) * inv_sqrt_n + p_mid

    mb = FOURIER_ROW_BLOCK
    sign_mb = (1 - 2 * (lax.broadcasted_iota(jnp.int32, (mb, D_FOURIER), 0) & 1)).astype(F32)
    for i in range(h // mb):
        blk = slice(i * mb, (i + 1) * mb)
        yc = jnp.dot(c_ref[blk, :], pe_ref[...], preferred_element_type=F32) + sign_mb * p_mid
        ys = jnp.dot(s_ref[blk, :], qo_ref[...], preferred_element_type=F32)
        o_ref[0, blk, :] = ((yc - ys) * g_ref[0, blk, :].astype(F32)).astype(BF16)
        gs_ref[blk, :] = (yc + ys).astype(BF16)

    for i in range(nb):
        src = h - (i + 1) * fb
        t = jnp.dot(js, gs_ref[src:src + fb, :], preferred_element_type=F32)
        edge = y_mid if i == 0 else gs_ref[h - i * fb:h - i * fb + 16, :][0:1, :].astype(F32)
        t = jnp.where(first_row, edge, t)
        rows = slice(h + i * fb, h + (i + 1) * fb)
        o_ref[0, rows, :] = (t * g_ref[0, rows, :].astype(F32)).astype(BF16)


def _fourier(proj, ab, c_mat, s_mat, js):
    bsz, length, _ = proj.shape
    h = length // 2
    return pl.pallas_call(
        _fourier_kernel,
        grid=(bsz,),
        in_specs=[
            pl.BlockSpec((1, length, D_FOURIER), lambda b: (b, 0, COL_F_IN // D_FOURIER)),
            pl.BlockSpec((1, length, D_FOURIER), lambda b: (b, 0, COL_F_GATE // D_FOURIER)),
            pl.BlockSpec((2, N_FOURIER_GROUPS // 2, 2 * FOURIER_GROUP, 2 * FOURIER_GROUP), lambda b: (0, 0, 0, 0)),
            pl.BlockSpec((h, h), lambda b: (0, 0)),
            pl.BlockSpec((h, h), lambda b: (0, 0)),
            pl.BlockSpec((FOLD_BLOCK, FOLD_BLOCK), lambda b: (0, 0)),
        ],
        out_specs=pl.BlockSpec((1, length, D_FOURIER), lambda b: (b, 0, 0)),
        out_shape=jax.ShapeDtypeStruct((bsz, length, D_FOURIER), BF16),
        scratch_shapes=[
            pltpu.VMEM((h, D_FOURIER), BF16),
            pltpu.VMEM((h, D_FOURIER), BF16),
            pltpu.VMEM((h, D_FOURIER), BF16),
            pltpu.VMEM((h, D_FOURIER), BF16),
            pltpu.VMEM((h, D_FOURIER), BF16),
        ],
        compiler_params=pltpu.CompilerParams(
            dimension_semantics=("parallel",), vmem_limit_bytes=VMEM_LIMIT),
        name="fourier",
    )(proj, proj, ab, c_mat, s_mat, js)


def _seq_dft_matrices(length):
    h = length // 2
    idx = np.arange(h)
    ang = 2.0 * np.pi * ((idx[:, None] * idx[None, :]) % length) / length
    scale = 1.0 / np.sqrt(length)
    c_mat = jnp.asarray((np.cos(ang) * scale).astype(np.float32)).astype(BF16)
    s_mat = jnp.asarray((np.sin(ang) * scale).astype(np.float32)).astype(BF16)
    r = np.arange(1, FOLD_BLOCK)
    js = np.zeros((FOLD_BLOCK, FOLD_BLOCK), np.float32)
    js[r, FOLD_BLOCK - r] = 1.0
    return c_mat, s_mat, jnp.asarray(js).astype(BF16)


def _ret_kernel(q_ref, k_ref, v_ref, rg_ref, kc_ref, vc_ref, dm_ref, rd_ref, cd_ref, gain_ref,
                o_ref, u_ref, st_ref, a_ref):
    c = RET_CHUNK
    hd = RET_HEAD_DIM
    n_chunks = q_ref.shape[1] // c
    n_ctx = kc_ref.shape[1] // c

    for h in range(N_RET_HEADS):
        cols = slice(h * hd, (h + 1) * hd)
        dm = dm_ref[h]
        q_dec = rd_ref[h, 0]
        k_dec = rd_ref[h, 1]
        c_dec_f = cd_ref[h, 0:1, 0:hd]
        c_dec_b = cd_ref[h, 0:1, hd:2 * hd]
        gain = gain_ref[:, cols]

        def both_decays(t, dec):
            return jnp.concatenate([t * dec[:, 0:hd], t * dec[:, hd:2 * hd]], axis=1)

        def kv_outer(k, v):
            return lax.dot_general(v, both_decays(k, k_dec), (((0,), (0,)), ((), ())), preferred_element_type=F32)

        for j in range(n_ctx):
            rows = slice(j * c, (j + 1) * c)
            u_ref[j] = kv_outer(kc_ref[0, rows, cols], vc_ref[0, rows, cols])
        for j in range(n_chunks):
            rows = slice(j * c, (j + 1) * c)
            u_ref[n_ctx + j] = kv_outer(k_ref[0, rows, cols], v_ref[0, rows, cols])

        sf = jnp.zeros((hd, hd), F32)
        for j in range(n_ctx + n_chunks - 1):
            sf = c_dec_f * sf + u_ref[j, :, 0:hd]
            if j + 1 >= n_ctx:
                st_ref[j + 1 - n_ctx, :, 0:hd] = sf.astype(BF16)
        sb = jnp.zeros((hd, hd), F32)
        for j in reversed(range(n_ctx)):
            sb = c_dec_b * sb + u_ref[j, :, hd:2 * hd]
        for j in reversed(range(n_chunks)):
            st_ref[j, :, hd:2 * hd] = sb.astype(BF16)
            if j > 0:
                sb = c_dec_b * sb + u_ref[n_ctx + j, :, hd:2 * hd]

        for j in range(n_chunks):
            rows = slice(j * c, (j + 1) * c)
            s = lax.dot_general(q_ref[0, rows, cols], k_ref[0, rows, cols], (((1,), (1,)), ((), ())),
                                preferred_element_type=F32)
            a_ref[j] = (s * dm).astype(BF16)

        for j in range(n_chunks):
            rows = slice(j * c, (j + 1) * c)
            o = jnp.dot(a_ref[j], v_ref[0, rows, cols], preferred_element_type=F32)
            o += lax.dot_general(both_decays(q_ref[0, rows, cols], q_dec), st_ref[j], (((1,), (1,)), ((), ())),
                                 preferred_element_type=F32)
            mu = jnp.mean(o, axis=-1, keepdims=True)
            d = o - mu
            var = jnp.mean(d * d, axis=-1, keepdims=True)
            on = d * lax.rsqrt(var + EPS)
            o_ref[0, rows, cols] = (on * gain * rg_ref[0, rows, cols].astype(F32)).astype(BF16)


def _retention(proj, proj_c, dm, rd, cd, gain):
    bsz, length, _ = proj.shape
    ctx_len = proj_c.shape[1]
    hd = RET_HEAD_DIM
    c = RET_CHUNK
    n_chunks = length // c
    n_ctx = ctx_len // c
    assert n_ctx >= 1

    def col(base):
        return lambda b: (b, 0, base // D_RET)

    return pl.pallas_call(
        _ret_kernel,
        grid=(bsz,),
        in_specs=[
            pl.BlockSpec((1, length, D_RET), col(COL_Q)),
            pl.BlockSpec((1, length, D_RET), col(COL_K)),
            pl.BlockSpec((1, length, D_RET), col(COL_V)),
            pl.BlockSpec((1, length, D_RET), col(COL_R_GATE)),
            pl.BlockSpec((1, ctx_len, D_RET), col(0)),
            pl.BlockSpec((1, ctx_len, D_RET), col(D_RET)),
            pl.BlockSpec((N_RET_HEADS, c, c), lambda b: (0, 0, 0)),
            pl.BlockSpec((N_RET_HEADS, 2, c, 2 * hd), lambda b: (0, 0, 0, 0)),
            pl.BlockSpec((N_RET_HEADS, 8, 2 * hd), lambda b: (0, 0, 0)),
            pl.BlockSpec((1, D_RET), lambda b: (0, 0)),
        ],
        out_specs=pl.BlockSpec((1, length, D_RET), lambda b: (b, 0, 0)),
        out_shape=jax.ShapeDtypeStruct((bsz, length, D_RET), BF16),
        scratch_shapes=[
            pltpu.VMEM((n_ctx + n_chunks, hd, 2 * hd), F32),
            pltpu.VMEM((n_chunks, hd, 2 * hd), BF16),
            pltpu.VMEM((n_chunks, c, c), BF16),
        ],
        compiler_params=pltpu.CompilerParams(
            dimension_semantics=("parallel",), vmem_limit_bytes=VMEM_LIMIT),
        name="retention",
    )(proj, proj, proj, proj, proj_c, proj_c, dm, rd, cd, gain)


def _outproj_kernel(yf_ref, yr_ref, w_ref, x_ref, gate_ref, o_ref):
    rb = OUTPROJ_ROW_BLOCK
    for r in range(x_ref.shape[1] // rb):
        rows = slice(r * rb, (r + 1) * rb)
        y = jnp.dot(yf_ref[0, rows, :], w_ref[0:D_FOURIER, :], preferred_element_type=F32)
        y += jnp.dot(yr_ref[0, rows, :], w_ref[D_FOURIER:D_MIX, :], preferred_element_type=F32)
        ms = jnp.mean(y * y, axis=-1, keepdims=True)
        o_ref[0, rows, :] = x_ref[0, rows, :] + y * lax.rsqrt(ms + EPS) * gate_ref[0]


def _outproj(yf, yr, w_out, x, gate, *, tm):
    bsz, length, _ = x.shape
    return pl.pallas_call(
        _outproj_kernel,
        grid=(bsz, length // tm),
        in_specs=[
            pl.BlockSpec((1, tm, D_FOURIER), lambda b, t: (b, t, 0)),
            pl.BlockSpec((1, tm, D_RET), lambda b, t: (b, t, 0)),
            pl.BlockSpec((D_MIX, D_MODEL), lambda b, t: (0, 0)),
            pl.BlockSpec((1, tm, D_MODEL), lambda b, t: (b, t, 0)),
            pl.BlockSpec((1, 1, D_MODEL), lambda b, t: (b, 0, 0)),
        ],
        out_specs=pl.BlockSpec((1, tm, D_MODEL), lambda b, t: (b, t, 0)),
        out_shape=jax.ShapeDtypeStruct((bsz, length, D_MODEL), F32),
        compiler_params=pltpu.CompilerParams(
            dimension_semantics=("parallel", "parallel"), vmem_limit_bytes=VMEM_LIMIT),
        name="outproj",
    )(yf, yr, w_out, x, gate)


def kernel(x, c, ctx, c_ctx, w_ada, b_ada, g_pre, g_post, w_in, w_fourier, decay_logit, ret_gn_gain, w_out):
    bsz, seq_len, _ = x.shape
    ctx_len = ctx.shape[1]
    assert w_ada.shape[0] == 1, "single layer only"
    assert bsz + 1 <= ADA_ROWS

    cvec = jnp.concatenate([c, c_ctx[None, :], jnp.zeros((ADA_ROWS - bsz - 1, D_MODEL), F32)], axis=0)
    mod = _adaln(cvec, w_ada[0], b_ada[0], g_pre[0], g_post[0])
    shift = mod[:bsz, 0:D_MODEL].reshape(bsz, 1, D_MODEL)
    mult = mod[:bsz, D_MODEL:2 * D_MODEL].reshape(bsz, 1, D_MODEL)
    gate = mod[:bsz, 2 * D_MODEL:].reshape(bsz, 1, D_MODEL)
    shift_c = mod[bsz:bsz + 1, 0:D_MODEL].reshape(1, 1, D_MODEL)
    mult_c = mod[bsz:bsz + 1, D_MODEL:2 * D_MODEL].reshape(1, 1, D_MODEL)

    dm, rd, cd, ab = _tables(decay_logit[0], w_fourier[0])

    cos_t, sin_t = _rotary_tables(seq_len)

    gate_cols = list(range(COL_F_GATE, COL_Q, RET_HEAD_DIM)) + list(range(COL_R_GATE, D_IN, RET_HEAD_DIM))
    proj = _inproj(x, shift, mult, w_in[0], cos_t, sin_t, col_block=D_IN, blocks=(0,),
                   rot_lo=COL_Q, rot_hi=COL_V, silu_cols=gate_cols, tm=TOKEN_TILE)
    ctx_rows = bsz * ctx_len
    assert COL_V == COL_K + D_RET and COL_K % D_RET == 0
    proj_c = _inproj(ctx.reshape(1, ctx_rows, D_MODEL), shift_c, mult_c, w_in[0], cos_t, sin_t,
                     col_block=D_RET, blocks=(COL_K // D_RET, COL_V // D_RET),
                     rot_lo=0, rot_hi=0, silu_cols=(), tm=min(TOKEN_TILE, ctx_rows))
    proj_c = proj_c.reshape(bsz, ctx_len, 2 * D_RET)

    assert (seq_len // 2) % FOLD_BLOCK == 0
    yf = _fourier(proj, ab, *_seq_dft_matrices(seq_len))
    yr = _retention(proj, proj_c, dm, rd, cd, ret_gn_gain[0].reshape(1, D_RET))
    return _outproj(yf, yr, w_out[0].astype(BF16), x, gate, tm=seq_len)
```

```python
import functools

import numpy as np
import jax
import jax.numpy as jnp
from jax import lax
from jax.experimental import pallas as pl
from jax.experimental.pallas import tpu as pltpu

D_MODEL = 1024
GRID_W = 64
D_FOURIER = 512
N_FOURIER_GROUPS = 4
FOURIER_GROUP = D_FOURIER // N_FOURIER_GROUPS
D_RET = 512
N_RET_HEADS = 4
RET_HEAD_DIM = D_RET // N_RET_HEADS
D_MIX = D_FOURIER + D_RET
D_IN = 2 * D_FOURIER + 4 * D_RET
RET_CHUNK = 128
ROPE_BASE = 10000.0
QK_SCALE = RET_HEAD_DIM ** -0.5
EPS = 1e-6

COL_F_IN = 0
COL_F_GATE = D_FOURIER
COL_Q = 2 * D_FOURIER
COL_K = COL_Q + D_RET
COL_V = COL_K + D_RET
COL_R_GATE = COL_V + D_RET

MXU_COLS = 256
FOLD_BLOCK = 256
FOURIER_ROW_BLOCK = 512
OUTPROJ_ROW_BLOCK = 256
SUBLANES = 8
MOD_SHIFT, MOD_MULT, MOD_GATE = 0, 1, 2
TOKEN_TILE = 1024
VMEM_LIMIT = 56 * 1024 * 1024

F32 = jnp.float32
BF16 = jnp.bfloat16


def _silu(v):
    return v * jax.nn.sigmoid(v)


def _adaln_kernel(c_ref, cc_ref, w_ref, b_ref, gpre_ref, gpost_ref, o_ref, s_ref):
    j = pl.program_id(0)
    nb = c_ref.shape[0]
    s_ref[0:nb, :] = _silu(c_ref[...])
    s_ref[nb:, :] = jnp.broadcast_to(_silu(cc_ref[...]), (s_ref.shape[0] - nb, D_MODEL))
    m = jnp.dot(s_ref[...], w_ref[...], preferred_element_type=F32) + b_ref[...]
    add = jnp.where(j == 1, 1.0, 0.0)
    mul = jnp.where(j == 1, gpre_ref[...], jnp.where(j == 2, gpost_ref[...], 1.0))
    o_ref[...] = (m + add) * mul


def _adaln(c, c_ctx, w_ada, b_ada, g_pre, g_post):
    bsz = c.shape[0]
    n_out = w_ada.shape[1]
    bn = D_MODEL
    assert n_out == 3 * bn and bsz % SUBLANES == 0
    rows = bsz + SUBLANES
    const = lambda j: (0, 0)
    return pl.pallas_call(
        _adaln_kernel,
        grid=(n_out // bn,),
        in_specs=[
            pl.BlockSpec((bsz, D_MODEL), const),
            pl.BlockSpec((1, D_MODEL), const),
            pl.BlockSpec((D_MODEL, bn), lambda j: (0, j)),
            pl.BlockSpec((1, bn), lambda j: (0, j)),
            pl.BlockSpec((1, D_MODEL), const),
            pl.BlockSpec((1, D_MODEL), const),
        ],
        out_specs=pl.BlockSpec((rows, bn), lambda j: (0, j)),
        out_shape=jax.ShapeDtypeStruct((rows, n_out), F32),
        scratch_shapes=[pltpu.VMEM((rows, D_MODEL), F32)],
        name="adaln",
    )(c, c_ctx.reshape(1, D_MODEL), w_ada, b_ada.reshape(1, n_out), g_pre.reshape(1, D_MODEL),
      g_post.reshape(1, D_MODEL))


def _tables_kernel(dl_ref, wf_ref, cc_ref, sc_ref, dm_ref, rd_ref, cd_ref, ab_ref):
    c = RET_CHUNK
    n = lax.broadcasted_iota(jnp.int32, (c, c), 0).astype(F32)
    m = lax.broadcasted_iota(jnp.int32, (c, c), 1).astype(F32)
    diff = n - m

    def log_sigmoid(v):
        return jnp.minimum(v, 0.0) - jnp.log1p(jnp.exp(-jnp.abs(v)))

    for h in range(N_RET_HEADS):
        lg_f = log_sigmoid(jnp.full((c, c), dl_ref[0, h], F32))
        lg_b = log_sigmoid(jnp.full((c, c), dl_ref[1, h], F32))
        dm_ref[h] = QK_SCALE * jnp.where(diff >= 0, jnp.exp(lg_f * jnp.maximum(diff, 0.0)),
                                         jnp.exp(lg_b * jnp.maximum(-diff, 0.0)))
        rd_ref[h, 0, :, 0:c] = (QK_SCALE * jnp.exp(lg_f * (n + 1.0))).astype(BF16)
        rd_ref[h, 0, :, c:2 * c] = (QK_SCALE * jnp.exp(lg_b * (c - n))).astype(BF16)
        rd_ref[h, 1, :, 0:c] = jnp.exp(lg_f * (c - 1.0 - n)).astype(BF16)
        rd_ref[h, 1, :, c:2 * c] = jnp.exp(lg_b * n).astype(BF16)
        cd_ref[h, :, 0:c] = jnp.exp(lg_f[0:SUBLANES, :] * float(c))
        cd_ref[h, :, c:2 * c] = jnp.exp(lg_b[0:SUBLANES, :] * float(c))
    fg = FOURIER_GROUP
    ab_ref[...] = jnp.zeros(ab_ref.shape, BF16)
    for g in range(N_FOURIER_GROUPS):
        wf = wf_ref[g]
        a = jnp.dot(cc_ref[...], wf, preferred_element_type=F32, precision=lax.Precision.HIGHEST)
        b = jnp.dot(sc_ref[...], wf, preferred_element_type=F32, precision=lax.Precision.HIGHEST)
        pair, off = g // 2, (g % 2) * fg
        ab_ref[0, pair, off:off + fg, off:off + fg] = a.astype(BF16)
        ab_ref[1, pair, off:off + fg, off:off + fg] = b.astype(BF16)


def _tables(decay_logit, w_fourier):
    c = RET_CHUNK
    idx = np.arange(FOURIER_GROUP)
    ang = 2.0 * np.pi * ((idx[:, None] * idx[None, :]) % FOURIER_GROUP) / FOURIER_GROUP
    cc = jnp.asarray(np.cos(ang) / np.sqrt(FOURIER_GROUP), F32)
    sc = jnp.asarray(np.sin(ang) / np.sqrt(FOURIER_GROUP), F32)
    return pl.pallas_call(
        _tables_kernel,
        out_shape=(
            jax.ShapeDtypeStruct((N_RET_HEADS, c, c), F32),
            jax.ShapeDtypeStruct((N_RET_HEADS, 2, c, 2 * c), BF16),
            jax.ShapeDtypeStruct((N_RET_HEADS, SUBLANES, 2 * c), F32),
            jax.ShapeDtypeStruct((2, N_FOURIER_GROUPS // 2, 2 * FOURIER_GROUP, 2 * FOURIER_GROUP), BF16),
        ),
        in_specs=[pl.BlockSpec(memory_space=pltpu.SMEM)] + [pl.BlockSpec(memory_space=pltpu.VMEM)] * 3,
        name="tables",
    )(decay_logit, w_fourier, cc, sc)


def _inproj_kernel(x_ref, shift_ref, mult_ref, cos_ref, sin_ref, *refs, n_w, rot_lo, rot_hi, silu_cols):
    w_refs, o_ref, wb_ref = refs[:n_w], refs[n_w], refs[n_w + 1]
    n_out = wb_ref.shape[1]
    hd = RET_HEAD_DIM
    bn = MXU_COLS

    @pl.when(jnp.logical_and(pl.program_id(0) == 0, pl.program_id(1) == 0))
    def _():
        for i, w_ref in enumerate(w_refs):
            width = w_ref.shape[1]
            for j in range(width // bn):
                wb_ref[:, i * width + j * bn:i * width + (j + 1) * bn] = w_ref[:, j * bn:(j + 1) * bn].astype(BF16)

    x = x_ref[0]
    ms = jnp.mean(x * x, axis=-1, keepdims=True)
    hb = (x * lax.rsqrt(ms + EPS) * mult_ref[0] + shift_ref[0]).astype(BF16)
    is_u1 = (lax.broadcasted_iota(jnp.int32, (x.shape[0], hd), 1) & (hd // 4)) == 0

    def epilogue_weight(j):
        lo = j * bn
        return (lo in silu_cols) + (rot_lo <= lo < rot_hi)

    for j in sorted(range(n_out // bn), key=epilogue_weight, reverse=True):
        p = jnp.dot(hb, wb_ref[:, j * bn:(j + 1) * bn], preferred_element_type=F32)
        for i in range(bn // hd):
            ph = p[:, i * hd:(i + 1) * hd]
            lo = j * bn + i * hd
            if lo in silu_cols:
                ph = _silu(ph)
            if rot_lo <= lo < rot_hi:
                partner = jnp.where(is_u1, pltpu.roll(ph, hd - hd // 4, axis=1), pltpu.roll(ph, hd // 4, axis=1))
                ph = ph * cos_ref[...] + partner * sin_ref[...]
            o_ref[0, :, lo:lo + hd] = ph.astype(BF16)


def _inproj(x, mod, w, cos_t, sin_t, *, mod_row, col_block, blocks, rot_lo, rot_hi, silu_cols, tm):
    bsz, length, _ = x.shape
    n_out = len(blocks) * col_block
    def mod_map(k):
        return (lambda b, t: (b, 0, k)) if mod_row is None else (lambda b, t: (mod_row, 0, k))
    rot_map = (lambda b, t: (t, 0)) if rot_hi > rot_lo else (lambda b, t: (0, 0))
    w_specs = [pl.BlockSpec((D_MODEL, col_block), functools.partial(lambda b, t, blk: (0, blk), blk=blk),
                            pipeline_mode=pl.Buffered(1)) for blk in blocks]
    return pl.pallas_call(
        functools.partial(_inproj_kernel, n_w=len(blocks), rot_lo=rot_lo, rot_hi=rot_hi,
                          silu_cols=frozenset(silu_cols)),
        grid=(bsz, length // tm),
        in_specs=[
            pl.BlockSpec((1, tm, D_MODEL), lambda b, t: (b, t, 0)),
            pl.BlockSpec((1, 1, D_MODEL), mod_map(MOD_SHIFT)),
            pl.BlockSpec((1, 1, D_MODEL), mod_map(MOD_MULT)),
            pl.BlockSpec((tm, RET_HEAD_DIM), rot_map),
            pl.BlockSpec((tm, RET_HEAD_DIM), rot_map),
        ] + w_specs,
        out_specs=pl.BlockSpec((1, tm, n_out), lambda b, t: (b, t, 0)),
        out_shape=jax.ShapeDtypeStruct((bsz, length, n_out), BF16),
        scratch_shapes=[pltpu.VMEM((D_MODEL, n_out), BF16)],
        compiler_params=pltpu.CompilerParams(
            dimension_semantics=("arbitrary", "arbitrary"), vmem_limit_bytes=VMEM_LIMIT),
        name="inproj",
    )(x, mod, mod, cos_t, sin_t, *([w] * len(blocks)))


def _rotary_tables(seq_len):
    n_freq = RET_HEAD_DIM // 4
    pos = np.arange(seq_len)
    row = (pos // GRID_W).astype(np.float64)
    col = (pos % GRID_W).astype(np.float64)
    inv_freq = ROPE_BASE ** (-np.arange(n_freq, dtype=np.float64) / n_freq)
    ang_r = row[:, None] * inv_freq[None, :]
    ang_c = col[:, None] * inv_freq[None, :]
    cos_t = np.concatenate([np.cos(ang_r), np.cos(ang_r), np.cos(ang_c), np.cos(ang_c)], axis=1)
    sin_t = np.concatenate([-np.sin(ang_r), np.sin(ang_r), -np.sin(ang_c), np.sin(ang_c)], axis=1)
    return jnp.asarray(cos_t, F32), jnp.asarray(sin_t, F32)


def _fourier_kernel(u_ref, g_ref, ab_ref, c_ref, s_ref, js_ref, o_ref, ue_ref, uo_ref, pe_ref, qo_ref, gs_ref):
    n = u_ref.shape[1]
    h = n // 2
    fb = FOLD_BLOCK
    nb = h // fb
    fg = FOURIER_GROUP
    inv_sqrt_n = 1.0 / np.sqrt(n)
    js = js_ref[...]
    first_row = lax.broadcasted_iota(jnp.int32, (fb, D_FOURIER), 0) == 0

    def channel_map(v, half):
        return jnp.concatenate(
            [jnp.dot(v[:, p * 2 * fg:(p + 1) * 2 * fg], ab_ref[half, p], preferred_element_type=F32)
             for p in range(N_FOURIER_GROUPS // 2)], axis=1)

    for i in range(nb):
        blk = slice(i * fb, (i + 1) * fb)
        lo = u_ref[0, blk, :].astype(F32)
        src = n - (i + 1) * fb
        t = jnp.dot(js, u_ref[0, src:src + fb, :], preferred_element_type=F32)
        if i > 0:
            t = jnp.where(first_row, u_ref[0, n - i * fb:n - i * fb + 16, :][0:1, :].astype(F32), t)
        ue_ref[blk, :] = (lo + t).astype(BF16)
        uo_ref[blk, :] = (lo - t).astype(BF16)

    for i in range(nb):
        blk = slice(i * fb, (i + 1) * fb)
        pe_ref[blk, :] = channel_map(ue_ref[blk, :], 0).astype(BF16)
        qo_ref[blk, :] = channel_map(uo_ref[blk, :], 1).astype(BF16)

    p_mid = channel_map(u_ref[0, h:h + 16, :], 0)[0:1, :] * inv_sqrt_n
    sign = (1 - 2 * (lax.broadcasted_iota(jnp.int32, (h, D_FOURIER), 0) & 1)).astype(F32)
    y_mid = jnp.sum(pe_ref[...].astype(F32) * sign, axis=0, keepdims=True) * inv_sqrt_n + p_mid

    mb = FOURIER_ROW_BLOCK
    sign_mb = (1 - 2 * (lax.broadcasted_iota(jnp.int32, (mb, D_FOURIER), 0) & 1)).astype(F32)
    for i in range(h // mb):
        blk = slice(i * mb, (i + 1) * mb)
        yc = jnp.dot(c_ref[blk, :], pe_ref[...], preferred_element_type=F32) + sign_mb * p_mid
        ys = jnp.dot(s_ref[blk, :], qo_ref[...], preferred_element_type=F32)
        o_ref[0, blk, :] = ((yc - ys) * g_ref[0, blk, :].astype(F32)).astype(BF16)
        gs_ref[blk, :] = (yc + ys).astype(BF16)

    for i in range(nb):
        src = h - (i + 1) * fb
        t = jnp.dot(js, gs_ref[src:src + fb, :], preferred_element_type=F32)
        edge = y_mid if i == 0 else gs_ref[h - i * fb:h - i * fb + 16, :][0:1, :].astype(F32)
        t = jnp.where(first_row, edge, t)
        rows = slice(h + i * fb, h + (i + 1) * fb)
        o_ref[0, rows, :] = (t * g_ref[0, rows, :].astype(F32)).astype(BF16)


def _fourier(proj, ab, c_mat, s_mat, js):
    bsz, length, _ = proj.shape
    h = length // 2
    return pl.pallas_call(
        _fourier_kernel,
        grid=(bsz,),
        in_specs=[
            pl.BlockSpec((1, length, D_FOURIER), lambda b: (b, 0, COL_F_IN // D_FOURIER)),
            pl.BlockSpec((1, length, D_FOURIER), lambda b: (b, 0, COL_F_GATE // D_FOURIER)),
            pl.BlockSpec((2, N_FOURIER_GROUPS // 2, 2 * FOURIER_GROUP, 2 * FOURIER_GROUP), lambda b: (0, 0, 0, 0)),
            pl.BlockSpec((h, h), lambda b: (0, 0)),
            pl.BlockSpec((h, h), lambda b: (0, 0)),
            pl.BlockSpec((FOLD_BLOCK, FOLD_BLOCK), lambda b: (0, 0)),
        ],
        out_specs=pl.BlockSpec((1, length, D_FOURIER), lambda b: (b, 0, 0)),
        out_shape=jax.ShapeDtypeStruct((bsz, length, D_FOURIER), BF16),
        scratch_shapes=[
            pltpu.VMEM((h, D_FOURIER), BF16),
            pltpu.VMEM((h, D_FOURIER), BF16),
            pltpu.VMEM((h, D_FOURIER), BF16),
            pltpu.VMEM((h, D_FOURIER), BF16),
            pltpu.VMEM((h, D_FOURIER), BF16),
        ],
        compiler_params=pltpu.CompilerParams(
            dimension_semantics=("parallel",), vmem_limit_bytes=VMEM_LIMIT),
        name="fourier",
    )(proj, proj, ab, c_mat, s_mat, js)


def _seq_dft_matrices(length):
    h = length // 2
    idx = np.arange(h)
    ang = 2.0 * np.pi * ((idx[:, None] * idx[None, :]) % length) / length
    scale = 1.0 / np.sqrt(length)
    c_mat = jnp.asarray((np.cos(ang) * scale).astype(np.float32)).astype(BF16)
    s_mat = jnp.asarray((np.sin(ang) * scale).astype(np.float32)).astype(BF16)
    r = np.arange(1, FOLD_BLOCK)
    js = np.zeros((FOLD_BLOCK, FOLD_BLOCK), np.float32)
    js[r, FOLD_BLOCK - r] = 1.0
    return c_mat, s_mat, jnp.asarray(js).astype(BF16)


def _ret_kernel(q_ref, k_ref, v_ref, rg_ref, kc_ref, vc_ref, dm_ref, rd_ref, cd_ref, gain_ref,
                o_ref, u_ref, st_ref, a_ref):
    c = RET_CHUNK
    hd = RET_HEAD_DIM
    n_chunks = q_ref.shape[1] // c
    n_ctx = kc_ref.shape[1] // c

    for h in range(N_RET_HEADS):
        cols = slice(h * hd, (h + 1) * hd)
        dm = dm_ref[h]
        q_dec = rd_ref[h, 0]
        k_dec = rd_ref[h, 1]
        c_dec_f = cd_ref[h, 0:1, 0:hd]
        c_dec_b = cd_ref[h, 0:1, hd:2 * hd]
        gain = gain_ref[:, cols]

        def both_decays(t, dec):
            return jnp.concatenate([t * dec[:, 0:hd], t * dec[:, hd:2 * hd]], axis=1)

        def kv_outer(k, v):
            return lax.dot_general(v, both_decays(k, k_dec), (((0,), (0,)), ((), ())), preferred_element_type=F32)

        for j in range(n_ctx):
            rows = slice(j * c, (j + 1) * c)
            u_ref[j] = kv_outer(kc_ref[0, rows, cols], vc_ref[0, rows, cols])
        for j in range(n_chunks):
            rows = slice(j * c, (j + 1) * c)
            u_ref[n_ctx + j] = kv_outer(k_ref[0, rows, cols], v_ref[0, rows, cols])

        sf = jnp.zeros((hd, hd), F32)
        for j in range(n_ctx + n_chunks - 1):
            sf = c_dec_f * sf + u_ref[j, :, 0:hd]
            if j + 1 >= n_ctx:
                st_ref[j + 1 - n_ctx, :, 0:hd] = sf.astype(BF16)
        sb = jnp.zeros((hd, hd), F32)
        for j in reversed(range(n_ctx)):
            sb = c_dec_b * sb + u_ref[j, :, hd:2 * hd]
        for j in reversed(range(n_chunks)):
            st_ref[j, :, hd:2 * hd] = sb.astype(BF16)
            if j > 0:
                sb = c_dec_b * sb + u_ref[n_ctx + j, :, hd:2 * hd]

        for j in range(n_chunks):
            rows = slice(j * c, (j + 1) * c)
            s = lax.dot_general(q_ref[0, rows, cols], k_ref[0, rows, cols], (((1,), (1,)), ((), ())),
                                preferred_element_type=F32)
            a_ref[j] = (s * dm).astype(BF16)

        for j in range(n_chunks):
            rows = slice(j * c, (j + 1) * c)
            o = jnp.dot(a_ref[j], v_ref[0, rows, cols], preferred_element_type=F32)
            o += lax.dot_general(both_decays(q_ref[0, rows, cols], q_dec), st_ref[j], (((1,), (1,)), ((), ())),
                                 preferred_element_type=F32)
            mu = jnp.mean(o, axis=-1, keepdims=True)
            d = o - mu
            var = jnp.mean(d * d, axis=-1, keepdims=True)
            on = d * lax.rsqrt(var + EPS)
            o_ref[0, rows, cols] = (on * gain * rg_ref[0, rows, cols].astype(F32)).astype(BF16)


def _retention(proj, proj_c, dm, rd, cd, gain):
    bsz, length, _ = proj.shape
    ctx_len = proj_c.shape[1]
    hd = RET_HEAD_DIM
    c = RET_CHUNK
    n_chunks = length // c
    n_ctx = ctx_len // c
    assert n_ctx >= 1

    def col(base):
        return lambda b: (b, 0, base // D_RET)

    return pl.pallas_call(
        _ret_kernel,
        grid=(bsz,),
        in_specs=[
            pl.BlockSpec((1, length, D_RET), col(COL_Q)),
            pl.BlockSpec((1, length, D_RET), col(COL_K)),
            pl.BlockSpec((1, length, D_RET), col(COL_V)),
            pl.BlockSpec((1, length, D_RET), col(COL_R_GATE)),
            pl.BlockSpec((1, ctx_len, D_RET), col(0)),
            pl.BlockSpec((1, ctx_len, D_RET), col(D_RET)),
            pl.BlockSpec((N_RET_HEADS, c, c), lambda b: (0, 0, 0)),
            pl.BlockSpec((N_RET_HEADS, 2, c, 2 * hd), lambda b: (0, 0, 0, 0)),
            pl.BlockSpec((N_RET_HEADS, SUBLANES, 2 * hd), lambda b: (0, 0, 0)),
            pl.BlockSpec((1, D_RET), lambda b: (0, 0)),
        ],
        out_specs=pl.BlockSpec((1, length, D_RET), lambda b: (b, 0, 0)),
        out_shape=jax.ShapeDtypeStruct((bsz, length, D_RET), BF16),
        scratch_shapes=[
            pltpu.VMEM((n_ctx + n_chunks, hd, 2 * hd), F32),
            pltpu.VMEM((n_chunks, hd, 2 * hd), BF16),
            pltpu.VMEM((n_chunks, c, c), BF16),
        ],
        compiler_params=pltpu.CompilerParams(
            dimension_semantics=("parallel",), vmem_limit_bytes=VMEM_LIMIT),
        name="retention",
    )(proj, proj, proj, proj, proj_c, proj_c, dm, rd, cd, gain)


def _outproj_kernel(yf_ref, yr_ref, w_ref, x_ref, gate_ref, o_ref, wb_ref):
    @pl.when(jnp.logical_and(pl.program_id(0) == 0, pl.program_id(1) == 0))
    def _():
        wb_ref[...] = w_ref[...].astype(BF16)

    rb = OUTPROJ_ROW_BLOCK
    for r in range(x_ref.shape[1] // rb):
        rows = slice(r * rb, (r + 1) * rb)
        y = jnp.dot(yf_ref[0, rows, :], wb_ref[0:D_FOURIER, :], preferred_element_type=F32)
        y += jnp.dot(yr_ref[0, rows, :], wb_ref[D_FOURIER:D_MIX, :], preferred_element_type=F32)
        ms = jnp.mean(y * y, axis=-1, keepdims=True)
        o_ref[0, rows, :] = x_ref[0, rows, :] + y * lax.rsqrt(ms + EPS) * gate_ref[0]


def _outproj(yf, yr, w_out, x, mod, *, tm):
    bsz, length, _ = x.shape
    return pl.pallas_call(
        _outproj_kernel,
        grid=(bsz, length // tm),
        in_specs=[
            pl.BlockSpec((1, tm, D_FOURIER), lambda b, t: (b, t, 0)),
            pl.BlockSpec((1, tm, D_RET), lambda b, t: (b, t, 0)),
            pl.BlockSpec((D_MIX, D_MODEL), lambda b, t: (0, 0), pipeline_mode=pl.Buffered(1)),
            pl.BlockSpec((1, tm, D_MODEL), lambda b, t: (b, t, 0)),
            pl.BlockSpec((1, 1, D_MODEL), lambda b, t: (b, 0, MOD_GATE)),
        ],
        out_specs=pl.BlockSpec((1, tm, D_MODEL), lambda b, t: (b, t, 0)),
        out_shape=jax.ShapeDtypeStruct((bsz, length, D_MODEL), F32),
        scratch_shapes=[pltpu.VMEM((D_MIX, D_MODEL), BF16)],
        compiler_params=pltpu.CompilerParams(
            dimension_semantics=("arbitrary", "arbitrary"), vmem_limit_bytes=VMEM_LIMIT),
        name="outproj",
    )(yf, yr, w_out, x, mod)


def kernel(x, c, ctx, c_ctx, w_ada, b_ada, g_pre, g_post, w_in, w_fourier, decay_logit, ret_gn_gain, w_out):
    bsz, seq_len, _ = x.shape
    ctx_len = ctx.shape[1]
    assert w_ada.shape[0] == 1, "single layer only"

    mod = _adaln(c, c_ctx, w_ada[0], b_ada[0], g_pre[0], g_post[0])
    mod = mod.reshape(mod.shape[0], 1, 3 * D_MODEL)

    dm, rd, cd, ab = _tables(decay_logit[0], w_fourier[0])

    cos_t, sin_t = _rotary_tables(seq_len)

    gate_cols = list(range(COL_F_GATE, COL_Q, RET_HEAD_DIM)) + list(range(COL_R_GATE, D_IN, RET_HEAD_DIM))
    proj = _inproj(x, mod, w_in[0], cos_t, sin_t, mod_row=None, col_block=D_IN, blocks=(0,),
                   rot_lo=COL_Q, rot_hi=COL_V, silu_cols=gate_cols, tm=TOKEN_TILE)
    ctx_rows = bsz * ctx_len
    assert COL_V == COL_K + D_RET and COL_K % D_RET == 0
    proj_c = _inproj(ctx.reshape(1, ctx_rows, D_MODEL), mod, w_in[0], cos_t, sin_t, mod_row=bsz,
                     col_block=D_RET, blocks=(COL_K // D_RET, COL_V // D_RET),
                     rot_lo=0, rot_hi=0, silu_cols=(), tm=min(TOKEN_TILE, ctx_rows))
    proj_c = proj_c.reshape(bsz, ctx_len, 2 * D_RET)

    assert (seq_len // 2) % FOLD_BLOCK == 0
    yf = _fourier(proj, ab, *_seq_dft_matrices(seq_len))
    yr = _retention(proj, proj_c, dm, rd, cd, ret_gn_gain[0].reshape(1, D_RET))
    return _outproj(yf, yr, w_out[0], x, mod, tm=seq_len)
```

```python
import functools

import numpy as np
import jax
import jax.numpy as jnp
from jax import lax
from jax.experimental import pallas as pl
from jax.experimental.pallas import tpu as pltpu

D_MODEL = 1024
GRID_W = 64
D_FOURIER = 512
N_FOURIER_GROUPS = 4
FOURIER_GROUP = D_FOURIER // N_FOURIER_GROUPS
D_RET = 512
N_RET_HEADS = 4
RET_HEAD_DIM = D_RET // N_RET_HEADS
D_MIX = D_FOURIER + D_RET
D_IN = 2 * D_FOURIER + 4 * D_RET
RET_CHUNK = 128
ROPE_BASE = 10000.0
QK_SCALE = RET_HEAD_DIM ** -0.5
EPS = 1e-6

COL_F_IN = 0
COL_F_GATE = D_FOURIER
COL_Q = 2 * D_FOURIER
COL_K = COL_Q + D_RET
COL_V = COL_K + D_RET
COL_R_GATE = COL_V + D_RET

MXU_COLS = 256
FOLD_BLOCK = 256
FOURIER_ROW_BLOCK = 512
OUTPROJ_ROW_BLOCK = 256
FUSED_STEPS = 2
SUBLANES = 8
MOD_SHIFT, MOD_MULT, MOD_GATE = 0, 1, 2
TOKEN_TILE = 1024
VMEM_LIMIT = 56 * 1024 * 1024

F32 = jnp.float32
BF16 = jnp.bfloat16


def _silu(v):
    return v * jax.nn.sigmoid(v)


def _adaln_kernel(c_ref, cc_ref, w_ref, b_ref, gpre_ref, gpost_ref, o_ref, s_ref):
    j = pl.program_id(0)
    nb = c_ref.shape[0]
    s_ref[0:nb, :] = _silu(c_ref[...])
    s_ref[nb:, :] = jnp.broadcast_to(_silu(cc_ref[...]), (s_ref.shape[0] - nb, D_MODEL))
    m = jnp.dot(s_ref[...], w_ref[...], preferred_element_type=F32) + b_ref[...]
    add = jnp.where(j == 1, 1.0, 0.0)
    mul = jnp.where(j == 1, gpre_ref[...], jnp.where(j == 2, gpost_ref[...], 1.0))
    o_ref[...] = (m + add) * mul


def _adaln(c, c_ctx, w_ada, b_ada, g_pre, g_post):
    bsz = c.shape[0]
    n_out = w_ada.shape[1]
    bn = D_MODEL
    assert n_out == 3 * bn and bsz % SUBLANES == 0
    rows = bsz + SUBLANES
    const = lambda j: (0, 0)
    return pl.pallas_call(
        _adaln_kernel,
        grid=(n_out // bn,),
        in_specs=[
            pl.BlockSpec((bsz, D_MODEL), const),
            pl.BlockSpec((1, D_MODEL), const),
            pl.BlockSpec((D_MODEL, bn), lambda j: (0, j)),
            pl.BlockSpec((1, bn), lambda j: (0, j)),
            pl.BlockSpec((1, D_MODEL), const),
            pl.BlockSpec((1, D_MODEL), const),
        ],
        out_specs=pl.BlockSpec((rows, bn), lambda j: (0, j)),
        out_shape=jax.ShapeDtypeStruct((rows, n_out), F32),
        scratch_shapes=[pltpu.VMEM((rows, D_MODEL), F32)],
        name="adaln",
    )(c, c_ctx.reshape(1, D_MODEL), w_ada, b_ada.reshape(1, n_out), g_pre.reshape(1, D_MODEL),
      g_post.reshape(1, D_MODEL))


def _tables_kernel(dl_ref, wf_ref, cc_ref, sc_ref, dm_ref, rd_ref, cd_ref, ab_ref):
    c = RET_CHUNK
    n = lax.broadcasted_iota(jnp.int32, (c, c), 0).astype(F32)
    m = lax.broadcasted_iota(jnp.int32, (c, c), 1).astype(F32)
    diff = n - m

    def log_sigmoid(v):
        return jnp.minimum(v, 0.0) - jnp.log1p(jnp.exp(-jnp.abs(v)))

    for h in range(N_RET_HEADS):
        lg_f = log_sigmoid(jnp.full((c, c), dl_ref[0, h], F32))
        lg_b = log_sigmoid(jnp.full((c, c), dl_ref[1, h], F32))
        dm_ref[h] = QK_SCALE * jnp.where(diff >= 0, jnp.exp(lg_f * jnp.maximum(diff, 0.0)),
                                         jnp.exp(lg_b * jnp.maximum(-diff, 0.0)))
        rd_ref[h, 0, :, 0:c] = (QK_SCALE * jnp.exp(lg_f * (n + 1.0))).astype(BF16)
        rd_ref[h, 0, :, c:2 * c] = (QK_SCALE * jnp.exp(lg_b * (c - n))).astype(BF16)
        rd_ref[h, 1, :, 0:c] = jnp.exp(lg_f * (c - 1.0 - n)).astype(BF16)
        rd_ref[h, 1, :, c:2 * c] = jnp.exp(lg_b * n).astype(BF16)
        cd_ref[h, :, 0:c] = jnp.exp(lg_f[0:SUBLANES, :] * float(c))
        cd_ref[h, :, c:2 * c] = jnp.exp(lg_b[0:SUBLANES, :] * float(c))
    fg = FOURIER_GROUP
    ab_ref[...] = jnp.zeros(ab_ref.shape, BF16)
    for g in range(N_FOURIER_GROUPS):
        wf = wf_ref[g]
        a = jnp.dot(cc_ref[...], wf, preferred_element_type=F32, precision=lax.Precision.HIGHEST)
        b = jnp.dot(sc_ref[...], wf, preferred_element_type=F32, precision=lax.Precision.HIGHEST)
        pair, off = g // 2, (g % 2) * fg
        ab_ref[0, pair, off:off + fg, off:off + fg] = a.astype(BF16)
        ab_ref[1, pair, off:off + fg, off:off + fg] = b.astype(BF16)


def _tables(decay_logit, w_fourier):
    c = RET_CHUNK
    idx = np.arange(FOURIER_GROUP)
    ang = 2.0 * np.pi * ((idx[:, None] * idx[None, :]) % FOURIER_GROUP) / FOURIER_GROUP
    cc = jnp.asarray(np.cos(ang) / np.sqrt(FOURIER_GROUP), F32)
    sc = jnp.asarray(np.sin(ang) / np.sqrt(FOURIER_GROUP), F32)
    return pl.pallas_call(
        _tables_kernel,
        out_shape=(
            jax.ShapeDtypeStruct((N_RET_HEADS, c, c), F32),
            jax.ShapeDtypeStruct((N_RET_HEADS, 2, c, 2 * c), BF16),
            jax.ShapeDtypeStruct((N_RET_HEADS, SUBLANES, 2 * c), F32),
            jax.ShapeDtypeStruct((2, N_FOURIER_GROUPS // 2, 2 * FOURIER_GROUP, 2 * FOURIER_GROUP), BF16),
        ),
        in_specs=[pl.BlockSpec(memory_space=pltpu.SMEM)] + [pl.BlockSpec(memory_space=pltpu.VMEM)] * 3,
        name="tables",
    )(decay_logit, w_fourier, cc, sc)


def _inproj_kernel(x_ref, shift_ref, mult_ref, cos_ref, sin_ref, *refs, n_w, rot_lo, rot_hi, silu_cols):
    w_refs, o_ref, wb_ref = refs[:n_w], refs[n_w], refs[n_w + 1]
    n_out = wb_ref.shape[1]
    hd = RET_HEAD_DIM
    bn = MXU_COLS

    @pl.when(jnp.logical_and(pl.program_id(0) == 0, pl.program_id(1) == 0))
    def _():
        for i, w_ref in enumerate(w_refs):
            width = w_ref.shape[1]
            for j in range(width // bn):
                wb_ref[:, i * width + j * bn:i * width + (j + 1) * bn] = w_ref[:, j * bn:(j + 1) * bn].astype(BF16)

    x = x_ref[0]
    ms = jnp.mean(x * x, axis=-1, keepdims=True)
    hb = (x * lax.rsqrt(ms + EPS) * mult_ref[0] + shift_ref[0]).astype(BF16)
    is_u1 = (lax.broadcasted_iota(jnp.int32, (x.shape[0], hd), 1) & (hd // 4)) == 0

    def epilogue_weight(j):
        lo = j * bn
        return (lo in silu_cols) + (rot_lo <= lo < rot_hi)

    for j in sorted(range(n_out // bn), key=epilogue_weight, reverse=True):
        p = jnp.dot(hb, wb_ref[:, j * bn:(j + 1) * bn], preferred_element_type=F32)
        for i in range(bn // hd):
            ph = p[:, i * hd:(i + 1) * hd]
            lo = j * bn + i * hd
            if lo in silu_cols:
                ph = _silu(ph)
            if rot_lo <= lo < rot_hi:
                partner = jnp.where(is_u1, pltpu.roll(ph, hd - hd // 4, axis=1), pltpu.roll(ph, hd // 4, axis=1))
                ph = ph * cos_ref[...] + partner * sin_ref[...]
            o_ref[0, :, lo:lo + hd] = ph.astype(BF16)


def _inproj(x, mod, w, cos_t, sin_t, *, mod_row, col_block, blocks, rot_lo, rot_hi, silu_cols, tm):
    bsz, length, _ = x.shape
    n_out = len(blocks) * col_block
    def mod_map(k):
        return (lambda b, t: (b, 0, k)) if mod_row is None else (lambda b, t: (mod_row, 0, k))
    rot_map = (lambda b, t: (t, 0)) if rot_hi > rot_lo else (lambda b, t: (0, 0))
    w_specs = [pl.BlockSpec((D_MODEL, col_block), functools.partial(lambda b, t, blk: (0, blk), blk=blk),
                            pipeline_mode=pl.Buffered(1)) for blk in blocks]
    return pl.pallas_call(
        functools.partial(_inproj_kernel, n_w=len(blocks), rot_lo=rot_lo, rot_hi=rot_hi,
                          silu_cols=frozenset(silu_cols)),
        grid=(bsz, length // tm),
        in_specs=[
            pl.BlockSpec((1, tm, D_MODEL), lambda b, t: (b, t, 0)),
            pl.BlockSpec((1, 1, D_MODEL), mod_map(MOD_SHIFT)),
            pl.BlockSpec((1, 1, D_MODEL), mod_map(MOD_MULT)),
            pl.BlockSpec((tm, RET_HEAD_DIM), rot_map),
            pl.BlockSpec((tm, RET_HEAD_DIM), rot_map),
        ] + w_specs,
        out_specs=pl.BlockSpec((1, tm, n_out), lambda b, t: (b, t, 0)),
        out_shape=jax.ShapeDtypeStruct((bsz, length, n_out), BF16),
        scratch_shapes=[pltpu.VMEM((D_MODEL, n_out), BF16)],
        compiler_params=pltpu.CompilerParams(
            dimension_semantics=("arbitrary", "arbitrary"), vmem_limit_bytes=VMEM_LIMIT),
        name="inproj",
    )(x, mod, mod, cos_t, sin_t, *([w] * len(blocks)))


def _rotary_tables(seq_len):
    n_freq = RET_HEAD_DIM // 4
    pos = np.arange(seq_len)
    row = (pos // GRID_W).astype(np.float64)
    col = (pos % GRID_W).astype(np.float64)
    inv_freq = ROPE_BASE ** (-np.arange(n_freq, dtype=np.float64) / n_freq)
    ang_r = row[:, None] * inv_freq[None, :]
    ang_c = col[:, None] * inv_freq[None, :]
    cos_t = np.concatenate([np.cos(ang_r), np.cos(ang_r), np.cos(ang_c), np.cos(ang_c)], axis=1)
    sin_t = np.concatenate([-np.sin(ang_r), np.sin(ang_r), -np.sin(ang_c), np.sin(ang_c)], axis=1)
    return jnp.asarray(cos_t, F32), jnp.asarray(sin_t, F32)


def _fourier_outproj_kernel(u_ref, g_ref, ab_ref, c_ref, s_ref, js_ref, yr_ref, w_ref, x_ref, gate_ref,
                            o_ref, ue_ref, uo_ref, pe_ref, qo_ref, gs_ref, mid_ref, yf_ref, wb_ref):
    b = pl.program_id(0)
    t = pl.program_id(1)
    n_batch = pl.num_programs(0) - 1
    slot = b % 2
    n = u_ref.shape[1]
    h = n // 2
    fb = FOLD_BLOCK
    nb = h // fb
    mb = FOURIER_ROW_BLOCK
    fg = FOURIER_GROUP
    inv_sqrt_n = 1.0 / np.sqrt(n)
    first_row = lax.broadcasted_iota(jnp.int32, (fb, D_FOURIER), 0) == 0

    @pl.when(jnp.logical_and(b == 0, t == 0))
    def _():
        wb_ref[...] = w_ref[...].astype(BF16)

    def channel_map(v, half):
        return jnp.concatenate(
            [jnp.dot(v[:, p * 2 * fg:(p + 1) * 2 * fg], ab_ref[half, p], preferred_element_type=F32)
             for p in range(N_FOURIER_GROUPS // 2)], axis=1)

    def fold_and_channel_map():
        js = js_ref[...]
        for i in range(nb):
            blk = slice(i * fb, (i + 1) * fb)
            lo = u_ref[0, blk, :].astype(F32)
            src = n - (i + 1) * fb
            r = jnp.dot(js, u_ref[0, src:src + fb, :], preferred_element_type=F32)
            if i > 0:
                r = jnp.where(first_row, u_ref[0, n - i * fb:n - i * fb + 16, :][0:1, :].astype(F32), r)
            ue_ref[blk, :] = (lo + r).astype(BF16)
            uo_ref[blk, :] = (lo - r).astype(BF16)
        for i in range(nb):
            blk = slice(i * fb, (i + 1) * fb)
            pe_ref[blk, :] = channel_map(ue_ref[blk, :], 0).astype(BF16)
            qo_ref[blk, :] = channel_map(uo_ref[blk, :], 1).astype(BF16)
        p_mid = channel_map(u_ref[0, h:h + 16, :], 0)[0:1, :] * inv_sqrt_n
        sign = (1 - 2 * (lax.broadcasted_iota(jnp.int32, (h, D_FOURIER), 0) & 1)).astype(F32)
        y_mid = jnp.sum(pe_ref[...].astype(F32) * sign, axis=0, keepdims=True) * inv_sqrt_n + p_mid
        mid_ref[0] = jnp.broadcast_to(p_mid, (SUBLANES, D_FOURIER))
        mid_ref[1] = jnp.broadcast_to(y_mid, (SUBLANES, D_FOURIER))

    def dft_block(i):
        blk = slice(i * mb, (i + 1) * mb)
        sign = (1 - 2 * (lax.broadcasted_iota(jnp.int32, (mb, D_FOURIER), 0) & 1)).astype(F32)
        yc = jnp.dot(c_ref[blk, :], pe_ref[...], preferred_element_type=F32) + sign * mid_ref[0, 0:1, :]
        ys = jnp.dot(s_ref[blk, :], qo_ref[...], preferred_element_type=F32)
        yf_ref[slot, blk, :] = ((yc - ys) * g_ref[0, blk, :].astype(F32)).astype(BF16)
        gs_ref[blk, :] = (yc + ys).astype(BF16)

    def reflect_upper_half():
        js = js_ref[...]
        for i in range(nb):
            src = h - (i + 1) * fb
            r = jnp.dot(js, gs_ref[src:src + fb, :], preferred_element_type=F32)
            edge = mid_ref[1, 0:1, :] if i == 0 else gs_ref[h - i * fb:h - i * fb + 16, :][0:1, :].astype(F32)
            r = jnp.where(first_row, edge, r)
            rows = slice(h + i * fb, h + (i + 1) * fb)
            yf_ref[slot, rows, :] = (r * g_ref[0, rows, :].astype(F32)).astype(BF16)

    phases = [fold_and_channel_map] + [functools.partial(dft_block, i) for i in range(h // mb)] + [reflect_upper_half]
    per_step = len(phases) // FUSED_STEPS
    assert per_step * FUSED_STEPS == len(phases)
    for k in range(FUSED_STEPS):
        def run(k=k):
            for phase in phases[k * per_step:(k + 1) * per_step]:
                phase()
        pl.when(jnp.logical_and(b < n_batch, t == k))(run)

    @pl.when(b > 0)
    def _():
        tm = x_ref.shape[1]
        rb = OUTPROJ_ROW_BLOCK
        for r in range(tm // rb):
            rows = slice(r * rb, (r + 1) * rb)
            yf_rows = pl.ds(pl.multiple_of(t * tm, tm) + r * rb, rb)
            y = jnp.dot(yf_ref[1 - slot, yf_rows, :], wb_ref[0:D_FOURIER, :], preferred_element_type=F32)
            y += jnp.dot(yr_ref[0, rows, :], wb_ref[D_FOURIER:D_MIX, :], preferred_element_type=F32)
            ms = jnp.mean(y * y, axis=-1, keepdims=True)
            o_ref[0, rows, :] = x_ref[0, rows, :] + y * lax.rsqrt(ms + EPS) * gate_ref[0]


def _fourier_outproj(proj, yr, ab, c_mat, s_mat, js, w_out, x, mod):
    bsz, length, _ = x.shape
    h = length // 2
    assert h % FOLD_BLOCK == 0 and h % FOURIER_ROW_BLOCK == 0 and length % FUSED_STEPS == 0
    tm = length // FUSED_STEPS
    assert tm % OUTPROJ_ROW_BLOCK == 0

    def cur(col):
        return lambda b, t: (jnp.minimum(b, bsz - 1), 0, col)

    def prev_tile(b, t):
        return (jnp.maximum(b - 1, 0), jnp.where(b == 0, 0, t), 0)

    const2 = lambda b, t: (0, 0)
    resident = pl.Buffered(1)
    return pl.pallas_call(
        _fourier_outproj_kernel,
        grid=(bsz + 1, FUSED_STEPS),
        in_specs=[
            pl.BlockSpec((1, length, D_FOURIER), cur(COL_F_IN // D_FOURIER)),
            pl.BlockSpec((1, length, D_FOURIER), cur(COL_F_GATE // D_FOURIER)),
            pl.BlockSpec((2, N_FOURIER_GROUPS // 2, 2 * FOURIER_GROUP, 2 * FOURIER_GROUP), lambda b, t: (0, 0, 0, 0),
                         pipeline_mode=resident),
            pl.BlockSpec((h, h), const2, pipeline_mode=resident),
            pl.BlockSpec((h, h), const2, pipeline_mode=resident),
            pl.BlockSpec((FOLD_BLOCK, FOLD_BLOCK), const2, pipeline_mode=resident),
            pl.BlockSpec((1, tm, D_RET), prev_tile),
            pl.BlockSpec((D_MIX, D_MODEL), const2, pipeline_mode=resident),
            pl.BlockSpec((1, tm, D_MODEL), prev_tile),
            pl.BlockSpec((1, 1, D_MODEL), lambda b, t: (jnp.maximum(b - 1, 0), 0, MOD_GATE)),
        ],
        out_specs=pl.BlockSpec((1, tm, D_MODEL), prev_tile),
        out_shape=jax.ShapeDtypeStruct((bsz, length, D_MODEL), F32),
        scratch_shapes=[
            pltpu.VMEM((h, D_FOURIER), BF16),
            pltpu.VMEM((h, D_FOURIER), BF16),
            pltpu.VMEM((h, D_FOURIER), BF16),
            pltpu.VMEM((h, D_FOURIER), BF16),
            pltpu.VMEM((h, D_FOURIER), BF16),
            pltpu.VMEM((2, SUBLANES, D_FOURIER), F32),
            pltpu.VMEM((2, length, D_FOURIER), BF16),
            pltpu.VMEM((D_MIX, D_MODEL), BF16),
        ],
        compiler_params=pltpu.CompilerParams(
            dimension_semantics=("arbitrary", "arbitrary"), vmem_limit_bytes=VMEM_LIMIT),
        name="fourier_outproj",
    )(proj, proj, ab, c_mat, s_mat, js, yr, w_out, x, mod)


def _seq_dft_matrices(length):
    h = length // 2
    idx = np.arange(h)
    ang = 2.0 * np.pi * ((idx[:, None] * idx[None, :]) % length) / length
    scale = 1.0 / np.sqrt(length)
    c_mat = jnp.asarray((np.cos(ang) * scale).astype(np.float32)).astype(BF16)
    s_mat = jnp.asarray((np.sin(ang) * scale).astype(np.float32)).astype(BF16)
    r = np.arange(1, FOLD_BLOCK)
    js = np.zeros((FOLD_BLOCK, FOLD_BLOCK), np.float32)
    js[r, FOLD_BLOCK - r] = 1.0
    return c_mat, s_mat, jnp.asarray(js).astype(BF16)


def _ret_kernel(q_ref, k_ref, v_ref, rg_ref, kc_ref, vc_ref, dm_ref, rd_ref, cd_ref, gain_ref,
                o_ref, u_ref, st_ref, a_ref):
    c = RET_CHUNK
    hd = RET_HEAD_DIM
    n_chunks = q_ref.shape[1] // c
    n_ctx = kc_ref.shape[1] // c

    for h in range(N_RET_HEADS):
        cols = slice(h * hd, (h + 1) * hd)
        dm = dm_ref[h]
        q_dec = rd_ref[h, 0]
        k_dec = rd_ref[h, 1]
        c_dec_f = cd_ref[h, 0:1, 0:hd]
        c_dec_b = cd_ref[h, 0:1, hd:2 * hd]
        gain = gain_ref[:, cols]

        def both_decays(t, dec):
            return jnp.concatenate([t * dec[:, 0:hd], t * dec[:, hd:2 * hd]], axis=1)

        def kv_outer(k, v):
            return lax.dot_general(v, both_decays(k, k_dec), (((0,), (0,)), ((), ())), preferred_element_type=F32)

        for j in range(n_ctx):
            rows = slice(j * c, (j + 1) * c)
            u_ref[j] = kv_outer(kc_ref[0, rows, cols], vc_ref[0, rows, cols])
        for j in range(n_chunks):
            rows = slice(j * c, (j + 1) * c)
            u_ref[n_ctx + j] = kv_outer(k_ref[0, rows, cols], v_ref[0, rows, cols])

        sf = jnp.zeros((hd, hd), F32)
        for j in range(n_ctx + n_chunks - 1):
            sf = c_dec_f * sf + u_ref[j, :, 0:hd]
            if j + 1 >= n_ctx:
                st_ref[j + 1 - n_ctx, :, 0:hd] = sf.astype(BF16)
        sb = jnp.zeros((hd, hd), F32)
        for j in reversed(range(n_ctx)):
            sb = c_dec_b * sb + u_ref[j, :, hd:2 * hd]
        for j in reversed(range(n_chunks)):
            st_ref[j, :, hd:2 * hd] = sb.astype(BF16)
            if j > 0:
                sb = c_dec_b * sb + u_ref[n_ctx + j, :, hd:2 * hd]

        for j in range(n_chunks):
            rows = slice(j * c, (j + 1) * c)
            s = lax.dot_general(q_ref[0, rows, cols], k_ref[0, rows, cols], (((1,), (1,)), ((), ())),
                                preferred_element_type=F32)
            a_ref[j] = (s * dm).astype(BF16)

        for j in range(n_chunks):
            rows = slice(j * c, (j + 1) * c)
            o = jnp.dot(a_ref[j], v_ref[0, rows, cols], preferred_element_type=F32)
            o += lax.dot_general(both_decays(q_ref[0, rows, cols], q_dec), st_ref[j], (((1,), (1,)), ((), ())),
                                 preferred_element_type=F32)
            mu = jnp.mean(o, axis=-1, keepdims=True)
            d = o - mu
            var = jnp.mean(d * d, axis=-1, keepdims=True)
            on = d * lax.rsqrt(var + EPS)
            o_ref[0, rows, cols] = (on * gain * rg_ref[0, rows, cols].astype(F32)).astype(BF16)


def _retention(proj, proj_c, dm, rd, cd, gain):
    bsz, length, _ = proj.shape
    ctx_len = proj_c.shape[1]
    hd = RET_HEAD_DIM
    c = RET_CHUNK
    n_chunks = length // c
    n_ctx = ctx_len // c
    assert n_ctx >= 1

    def col(base):
        return lambda b: (b, 0, base // D_RET)

    return pl.pallas_call(
        _ret_kernel,
        grid=(bsz,),
        in_specs=[
            pl.BlockSpec((1, length, D_RET), col(COL_Q)),
            pl.BlockSpec((1, length, D_RET), col(COL_K)),
            pl.BlockSpec((1, length, D_RET), col(COL_V)),
            pl.BlockSpec((1, length, D_RET), col(COL_R_GATE)),
            pl.BlockSpec((1, ctx_len, D_RET), col(0)),
            pl.BlockSpec((1, ctx_len, D_RET), col(D_RET)),
            pl.BlockSpec((N_RET_HEADS, c, c), lambda b: (0, 0, 0)),
            pl.BlockSpec((N_RET_HEADS, 2, c, 2 * hd), lambda b: (0, 0, 0, 0)),
            pl.BlockSpec((N_RET_HEADS, SUBLANES, 2 * hd), lambda b: (0, 0, 0)),
            pl.BlockSpec((1, D_RET), lambda b: (0, 0)),
        ],
        out_specs=pl.BlockSpec((1, length, D_RET), lambda b: (b, 0, 0)),
        out_shape=jax.ShapeDtypeStruct((bsz, length, D_RET), BF16),
        scratch_shapes=[
            pltpu.VMEM((n_ctx + n_chunks, hd, 2 * hd), F32),
            pltpu.VMEM((n_chunks, hd, 2 * hd), BF16),
            pltpu.VMEM((n_chunks, c, c), BF16),
        ],
        compiler_params=pltpu.CompilerParams(
            dimension_semantics=("parallel",), vmem_limit_bytes=VMEM_LIMIT),
        name="retention",
    )(proj, proj, proj, proj, proj_c, proj_c, dm, rd, cd, gain)


def kernel(x, c, ctx, c_ctx, w_ada, b_ada, g_pre, g_post, w_in, w_fourier, decay_logit, ret_gn_gain, w_out):
    bsz, seq_len, _ = x.shape
    ctx_len = ctx.shape[1]
    assert w_ada.shape[0] == 1, "single layer only"

    mod = _adaln(c, c_ctx, w_ada[0], b_ada[0], g_pre[0], g_post[0])
    mod = mod.reshape(mod.shape[0], 1, 3 * D_MODEL)

    dm, rd, cd, ab = _tables(decay_logit[0], w_fourier[0])

    cos_t, sin_t = _rotary_tables(seq_len)

    gate_cols = list(range(COL_F_GATE, COL_Q, RET_HEAD_DIM)) + list(range(COL_R_GATE, D_IN, RET_HEAD_DIM))
    proj = _inproj(x, mod, w_in[0], cos_t, sin_t, mod_row=None, col_block=D_IN, blocks=(0,),
                   rot_lo=COL_Q, rot_hi=COL_V, silu_cols=gate_cols, tm=TOKEN_TILE)
    ctx_rows = bsz * ctx_len
    assert COL_V == COL_K + D_RET and COL_K % D_RET == 0
    proj_c = _inproj(ctx.reshape(1, ctx_rows, D_MODEL), mod, w_in[0], cos_t, sin_t, mod_row=bsz,
                     col_block=D_RET, blocks=(COL_K // D_RET, COL_V // D_RET),
                     rot_lo=0, rot_hi=0, silu_cols=(), tm=min(TOKEN_TILE, ctx_rows))
    proj_c = proj_c.reshape(bsz, ctx_len, 2 * D_RET)

    yr = _retention(proj, proj_c, dm, rd, cd, ret_gn_gain[0].reshape(1, D_RET))
    return _fourier_outproj(proj, yr, ab, *_seq_dft_matrices(seq_len), w_out[0], x, mod)
```

```python
import functools

import numpy as np
import jax
import jax.numpy as jnp
from jax import lax
from jax.experimental import pallas as pl
from jax.experimental.pallas import tpu as pltpu

D_MODEL = 1024
GRID_W = 64
D_FOURIER = 512
N_FOURIER_GROUPS = 4
FOURIER_GROUP = D_FOURIER // N_FOURIER_GROUPS
D_RET = 512
N_RET_HEADS = 4
RET_HEAD_DIM = D_RET // N_RET_HEADS
D_MIX = D_FOURIER + D_RET
D_IN = 2 * D_FOURIER + 4 * D_RET
RET_CHUNK = 128
ROPE_BASE = 10000.0
QK_SCALE = RET_HEAD_DIM ** -0.5
EPS = 1e-6

COL_F_IN = 0
COL_F_GATE = D_FOURIER
COL_Q = 2 * D_FOURIER
COL_K = COL_Q + D_RET
COL_V = COL_K + D_RET
COL_R_GATE = COL_V + D_RET

MXU_COLS = 256
FOLD_BLOCK = 256
FOURIER_ROW_BLOCK = 512
OUTPROJ_ROW_BLOCK = 256
FUSED_STEPS = 2
SUBLANES = 8
MOD_SHIFT, MOD_MULT, MOD_GATE = 0, 1, 2
TOKEN_TILE = 1024
VMEM_LIMIT = 56 * 1024 * 1024

F32 = jnp.float32
BF16 = jnp.bfloat16


def _silu(v):
    return v * jax.nn.sigmoid(v)


def _adaln_kernel(c_ref, cc_ref, w_ref, b_ref, gpre_ref, gpost_ref, o_ref, s_ref):
    j = pl.program_id(0)
    nb = c_ref.shape[0]
    s_ref[0:nb, :] = _silu(c_ref[...])
    s_ref[nb:, :] = jnp.broadcast_to(_silu(cc_ref[...]), (s_ref.shape[0] - nb, D_MODEL))
    m = jnp.dot(s_ref[...], w_ref[...], preferred_element_type=F32) + b_ref[...]
    add = jnp.where(j == 1, 1.0, 0.0)
    mul = jnp.where(j == 1, gpre_ref[...], jnp.where(j == 2, gpost_ref[...], 1.0))
    o_ref[...] = ((m + add) * mul)[:, None, :]


def _adaln(c, c_ctx, w_ada, b_ada, g_pre, g_post):
    bsz = c.shape[0]
    n_out = w_ada.shape[1]
    bn = D_MODEL
    assert n_out == 3 * bn and bsz % SUBLANES == 0
    rows = bsz + SUBLANES
    const = lambda j: (0, 0)
    return pl.pallas_call(
        _adaln_kernel,
        grid=(n_out // bn,),
        in_specs=[
            pl.BlockSpec((bsz, D_MODEL), const),
            pl.BlockSpec((1, D_MODEL), const),
            pl.BlockSpec((D_MODEL, bn), lambda j: (0, j)),
            pl.BlockSpec((1, bn), lambda j: (0, j)),
            pl.BlockSpec((1, D_MODEL), const),
            pl.BlockSpec((1, D_MODEL), const),
        ],
        out_specs=pl.BlockSpec((rows, 1, bn), lambda j: (0, 0, j)),
        out_shape=jax.ShapeDtypeStruct((rows, 1, n_out), F32),
        scratch_shapes=[pltpu.VMEM((rows, D_MODEL), F32)],
        name="adaln",
    )(c, c_ctx.reshape(1, D_MODEL), w_ada, b_ada.reshape(1, n_out), g_pre.reshape(1, D_MODEL),
      g_post.reshape(1, D_MODEL))


def _tables_kernel(dl_ref, wf_ref, cc_ref, sc_ref, dm_ref, rd_ref, cd_ref, ab_ref):
    c = RET_CHUNK
    n = lax.broadcasted_iota(jnp.int32, (c, c), 0).astype(F32)
    m = lax.broadcasted_iota(jnp.int32, (c, c), 1).astype(F32)
    diff = n - m

    def log_sigmoid(v):
        return jnp.minimum(v, 0.0) - jnp.log1p(jnp.exp(-jnp.abs(v)))

    for h in range(N_RET_HEADS):
        lg_f = log_sigmoid(jnp.full((c, c), dl_ref[0, h], F32))
        lg_b = log_sigmoid(jnp.full((c, c), dl_ref[1, h], F32))
        dm_ref[h] = QK_SCALE * jnp.where(diff >= 0, jnp.exp(lg_f * jnp.maximum(diff, 0.0)),
                                         jnp.exp(lg_b * jnp.maximum(-diff, 0.0)))
        rd_ref[h, 0, :, 0:c] = (QK_SCALE * jnp.exp(lg_f * (n + 1.0))).astype(BF16)
        rd_ref[h, 0, :, c:2 * c] = (QK_SCALE * jnp.exp(lg_b * (c - n))).astype(BF16)
        rd_ref[h, 1, :, 0:c] = jnp.exp(lg_f * (c - 1.0 - n)).astype(BF16)
        rd_ref[h, 1, :, c:2 * c] = jnp.exp(lg_b * n).astype(BF16)
        cd_ref[h, :, 0:c] = jnp.exp(lg_f[0:SUBLANES, :] * float(c))
        cd_ref[h, :, c:2 * c] = jnp.exp(lg_b[0:SUBLANES, :] * float(c))
    fg = FOURIER_GROUP
    ab_ref[...] = jnp.zeros(ab_ref.shape, BF16)
    for g in range(N_FOURIER_GROUPS):
        wf = wf_ref[g]
        a = jnp.dot(cc_ref[...], wf, preferred_element_type=F32, precision=lax.Precision.HIGHEST)
        b = jnp.dot(sc_ref[...], wf, preferred_element_type=F32, precision=lax.Precision.HIGHEST)
        pair, off = g // 2, (g % 2) * fg
        ab_ref[0, pair, off:off + fg, off:off + fg] = a.astype(BF16)
        ab_ref[1, pair, off:off + fg, off:off + fg] = b.astype(BF16)


def _tables(decay_logit, w_fourier):
    c = RET_CHUNK
    idx = np.arange(FOURIER_GROUP)
    ang = 2.0 * np.pi * ((idx[:, None] * idx[None, :]) % FOURIER_GROUP) / FOURIER_GROUP
    cc = jnp.asarray(np.cos(ang) / np.sqrt(FOURIER_GROUP), F32)
    sc = jnp.asarray(np.sin(ang) / np.sqrt(FOURIER_GROUP), F32)
    return pl.pallas_call(
        _tables_kernel,
        out_shape=(
            jax.ShapeDtypeStruct((N_RET_HEADS, c, c), F32),
            jax.ShapeDtypeStruct((N_RET_HEADS, 2, c, 2 * c), BF16),
            jax.ShapeDtypeStruct((N_RET_HEADS, SUBLANES, 2 * c), F32),
            jax.ShapeDtypeStruct((2, N_FOURIER_GROUPS // 2, 2 * FOURIER_GROUP, 2 * FOURIER_GROUP), BF16),
        ),
        in_specs=[pl.BlockSpec(memory_space=pltpu.SMEM)] + [pl.BlockSpec(memory_space=pltpu.VMEM)] * 3,
        name="tables",
    )(decay_logit, w_fourier, cc, sc)


def _inproj_kernel(x_ref, shift_ref, mult_ref, cos_ref, sin_ref, *refs, n_w, rot_lo, rot_hi, silu_cols):
    w_refs, o_ref, wb_ref = refs[:n_w], refs[n_w], refs[n_w + 1]
    n_out = wb_ref.shape[1]
    hd = RET_HEAD_DIM
    bn = MXU_COLS

    @pl.when(jnp.logical_and(pl.program_id(0) == 0, pl.program_id(1) == 0))
    def _():
        for i, w_ref in enumerate(w_refs):
            width = w_ref.shape[1]
            for j in range(width // bn):
                wb_ref[:, i * width + j * bn:i * width + (j + 1) * bn] = w_ref[:, j * bn:(j + 1) * bn].astype(BF16)

    x = x_ref[0]
    ms = jnp.mean(x * x, axis=-1, keepdims=True)
    hb = (x * lax.rsqrt(ms + EPS) * mult_ref[0] + shift_ref[0]).astype(BF16)
    is_u1 = (lax.broadcasted_iota(jnp.int32, (x.shape[0], hd), 1) & (hd // 4)) == 0

    def epilogue_weight(j):
        lo = j * bn
        return (lo in silu_cols) + (rot_lo <= lo < rot_hi)

    for j in sorted(range(n_out // bn), key=epilogue_weight, reverse=True):
        p = jnp.dot(hb, wb_ref[:, j * bn:(j + 1) * bn], preferred_element_type=F32)
        for i in range(bn // hd):
            ph = p[:, i * hd:(i + 1) * hd]
            lo = j * bn + i * hd
            if lo in silu_cols:
                ph = _silu(ph)
            if rot_lo <= lo < rot_hi:
                partner = jnp.where(is_u1, pltpu.roll(ph, hd - hd // 4, axis=1), pltpu.roll(ph, hd // 4, axis=1))
                ph = ph * cos_ref[...] + partner * sin_ref[...]
            o_ref[0, :, lo:lo + hd] = ph.astype(BF16)


def _inproj(x, mod, w, cos_t, sin_t, *, mod_row, col_block, blocks, rot_lo, rot_hi, silu_cols, tm):
    bsz, length, _ = x.shape
    n_out = len(blocks) * col_block
    def mod_map(k):
        return (lambda b, t: (b, 0, k)) if mod_row is None else (lambda b, t: (mod_row, 0, k))
    rot_map = (lambda b, t: (t, 0)) if rot_hi > rot_lo else (lambda b, t: (0, 0))
    w_specs = [pl.BlockSpec((D_MODEL, col_block), functools.partial(lambda b, t, blk: (0, blk), blk=blk),
                            pipeline_mode=pl.Buffered(1)) for blk in blocks]
    return pl.pallas_call(
        functools.partial(_inproj_kernel, n_w=len(blocks), rot_lo=rot_lo, rot_hi=rot_hi,
                          silu_cols=frozenset(silu_cols)),
        grid=(bsz, length // tm),
        in_specs=[
            pl.BlockSpec((1, tm, D_MODEL), lambda b, t: (b, t, 0)),
            pl.BlockSpec((1, 1, D_MODEL), mod_map(MOD_SHIFT)),
            pl.BlockSpec((1, 1, D_MODEL), mod_map(MOD_MULT)),
            pl.BlockSpec((tm, RET_HEAD_DIM), rot_map),
            pl.BlockSpec((tm, RET_HEAD_DIM), rot_map),
        ] + w_specs,
        out_specs=pl.BlockSpec((1, tm, n_out), lambda b, t: (b, t, 0)),
        out_shape=jax.ShapeDtypeStruct((bsz, length, n_out), BF16),
        scratch_shapes=[pltpu.VMEM((D_MODEL, n_out), BF16)],
        compiler_params=pltpu.CompilerParams(
            dimension_semantics=("arbitrary", "arbitrary"), vmem_limit_bytes=VMEM_LIMIT),
        name="inproj",
    )(x, mod, mod, cos_t, sin_t, *([w] * len(blocks)))


def _rotary_tables(seq_len):
    n_freq = RET_HEAD_DIM // 4
    pos = np.arange(seq_len)
    row = (pos // GRID_W).astype(np.float64)
    col = (pos % GRID_W).astype(np.float64)
    inv_freq = ROPE_BASE ** (-np.arange(n_freq, dtype=np.float64) / n_freq)
    ang_r = row[:, None] * inv_freq[None, :]
    ang_c = col[:, None] * inv_freq[None, :]
    cos_t = np.concatenate([np.cos(ang_r), np.cos(ang_r), np.cos(ang_c), np.cos(ang_c)], axis=1)
    sin_t = np.concatenate([-np.sin(ang_r), np.sin(ang_r), -np.sin(ang_c), np.sin(ang_c)], axis=1)
    return jnp.asarray(cos_t, F32), jnp.asarray(sin_t, F32)


def _fourier_outproj_kernel(u_ref, g_ref, ab_ref, c_ref, s_ref, js_ref, yr_ref, w_ref, x_ref, gate_ref,
                            o_ref, ue_ref, uo_ref, pe_ref, qo_ref, gs_ref, mid_ref, yf_ref, wb_ref):
    b = pl.program_id(0)
    t = pl.program_id(1)
    n_batch = pl.num_programs(0) - 1
    slot = b % 2
    n = u_ref.shape[1]
    h = n // 2
    fb = FOLD_BLOCK
    nb = h // fb
    mb = FOURIER_ROW_BLOCK
    fg = FOURIER_GROUP
    inv_sqrt_n = 1.0 / np.sqrt(n)
    first_row = lax.broadcasted_iota(jnp.int32, (fb, D_FOURIER), 0) == 0

    @pl.when(jnp.logical_and(b == 0, t == 0))
    def _():
        wb_ref[...] = w_ref[...].astype(BF16)

    def channel_map(v, half):
        return jnp.concatenate(
            [jnp.dot(v[:, p * 2 * fg:(p + 1) * 2 * fg], ab_ref[half, p], preferred_element_type=F32)
             for p in range(N_FOURIER_GROUPS // 2)], axis=1)

    def fold_and_channel_map():
        js = js_ref[...]
        for i in range(nb):
            blk = slice(i * fb, (i + 1) * fb)
            lo = u_ref[0, blk, :].astype(F32)
            src = n - (i + 1) * fb
            r = jnp.dot(js, u_ref[0, src:src + fb, :], preferred_element_type=F32)
            if i > 0:
                r = jnp.where(first_row, u_ref[0, n - i * fb:n - i * fb + 16, :][0:1, :].astype(F32), r)
            ue_ref[blk, :] = (lo + r).astype(BF16)
            uo_ref[blk, :] = (lo - r).astype(BF16)
        for i in range(nb):
            blk = slice(i * fb, (i + 1) * fb)
            pe_ref[blk, :] = channel_map(ue_ref[blk, :], 0).astype(BF16)
            qo_ref[blk, :] = channel_map(uo_ref[blk, :], 1).astype(BF16)
        p_mid = channel_map(u_ref[0, h:h + 16, :], 0)[0:1, :] * inv_sqrt_n
        sign = (1 - 2 * (lax.broadcasted_iota(jnp.int32, (h, D_FOURIER), 0) & 1)).astype(F32)
        y_mid = jnp.sum(pe_ref[...].astype(F32) * sign, axis=0, keepdims=True) * inv_sqrt_n + p_mid
        mid_ref[0] = jnp.broadcast_to(p_mid, (SUBLANES, D_FOURIER))
        mid_ref[1] = jnp.broadcast_to(y_mid, (SUBLANES, D_FOURIER))

    def dft_block(i):
        blk = slice(i * mb, (i + 1) * mb)
        sign = (1 - 2 * (lax.broadcasted_iota(jnp.int32, (mb, D_FOURIER), 0) & 1)).astype(F32)
        yc = jnp.dot(c_ref[blk, :], pe_ref[...], preferred_element_type=F32) + sign * mid_ref[0, 0:1, :]
        ys = jnp.dot(s_ref[blk, :], qo_ref[...], preferred_element_type=F32)
        yf_ref[slot, blk, :] = ((yc - ys) * g_ref[0, blk, :].astype(F32)).astype(BF16)
        gs_ref[blk, :] = (yc + ys).astype(BF16)

    def reflect_upper_half():
        js = js_ref[...]
        for i in range(nb):
            src = h - (i + 1) * fb
            r = jnp.dot(js, gs_ref[src:src + fb, :], preferred_element_type=F32)
            edge = mid_ref[1, 0:1, :] if i == 0 else gs_ref[h - i * fb:h - i * fb + 16, :][0:1, :].astype(F32)
            r = jnp.where(first_row, edge, r)
            rows = slice(h + i * fb, h + (i + 1) * fb)
            yf_ref[slot, rows, :] = (r * g_ref[0, rows, :].astype(F32)).astype(BF16)

    phases = [fold_and_channel_map] + [functools.partial(dft_block, i) for i in range(h // mb)] + [reflect_upper_half]
    per_step = len(phases) // FUSED_STEPS
    assert per_step * FUSED_STEPS == len(phases)
    def fourier_part(k):
        for phase in phases[k * per_step:(k + 1) * per_step]:
            phase()

    def outproj_tile():
        tm = x_ref.shape[1]
        rb = OUTPROJ_ROW_BLOCK
        for r in range(tm // rb):
            rows = slice(r * rb, (r + 1) * rb)
            yf_rows = pl.ds(pl.multiple_of(t * tm, tm) + r * rb, rb)
            y = jnp.dot(yf_ref[1 - slot, yf_rows, :], wb_ref[0:D_FOURIER, :], preferred_element_type=F32)
            y += jnp.dot(yr_ref[0, rows, :], wb_ref[D_FOURIER:D_MIX, :], preferred_element_type=F32)
            ms = jnp.mean(y * y, axis=-1, keepdims=True)
            o_ref[0, rows, :] = x_ref[0, rows, :] + y * lax.rsqrt(ms + EPS) * gate_ref[0]

    interior = jnp.logical_and(b > 0, b < n_batch)
    for k in range(FUSED_STEPS):
        def both(k=k):
            fourier_part(k)
            outproj_tile()
        pl.when(jnp.logical_and(interior, t == k))(both)
        pl.when(jnp.logical_and(b == 0, t == k))(functools.partial(fourier_part, k))
    pl.when(b == n_batch)(outproj_tile)


def _fourier_outproj(proj, yr, ab, c_mat, s_mat, js, w_out, x, mod):
    bsz, length, _ = x.shape
    h = length // 2
    assert h % FOLD_BLOCK == 0 and h % FOURIER_ROW_BLOCK == 0 and length % FUSED_STEPS == 0
    tm = length // FUSED_STEPS
    assert tm % OUTPROJ_ROW_BLOCK == 0

    def cur(col):
        return lambda b, t: (jnp.minimum(b, bsz - 1), 0, col)

    def prev_tile(b, t):
        return (jnp.maximum(b - 1, 0), jnp.where(b == 0, 0, t), 0)

    const2 = lambda b, t: (0, 0)
    resident = pl.Buffered(1)
    return pl.pallas_call(
        _fourier_outproj_kernel,
        grid=(bsz + 1, FUSED_STEPS),
        in_specs=[
            pl.BlockSpec((1, length, D_FOURIER), cur(COL_F_IN // D_FOURIER)),
            pl.BlockSpec((1, length, D_FOURIER), cur(COL_F_GATE // D_FOURIER)),
            pl.BlockSpec((2, N_FOURIER_GROUPS // 2, 2 * FOURIER_GROUP, 2 * FOURIER_GROUP), lambda b, t: (0, 0, 0, 0),
                         pipeline_mode=resident),
            pl.BlockSpec((h, h), const2, pipeline_mode=resident),
            pl.BlockSpec((h, h), const2, pipeline_mode=resident),
            pl.BlockSpec((FOLD_BLOCK, FOLD_BLOCK), const2, pipeline_mode=resident),
            pl.BlockSpec((1, tm, D_RET), prev_tile),
            pl.BlockSpec((D_MIX, D_MODEL), const2, pipeline_mode=resident),
            pl.BlockSpec((1, tm, D_MODEL), prev_tile),
            pl.BlockSpec((1, 1, D_MODEL), lambda b, t: (jnp.maximum(b - 1, 0), 0, MOD_GATE)),
        ],
        out_specs=pl.BlockSpec((1, tm, D_MODEL), prev_tile),
        out_shape=jax.ShapeDtypeStruct((bsz, length, D_MODEL), F32),
        scratch_shapes=[
            pltpu.VMEM((h, D_FOURIER), BF16),
            pltpu.VMEM((h, D_FOURIER), BF16),
            pltpu.VMEM((h, D_FOURIER), BF16),
            pltpu.VMEM((h, D_FOURIER), BF16),
            pltpu.VMEM((h, D_FOURIER), BF16),
            pltpu.VMEM((2, SUBLANES, D_FOURIER), F32),
            pltpu.VMEM((2, length, D_FOURIER), BF16),
            pltpu.VMEM((D_MIX, D_MODEL), BF16),
        ],
        compiler_params=pltpu.CompilerParams(
            dimension_semantics=("arbitrary", "arbitrary"), vmem_limit_bytes=VMEM_LIMIT),
        name="fourier_outproj",
    )(proj, proj, ab, c_mat, s_mat, js, yr, w_out, x, mod)


def _seq_dft_matrices(length):
    h = length // 2
    idx = np.arange(h)
    ang = 2.0 * np.pi * ((idx[:, None] * idx[None, :]) % length) / length
    scale = 1.0 / np.sqrt(length)
    c_mat = jnp.asarray((np.cos(ang) * scale).astype(np.float32)).astype(BF16)
    s_mat = jnp.asarray((np.sin(ang) * scale).astype(np.float32)).astype(BF16)
    r = np.arange(1, FOLD_BLOCK)
    js = np.zeros((FOLD_BLOCK, FOLD_BLOCK), np.float32)
    js[r, FOLD_BLOCK - r] = 1.0
    return c_mat, s_mat, jnp.asarray(js).astype(BF16)


def _ret_kernel(q_ref, k_ref, v_ref, rg_ref, kc_ref, vc_ref, dm_ref, rd_ref, cd_ref, gain_ref,
                o_ref, u_ref, st_ref, a_ref):
    c = RET_CHUNK
    hd = RET_HEAD_DIM
    n_chunks = q_ref.shape[1] // c
    n_ctx = kc_ref.shape[1] // c

    for h in range(N_RET_HEADS):
        cols = slice(h * hd, (h + 1) * hd)
        dm = dm_ref[h]
        q_dec = rd_ref[h, 0]
        k_dec = rd_ref[h, 1]
        c_dec_f = cd_ref[h, 0:1, 0:hd]
        c_dec_b = cd_ref[h, 0:1, hd:2 * hd]
        gain = gain_ref[:, cols]

        def both_decays(t, dec):
            return jnp.concatenate([t * dec[:, 0:hd], t * dec[:, hd:2 * hd]], axis=1)

        def kv_outer(k, v):
            return lax.dot_general(v, both_decays(k, k_dec), (((0,), (0,)), ((), ())), preferred_element_type=F32)

        for j in range(n_ctx):
            rows = slice(j * c, (j + 1) * c)
            u_ref[j] = kv_outer(kc_ref[0, rows, cols], vc_ref[0, rows, cols])
        for j in range(n_chunks):
            rows = slice(j * c, (j + 1) * c)
            u_ref[n_ctx + j] = kv_outer(k_ref[0, rows, cols], v_ref[0, rows, cols])

        sf = jnp.zeros((hd, hd), F32)
        for j in range(n_ctx + n_chunks - 1):
            sf = c_dec_f * sf + u_ref[j, :, 0:hd]
            if j + 1 >= n_ctx:
                st_ref[j + 1 - n_ctx, :, 0:hd] = sf.astype(BF16)
        sb = jnp.zeros((hd, hd), F32)
        for j in reversed(range(n_ctx)):
            sb = c_dec_b * sb + u_ref[j, :, hd:2 * hd]
        for j in reversed(range(n_chunks)):
            st_ref[j, :, hd:2 * hd] = sb.astype(BF16)
            if j > 0:
                sb = c_dec_b * sb + u_ref[n_ctx + j, :, hd:2 * hd]

        for j in range(n_chunks):
            rows = slice(j * c, (j + 1) * c)
            s = lax.dot_general(q_ref[0, rows, cols], k_ref[0, rows, cols], (((1,), (1,)), ((), ())),
                                preferred_element_type=F32)
            a_ref[j] = (s * dm).astype(BF16)

        for j in range(n_chunks):
            rows = slice(j * c, (j + 1) * c)
            o = jnp.dot(a_ref[j], v_ref[0, rows, cols], preferred_element_type=F32)
            o += lax.dot_general(both_decays(q_ref[0, rows, cols], q_dec), st_ref[j], (((1,), (1,)), ((), ())),
                                 preferred_element_type=F32)
            mu = jnp.mean(o, axis=-1, keepdims=True)
            d = o - mu
            var = jnp.mean(d * d, axis=-1, keepdims=True)
            on = d * lax.rsqrt(var + EPS)
            o_ref[0, rows, cols] = (on * gain * rg_ref[0, rows, cols].astype(F32)).astype(BF16)


def _retention(proj, proj_c, dm, rd, cd, gain):
    bsz, length, _ = proj.shape
    ctx_len = proj_c.shape[1]
    hd = RET_HEAD_DIM
    c = RET_CHUNK
    n_chunks = length // c
    n_ctx = ctx_len // c
    assert n_ctx >= 1

    def col(base):
        return lambda b: (b, 0, base // D_RET)

    return pl.pallas_call(
        _ret_kernel,
        grid=(bsz,),
        in_specs=[
            pl.BlockSpec((1, length, D_RET), col(COL_Q)),
            pl.BlockSpec((1, length, D_RET), col(COL_K)),
            pl.BlockSpec((1, length, D_RET), col(COL_V)),
            pl.BlockSpec((1, length, D_RET), col(COL_R_GATE)),
            pl.BlockSpec((1, ctx_len, D_RET), col(0)),
            pl.BlockSpec((1, ctx_len, D_RET), col(D_RET)),
            pl.BlockSpec((N_RET_HEADS, c, c), lambda b: (0, 0, 0)),
            pl.BlockSpec((N_RET_HEADS, 2, c, 2 * hd), lambda b: (0, 0, 0, 0)),
            pl.BlockSpec((N_RET_HEADS, SUBLANES, 2 * hd), lambda b: (0, 0, 0)),
            pl.BlockSpec((1, D_RET), lambda b: (0, 0)),
        ],
        out_specs=pl.BlockSpec((1, length, D_RET), lambda b: (b, 0, 0)),
        out_shape=jax.ShapeDtypeStruct((bsz, length, D_RET), BF16),
        scratch_shapes=[
            pltpu.VMEM((n_ctx + n_chunks, hd, 2 * hd), F32),
            pltpu.VMEM((n_chunks, hd, 2 * hd), BF16),
            pltpu.VMEM((n_chunks, c, c), BF16),
        ],
        compiler_params=pltpu.CompilerParams(
            dimension_semantics=("parallel",), vmem_limit_bytes=VMEM_LIMIT),
        name="retention",
    )(proj, proj, proj, proj, proj_c, proj_c, dm, rd, cd, gain)


def kernel(x, c, ctx, c_ctx, w_ada, b_ada, g_pre, g_post, w_in, w_fourier, decay_logit, ret_gn_gain, w_out):
    bsz, seq_len, _ = x.shape
    ctx_len = ctx.shape[1]
    assert w_ada.shape[0] == 1, "single layer only"

    mod = _adaln(c, c_ctx, w_ada[0], b_ada[0], g_pre[0], g_post[0])

    dm, rd, cd, ab = _tables(decay_logit[0], w_fourier[0])

    cos_t, sin_t = _rotary_tables(seq_len)

    gate_cols = list(range(COL_F_GATE, COL_Q, RET_HEAD_DIM)) + list(range(COL_R_GATE, D_IN, RET_HEAD_DIM))
    proj = _inproj(x, mod, w_in[0], cos_t, sin_t, mod_row=None, col_block=D_IN, blocks=(0,),
                   rot_lo=COL_Q, rot_hi=COL_V, silu_cols=gate_cols, tm=TOKEN_TILE)
    ctx_rows = bsz * ctx_len
    assert COL_V == COL_K + D_RET and COL_K % D_RET == 0
    proj_c = _inproj(ctx.reshape(1, ctx_rows, D_MODEL), mod, w_in[0], cos_t, sin_t, mod_row=bsz,
                     col_block=D_RET, blocks=(COL_K // D_RET, COL_V // D_RET),
                     rot_lo=0, rot_hi=0, silu_cols=(), tm=min(TOKEN_TILE, ctx_rows))
    proj_c = proj_c.reshape(bsz, ctx_len, 2 * D_RET)

    yr = _retention(proj, proj_c, dm, rd, cd, ret_gn_gain[0].reshape(1, D_RET))
    return _fourier_outproj(proj, yr, ab, *_seq_dft_matrices(seq_len), w_out[0], x, mod)
```

```python
import functools

import numpy as np
import jax
import jax.numpy as jnp
from jax import lax
from jax.experimental import pallas as pl
from jax.experimental.pallas import tpu as pltpu

D_MODEL = 1024
GRID_W = 64
D_FOURIER = 512
N_FOURIER_GROUPS = 4
FOURIER_GROUP = D_FOURIER // N_FOURIER_GROUPS
D_RET = 512
N_RET_HEADS = 4
RET_HEAD_DIM = D_RET // N_RET_HEADS
D_MIX = D_FOURIER + D_RET
D_IN = 2 * D_FOURIER + 4 * D_RET
RET_CHUNK = 128
ROPE_BASE = 10000.0
QK_SCALE = RET_HEAD_DIM ** -0.5
EPS = 1e-6

COL_F_IN = 0
COL_F_GATE = D_FOURIER
COL_Q = 2 * D_FOURIER
COL_K = COL_Q + D_RET
COL_V = COL_K + D_RET
COL_R_GATE = COL_V + D_RET

MXU_COLS = 256
FOLD_BLOCK = 256
FOURIER_ROW_BLOCK = 512
INPROJ_COL_CHUNK = 2 * MXU_COLS
OUTPROJ_ROW_BLOCK = 256
FUSED_STEPS = 2
SUBLANES = 8
MOD_SHIFT, MOD_MULT, MOD_GATE = 0, 1, 2
TOKEN_TILE = 1024
VMEM_LIMIT = 56 * 1024 * 1024

F32 = jnp.float32
BF16 = jnp.bfloat16


def _silu(v):
    return v * jax.nn.sigmoid(v)


def _adaln_kernel(c_ref, cc_ref, w_ref, b_ref, gpre_ref, gpost_ref, o_ref, s_ref):
    j = pl.program_id(0)
    nb = c_ref.shape[0]
    s_ref[0:nb, :] = _silu(c_ref[...])
    s_ref[nb:, :] = jnp.broadcast_to(_silu(cc_ref[...]), (s_ref.shape[0] - nb, D_MODEL))
    m = jnp.dot(s_ref[...], w_ref[...], preferred_element_type=F32) + b_ref[...]
    add = jnp.where(j == 1, 1.0, 0.0)
    mul = jnp.where(j == 1, gpre_ref[...], jnp.where(j == 2, gpost_ref[...], 1.0))
    o_ref[...] = ((m + add) * mul)[:, None, :]


def _adaln(c, c_ctx, w_ada, b_ada, g_pre, g_post):
    bsz = c.shape[0]
    n_out = w_ada.shape[1]
    bn = D_MODEL
    assert n_out == 3 * bn and bsz % SUBLANES == 0
    rows = bsz + SUBLANES
    const = lambda j: (0, 0)
    return pl.pallas_call(
        _adaln_kernel,
        grid=(n_out // bn,),
        in_specs=[
            pl.BlockSpec((bsz, D_MODEL), const),
            pl.BlockSpec((1, D_MODEL), const),
            pl.BlockSpec((D_MODEL, bn), lambda j: (0, j)),
            pl.BlockSpec((1, bn), lambda j: (0, j)),
            pl.BlockSpec((1, D_MODEL), const),
            pl.BlockSpec((1, D_MODEL), const),
        ],
        out_specs=pl.BlockSpec((rows, 1, bn), lambda j: (0, 0, j)),
        out_shape=jax.ShapeDtypeStruct((rows, 1, n_out), F32),
        scratch_shapes=[pltpu.VMEM((rows, D_MODEL), F32)],
        name="adaln",
    )(c, c_ctx.reshape(1, D_MODEL), w_ada, b_ada.reshape(1, n_out), g_pre.reshape(1, D_MODEL),
      g_post.reshape(1, D_MODEL))


def _tables_kernel(dl_ref, wf_ref, cc_ref, sc_ref, dm_ref, rd_ref, cd_ref, ab_ref):
    c = RET_CHUNK
    n = lax.broadcasted_iota(jnp.int32, (c, c), 0).astype(F32)
    m = lax.broadcasted_iota(jnp.int32, (c, c), 1).astype(F32)
    diff = n - m

    def log_sigmoid(v):
        return jnp.minimum(v, 0.0) - jnp.log1p(jnp.exp(-jnp.abs(v)))

    for h in range(N_RET_HEADS):
        lg_f = log_sigmoid(jnp.full((c, c), dl_ref[0, h], F32))
        lg_b = log_sigmoid(jnp.full((c, c), dl_ref[1, h], F32))
        dm_ref[h] = QK_SCALE * jnp.where(diff >= 0, jnp.exp(lg_f * jnp.maximum(diff, 0.0)),
                                         jnp.exp(lg_b * jnp.maximum(-diff, 0.0)))
        rd_ref[h, 0, :, 0:c] = (QK_SCALE * jnp.exp(lg_f * (n + 1.0))).astype(BF16)
        rd_ref[h, 0, :, c:2 * c] = (QK_SCALE * jnp.exp(lg_b * (c - n))).astype(BF16)
        rd_ref[h, 1, :, 0:c] = jnp.exp(lg_f * (c - 1.0 - n)).astype(BF16)
        rd_ref[h, 1, :, c:2 * c] = jnp.exp(lg_b * n).astype(BF16)
        cd_ref[h, :, 0:c] = jnp.exp(lg_f[0:SUBLANES, :] * float(c))
        cd_ref[h, :, c:2 * c] = jnp.exp(lg_b[0:SUBLANES, :] * float(c))
    fg = FOURIER_GROUP
    ab_ref[...] = jnp.zeros(ab_ref.shape, BF16)
    for g in range(N_FOURIER_GROUPS):
        wf = wf_ref[g]
        a = jnp.dot(cc_ref[...], wf, preferred_element_type=F32, precision=lax.Precision.HIGHEST)
        b = jnp.dot(sc_ref[...], wf, preferred_element_type=F32, precision=lax.Precision.HIGHEST)
        pair, off = g // 2, (g % 2) * fg
        ab_ref[0, pair, off:off + fg, off:off + fg] = a.astype(BF16)
        ab_ref[1, pair, off:off + fg, off:off + fg] = b.astype(BF16)


def _tables(decay_logit, w_fourier):
    c = RET_CHUNK
    idx = np.arange(FOURIER_GROUP)
    ang = 2.0 * np.pi * ((idx[:, None] * idx[None, :]) % FOURIER_GROUP) / FOURIER_GROUP
    cc = jnp.asarray(np.cos(ang) / np.sqrt(FOURIER_GROUP), F32)
    sc = jnp.asarray(np.sin(ang) / np.sqrt(FOURIER_GROUP), F32)
    return pl.pallas_call(
        _tables_kernel,
        out_shape=(
            jax.ShapeDtypeStruct((N_RET_HEADS, c, c), F32),
            jax.ShapeDtypeStruct((N_RET_HEADS, 2, c, 2 * c), BF16),
            jax.ShapeDtypeStruct((N_RET_HEADS, SUBLANES, 2 * c), F32),
            jax.ShapeDtypeStruct((2, N_FOURIER_GROUPS // 2, 2 * FOURIER_GROUP, 2 * FOURIER_GROUP), BF16),
        ),
        in_specs=[pl.BlockSpec(memory_space=pltpu.SMEM)] + [pl.BlockSpec(memory_space=pltpu.VMEM)] * 3,
        name="tables",
    )(decay_logit, w_fourier, cc, sc)


def _inproj_kernel(x_ref, shift_ref, mult_ref, cos_ref, sin_ref, *refs, n_w, rot_lo, rot_hi, silu_cols):
    w_refs, o_ref, wb_ref = refs[:n_w], refs[n_w], refs[n_w + 1]
    n_out = wb_ref.shape[1]
    hd = RET_HEAD_DIM
    bn = INPROJ_COL_CHUNK

    @pl.when(jnp.logical_and(pl.program_id(0) == 0, pl.program_id(1) == 0))
    def _():
        for i, w_ref in enumerate(w_refs):
            width = w_ref.shape[1]
            for j in range(width // bn):
                wb_ref[:, i * width + j * bn:i * width + (j + 1) * bn] = w_ref[:, j * bn:(j + 1) * bn].astype(BF16)

    x = x_ref[0]
    ms = jnp.mean(x * x, axis=-1, keepdims=True)
    hb = (x * lax.rsqrt(ms + EPS) * mult_ref[0] + shift_ref[0]).astype(BF16)
    is_u1 = (lax.broadcasted_iota(jnp.int32, (x.shape[0], hd), 1) & (hd // 4)) == 0

    def epilogue_weight(j):
        lo = j * bn
        return (lo in silu_cols) + (rot_lo <= lo < rot_hi)

    for j in sorted(range(n_out // bn), key=epilogue_weight, reverse=True):
        p = jnp.dot(hb, wb_ref[:, j * bn:(j + 1) * bn], preferred_element_type=F32)
        for i in range(bn // hd):
            ph = p[:, i * hd:(i + 1) * hd]
            lo = j * bn + i * hd
            if lo in silu_cols:
                ph = _silu(ph)
            if rot_lo <= lo < rot_hi:
                partner = jnp.where(is_u1, pltpu.roll(ph, hd - hd // 4, axis=1), pltpu.roll(ph, hd // 4, axis=1))
                ph = ph * cos_ref[...] + partner * sin_ref[...]
            o_ref[0, :, lo:lo + hd] = ph.astype(BF16)


def _inproj(x, mod, w, cos_t, sin_t, *, mod_row, col_block, blocks, rot_lo, rot_hi, silu_cols, tm):
    bsz, length, _ = x.shape
    n_out = len(blocks) * col_block
    def mod_map(k):
        return (lambda b, t: (b, 0, k)) if mod_row is None else (lambda b, t: (mod_row, 0, k))
    rot_map = (lambda b, t: (t, 0)) if rot_hi > rot_lo else (lambda b, t: (0, 0))
    w_specs = [pl.BlockSpec((D_MODEL, col_block), functools.partial(lambda b, t, blk: (0, blk), blk=blk),
                            pipeline_mode=pl.Buffered(1)) for blk in blocks]
    return pl.pallas_call(
        functools.partial(_inproj_kernel, n_w=len(blocks), rot_lo=rot_lo, rot_hi=rot_hi,
                          silu_cols=frozenset(silu_cols)),
        grid=(bsz, length // tm),
        in_specs=[
            pl.BlockSpec((1, tm, D_MODEL), lambda b, t: (b, t, 0)),
            pl.BlockSpec((1, 1, D_MODEL), mod_map(MOD_SHIFT)),
            pl.BlockSpec((1, 1, D_MODEL), mod_map(MOD_MULT)),
            pl.BlockSpec((tm, RET_HEAD_DIM), rot_map),
            pl.BlockSpec((tm, RET_HEAD_DIM), rot_map),
        ] + w_specs,
        out_specs=pl.BlockSpec((1, tm, n_out), lambda b, t: (b, t, 0)),
        out_shape=jax.ShapeDtypeStruct((bsz, length, n_out), BF16),
        scratch_shapes=[pltpu.VMEM((D_MODEL, n_out), BF16)],
        compiler_params=pltpu.CompilerParams(
            dimension_semantics=("arbitrary", "arbitrary"), vmem_limit_bytes=VMEM_LIMIT),
        name="inproj",
    )(x, mod, mod, cos_t, sin_t, *([w] * len(blocks)))


def _rotary_tables(seq_len):
    n_freq = RET_HEAD_DIM // 4
    pos = np.arange(seq_len)
    row = (pos // GRID_W).astype(np.float64)
    col = (pos % GRID_W).astype(np.float64)
    inv_freq = ROPE_BASE ** (-np.arange(n_freq, dtype=np.float64) / n_freq)
    ang_r = row[:, None] * inv_freq[None, :]
    ang_c = col[:, None] * inv_freq[None, :]
    cos_t = np.concatenate([np.cos(ang_r), np.cos(ang_r), np.cos(ang_c), np.cos(ang_c)], axis=1)
    sin_t = np.concatenate([-np.sin(ang_r), np.sin(ang_r), -np.sin(ang_c), np.sin(ang_c)], axis=1)
    return jnp.asarray(cos_t, F32), jnp.asarray(sin_t, F32)


def _fourier_outproj_kernel(u_ref, g_ref, ab_ref, c_ref, s_ref, js_ref, yr_ref, w_ref, x_ref, gate_ref,
                            o_ref, ue_ref, uo_ref, pe_ref, qo_ref, gs_ref, mid_ref, yf_ref, wb_ref):
    b = pl.program_id(0)
    t = pl.program_id(1)
    n_batch = pl.num_programs(0) - 1
    slot = b % 2
    n = u_ref.shape[1]
    h = n // 2
    fb = FOLD_BLOCK
    nb = h // fb
    mb = FOURIER_ROW_BLOCK
    fg = FOURIER_GROUP
    inv_sqrt_n = 1.0 / np.sqrt(n)
    first_row = lax.broadcasted_iota(jnp.int32, (fb, D_FOURIER), 0) == 0

    @pl.when(jnp.logical_and(b == 0, t == 0))
    def _():
        wb_ref[...] = w_ref[...].astype(BF16)

    def channel_map(v, half):
        return jnp.concatenate(
            [jnp.dot(v[:, p * 2 * fg:(p + 1) * 2 * fg], ab_ref[half, p], preferred_element_type=F32)
             for p in range(N_FOURIER_GROUPS // 2)], axis=1)

    def fold_and_channel_map():
        js = js_ref[...]
        for i in range(nb):
            blk = slice(i * fb, (i + 1) * fb)
            lo = u_ref[0, blk, :].astype(F32)
            src = n - (i + 1) * fb
            r = jnp.dot(js, u_ref[0, src:src + fb, :], preferred_element_type=F32)
            if i > 0:
                r = jnp.where(first_row, u_ref[0, n - i * fb:n - i * fb + 16, :][0:1, :].astype(F32), r)
            ue_ref[blk, :] = (lo + r).astype(BF16)
            uo_ref[blk, :] = (lo - r).astype(BF16)
        for i in range(nb):
            blk = slice(i * fb, (i + 1) * fb)
            pe_ref[blk, :] = channel_map(ue_ref[blk, :], 0).astype(BF16)
            qo_ref[blk, :] = channel_map(uo_ref[blk, :], 1).astype(BF16)
        p_mid = channel_map(u_ref[0, h:h + 16, :], 0)[0:1, :] * inv_sqrt_n
        sign = (1 - 2 * (lax.broadcasted_iota(jnp.int32, (h, D_FOURIER), 0) & 1)).astype(F32)
        y_mid = jnp.sum(pe_ref[...].astype(F32) * sign, axis=0, keepdims=True) * inv_sqrt_n + p_mid
        mid_ref[0] = jnp.broadcast_to(p_mid, (SUBLANES, D_FOURIER))
        mid_ref[1] = jnp.broadcast_to(y_mid, (SUBLANES, D_FOURIER))

    def dft_block(i):
        blk = slice(i * mb, (i + 1) * mb)
        sign = (1 - 2 * (lax.broadcasted_iota(jnp.int32, (mb, D_FOURIER), 0) & 1)).astype(F32)
        yc = jnp.dot(c_ref[blk, :], pe_ref[...], preferred_element_type=F32) + sign * mid_ref[0, 0:1, :]
        ys = jnp.dot(s_ref[blk, :], qo_ref[...], preferred_element_type=F32)
        yf_ref[slot, blk, :] = ((yc - ys) * g_ref[0, blk, :].astype(F32)).astype(BF16)
        gs_ref[blk, :] = (yc + ys).astype(BF16)

    def reflect_upper_half():
        js = js_ref[...]
        for i in range(nb):
            src = h - (i + 1) * fb
            r = jnp.dot(js, gs_ref[src:src + fb, :], preferred_element_type=F32)
            edge = mid_ref[1, 0:1, :] if i == 0 else gs_ref[h - i * fb:h - i * fb + 16, :][0:1, :].astype(F32)
            r = jnp.where(first_row, edge, r)
            rows = slice(h + i * fb, h + (i + 1) * fb)
            yf_ref[slot, rows, :] = (r * g_ref[0, rows, :].astype(F32)).astype(BF16)

    phases = [fold_and_channel_map] + [functools.partial(dft_block, i) for i in range(h // mb)] + [reflect_upper_half]
    per_step = len(phases) // FUSED_STEPS
    assert per_step * FUSED_STEPS == len(phases)
    def fourier_part(k):
        for phase in phases[k * per_step:(k + 1) * per_step]:
            phase()

    def outproj_tile():
        tm = x_ref.shape[1]
        rb = OUTPROJ_ROW_BLOCK
        for r in range(tm // rb):
            rows = slice(r * rb, (r + 1) * rb)
            yf_rows = pl.ds(pl.multiple_of(t * tm, tm) + r * rb, rb)
            y = jnp.dot(yf_ref[1 - slot, yf_rows, :], wb_ref[0:D_FOURIER, :], preferred_element_type=F32)
            y += jnp.dot(yr_ref[0, rows, :], wb_ref[D_FOURIER:D_MIX, :], preferred_element_type=F32)
            ms = jnp.mean(y * y, axis=-1, keepdims=True)
            o_ref[0, rows, :] = x_ref[0, rows, :] + y * lax.rsqrt(ms + EPS) * gate_ref[0]

    interior = jnp.logical_and(b > 0, b < n_batch)
    for k in range(FUSED_STEPS):
        def both(k=k):
            fourier_part(k)
            outproj_tile()
        pl.when(jnp.logical_and(interior, t == k))(both)
        pl.when(jnp.logical_and(b == 0, t == k))(functools.partial(fourier_part, k))
    pl.when(b == n_batch)(outproj_tile)


def _fourier_outproj(proj, yr, ab, c_mat, s_mat, js, w_out, x, mod):
    bsz, length, _ = x.shape
    h = length // 2
    assert h % FOLD_BLOCK == 0 and h % FOURIER_ROW_BLOCK == 0 and length % FUSED_STEPS == 0
    tm = length // FUSED_STEPS
    assert tm % OUTPROJ_ROW_BLOCK == 0

    def cur(col):
        return lambda b, t: (jnp.minimum(b, bsz - 1), 0, col)

    def prev_tile(b, t):
        return (jnp.maximum(b - 1, 0), jnp.where(b == 0, 0, t), 0)

    const2 = lambda b, t: (0, 0)
    resident = pl.Buffered(1)
    return pl.pallas_call(
        _fourier_outproj_kernel,
        grid=(bsz + 1, FUSED_STEPS),
        in_specs=[
            pl.BlockSpec((1, length, D_FOURIER), cur(COL_F_IN // D_FOURIER)),
            pl.BlockSpec((1, length, D_FOURIER), cur(COL_F_GATE // D_FOURIER)),
            pl.BlockSpec((2, N_FOURIER_GROUPS // 2, 2 * FOURIER_GROUP, 2 * FOURIER_GROUP), lambda b, t: (0, 0, 0, 0),
                         pipeline_mode=resident),
            pl.BlockSpec((h, h), const2, pipeline_mode=resident),
            pl.BlockSpec((h, h), const2, pipeline_mode=resident),
            pl.BlockSpec((FOLD_BLOCK, FOLD_BLOCK), const2, pipeline_mode=resident),
            pl.BlockSpec((1, tm, D_RET), prev_tile),
            pl.BlockSpec((D_MIX, D_MODEL), const2, pipeline_mode=resident),
            pl.BlockSpec((1, tm, D_MODEL), prev_tile),
            pl.BlockSpec((1, 1, D_MODEL), lambda b, t: (jnp.maximum(b - 1, 0), 0, MOD_GATE)),
        ],
        out_specs=pl.BlockSpec((1, tm, D_MODEL), prev_tile),
        out_shape=jax.ShapeDtypeStruct((bsz, length, D_MODEL), F32),
        scratch_shapes=[
            pltpu.VMEM((h, D_FOURIER), BF16),
            pltpu.VMEM((h, D_FOURIER), BF16),
            pltpu.VMEM((h, D_FOURIER), BF16),
            pltpu.VMEM((h, D_FOURIER), BF16),
            pltpu.VMEM((h, D_FOURIER), BF16),
            pltpu.VMEM((2, SUBLANES, D_FOURIER), F32),
            pltpu.VMEM((2, length, D_FOURIER), BF16),
            pltpu.VMEM((D_MIX, D_MODEL), BF16),
        ],
        compiler_params=pltpu.CompilerParams(
            dimension_semantics=("arbitrary", "arbitrary"), vmem_limit_bytes=VMEM_LIMIT),
        name="fourier_outproj",
    )(proj, proj, ab, c_mat, s_mat, js, yr, w_out, x, mod)


def _seq_dft_matrices(length):
    h = length // 2
    idx = np.arange(h)
    ang = 2.0 * np.pi * ((idx[:, None] * idx[None, :]) % length) / length
    scale = 1.0 / np.sqrt(length)
    c_mat = jnp.asarray((np.cos(ang) * scale).astype(np.float32)).astype(BF16)
    s_mat = jnp.asarray((np.sin(ang) * scale).astype(np.float32)).astype(BF16)
    r = np.arange(1, FOLD_BLOCK)
    js = np.zeros((FOLD_BLOCK, FOLD_BLOCK), np.float32)
    js[r, FOLD_BLOCK - r] = 1.0
    return c_mat, s_mat, jnp.asarray(js).astype(BF16)


def _ret_kernel(q_ref, k_ref, v_ref, rg_ref, kc_ref, vc_ref, dm_ref, rd_ref, cd_ref, gain_ref,
                o_ref, u_ref, st_ref, a_ref):
    c = RET_CHUNK
    hd = RET_HEAD_DIM
    n_chunks = q_ref.shape[1] // c
    n_ctx = kc_ref.shape[1] // c

    for h in range(N_RET_HEADS):
        cols = slice(h * hd, (h + 1) * hd)
        dm = dm_ref[h]
        q_dec = rd_ref[h, 0]
        k_dec = rd_ref[h, 1]
        c_dec_f = cd_ref[h, 0:1, 0:hd]
        c_dec_b = cd_ref[h, 0:1, hd:2 * hd]
        gain = gain_ref[:, cols]

        def both_decays(t, dec):
            return jnp.concatenate([t * dec[:, 0:hd], t * dec[:, hd:2 * hd]], axis=1)

        def kv_outer(k, v):
            return lax.dot_general(v, both_decays(k, k_dec), (((0,), (0,)), ((), ())), preferred_element_type=F32)

        for j in range(n_ctx):
            rows = slice(j * c, (j + 1) * c)
            u_ref[j] = kv_outer(kc_ref[0, rows, cols], vc_ref[0, rows, cols])
        for j in range(n_chunks):
            rows = slice(j * c, (j + 1) * c)
            u_ref[n_ctx + j] = kv_outer(k_ref[0, rows, cols], v_ref[0, rows, cols])

        sf = jnp.zeros((hd, hd), F32)
        for j in range(n_ctx + n_chunks - 1):
            sf = c_dec_f * sf + u_ref[j, :, 0:hd]
            if j + 1 >= n_ctx:
                st_ref[j + 1 - n_ctx, :, 0:hd] = sf.astype(BF16)
        sb = jnp.zeros((hd, hd), F32)
        for j in reversed(range(n_ctx)):
            sb = c_dec_b * sb + u_ref[j, :, hd:2 * hd]
        for j in reversed(range(n_chunks)):
            st_ref[j, :, hd:2 * hd] = sb.astype(BF16)
            if j > 0:
                sb = c_dec_b * sb + u_ref[n_ctx + j, :, hd:2 * hd]

        for j in range(n_chunks):
            rows = slice(j * c, (j + 1) * c)
            s = lax.dot_general(q_ref[0, rows, cols], k_ref[0, rows, cols], (((1,), (1,)), ((), ())),
                                preferred_element_type=F32)
            a_ref[j] = (s * dm).astype(BF16)

        for j in range(n_chunks):
            rows = slice(j * c, (j + 1) * c)
            o = jnp.dot(a_ref[j], v_ref[0, rows, cols], preferred_element_type=F32)
            o += lax.dot_general(both_decays(q_ref[0, rows, cols], q_dec), st_ref[j], (((1,), (1,)), ((), ())),
                                 preferred_element_type=F32)
            mu = jnp.mean(o, axis=-1, keepdims=True)
            d = o - mu
            var = jnp.mean(d * d, axis=-1, keepdims=True)
            on = d * lax.rsqrt(var + EPS)
            o_ref[0, rows, cols] = (on * gain * rg_ref[0, rows, cols].astype(F32)).astype(BF16)


def _retention(proj, proj_c, dm, rd, cd, gain):
    bsz, length, _ = proj.shape
    ctx_len = proj_c.shape[1]
    hd = RET_HEAD_DIM
    c = RET_CHUNK
    n_chunks = length // c
    n_ctx = ctx_len // c
    assert n_ctx >= 1

    def col(base):
        return lambda b: (b, 0, base // D_RET)

    return pl.pallas_call(
        _ret_kernel,
        grid=(bsz,),
        in_specs=[
            pl.BlockSpec((1, length, D_RET), col(COL_Q)),
            pl.BlockSpec((1, length, D_RET), col(COL_K)),
            pl.BlockSpec((1, length, D_RET), col(COL_V)),
            pl.BlockSpec((1, length, D_RET), col(COL_R_GATE)),
            pl.BlockSpec((1, ctx_len, D_RET), col(0)),
            pl.BlockSpec((1, ctx_len, D_RET), col(D_RET)),
            pl.BlockSpec((N_RET_HEADS, c, c), lambda b: (0, 0, 0)),
            pl.BlockSpec((N_RET_HEADS, 2, c, 2 * hd), lambda b: (0, 0, 0, 0)),
            pl.BlockSpec((N_RET_HEADS, SUBLANES, 2 * hd), lambda b: (0, 0, 0)),
            pl.BlockSpec((1, D_RET), lambda b: (0, 0)),
        ],
        out_specs=pl.BlockSpec((1, length, D_RET), lambda b: (b, 0, 0)),
        out_shape=jax.ShapeDtypeStruct((bsz, length, D_RET), BF16),
        scratch_shapes=[
            pltpu.VMEM((n_ctx + n_chunks, hd, 2 * hd), F32),
            pltpu.VMEM((n_chunks, hd, 2 * hd), BF16),
            pltpu.VMEM((n_chunks, c, c), BF16),
        ],
        compiler_params=pltpu.CompilerParams(
            dimension_semantics=("parallel",), vmem_limit_bytes=VMEM_LIMIT),
        name="retention",
    )(proj, proj, proj, proj, proj_c, proj_c, dm, rd, cd, gain)


def kernel(x, c, ctx, c_ctx, w_ada, b_ada, g_pre, g_post, w_in, w_fourier, decay_logit, ret_gn_gain, w_out):
    bsz, seq_len, _ = x.shape
    ctx_len = ctx.shape[1]
    assert w_ada.shape[0] == 1, "single layer only"

    mod = _adaln(c, c_ctx, w_ada[0], b_ada[0], g_pre[0], g_post[0])

    dm, rd, cd, ab = _tables(decay_logit[0], w_fourier[0])

    cos_t, sin_t = _rotary_tables(seq_len)

    gate_cols = list(range(COL_F_GATE, COL_Q, RET_HEAD_DIM)) + list(range(COL_R_GATE, D_IN, RET_HEAD_DIM))
    proj = _inproj(x, mod, w_in[0], cos_t, sin_t, mod_row=None, col_block=D_IN, blocks=(0,),
                   rot_lo=COL_Q, rot_hi=COL_V, silu_cols=gate_cols, tm=TOKEN_TILE)
    ctx_rows = bsz * ctx_len
    assert COL_V == COL_K + D_RET and COL_K % D_RET == 0
    proj_c = _inproj(ctx.reshape(1, ctx_rows, D_MODEL), mod, w_in[0], cos_t, sin_t, mod_row=bsz,
                     col_block=D_RET, blocks=(COL_K // D_RET, COL_V // D_RET),
                     rot_lo=0, rot_hi=0, silu_cols=(), tm=min(TOKEN_TILE, ctx_rows))
    proj_c = proj_c.reshape(bsz, ctx_len, 2 * D_RET)

    yr = _retention(proj, proj_c, dm, rd, cd, ret_gn_gain[0].reshape(1, D_RET))
    return _fourier_outproj(proj, yr, ab, *_seq_dft_matrices(seq_len), w_out[0], x, mod)
```

```python
import functools

import numpy as np
import jax
import jax.numpy as jnp
from jax import lax
from jax.experimental import pallas as pl
from jax.experimental.pallas import tpu as pltpu

D_MODEL = 1024
GRID_W = 64
D_FOURIER = 512
N_FOURIER_GROUPS = 4
FOURIER_GROUP = D_FOURIER // N_FOURIER_GROUPS
D_RET = 512
N_RET_HEADS = 4
RET_HEAD_DIM = D_RET // N_RET_HEADS
D_MIX = D_FOURIER + D_RET
D_IN = 2 * D_FOURIER + 4 * D_RET
RET_CHUNK = 128
ROPE_BASE = 10000.0
QK_SCALE = RET_HEAD_DIM ** -0.5
EPS = 1e-6

COL_F_IN = 0
COL_F_GATE = D_FOURIER
COL_Q = 2 * D_FOURIER
COL_K = COL_Q + D_RET
COL_V = COL_K + D_RET
COL_R_GATE = COL_V + D_RET

MXU_COLS = 256
FOLD_BLOCK = 256
FOURIER_ROW_BLOCK = 512
INPROJ_COL_CHUNK = 2 * MXU_COLS
OUTPROJ_ROW_BLOCK = 256
FUSED_STEPS = 2
SUBLANES = 8
MOD_SHIFT, MOD_MULT, MOD_GATE = 0, 1, 2
TOKEN_TILE = 1024
VMEM_LIMIT = 56 * 1024 * 1024

F32 = jnp.float32
BF16 = jnp.bfloat16


def _silu(v):
    return v * (0.5 * jnp.tanh(0.5 * v) + 0.5)


def _adaln_kernel(c_ref, cc_ref, w_ref, b_ref, gpre_ref, gpost_ref, o_ref, s_ref):
    j = pl.program_id(0)
    nb = c_ref.shape[0]
    s_ref[0:nb, :] = _silu(c_ref[...])
    s_ref[nb:, :] = jnp.broadcast_to(_silu(cc_ref[...]), (s_ref.shape[0] - nb, D_MODEL))
    m = jnp.dot(s_ref[...], w_ref[...], preferred_element_type=F32) + b_ref[...]
    add = jnp.where(j == 1, 1.0, 0.0)
    mul = jnp.where(j == 1, gpre_ref[...], jnp.where(j == 2, gpost_ref[...], 1.0))
    o_ref[...] = ((m + add) * mul)[:, None, :]


def _adaln(c, c_ctx, w_ada, b_ada, g_pre, g_post):
    bsz = c.shape[0]
    n_out = w_ada.shape[1]
    bn = D_MODEL
    assert n_out == 3 * bn and bsz % SUBLANES == 0
    rows = bsz + SUBLANES
    const = lambda j: (0, 0)
    return pl.pallas_call(
        _adaln_kernel,
        grid=(n_out // bn,),
        in_specs=[
            pl.BlockSpec((bsz, D_MODEL), const),
            pl.BlockSpec((1, D_MODEL), const),
            pl.BlockSpec((D_MODEL, bn), lambda j: (0, j)),
            pl.BlockSpec((1, bn), lambda j: (0, j)),
            pl.BlockSpec((1, D_MODEL), const),
            pl.BlockSpec((1, D_MODEL), const),
        ],
        out_specs=pl.BlockSpec((rows, 1, bn), lambda j: (0, 0, j)),
        out_shape=jax.ShapeDtypeStruct((rows, 1, n_out), F32),
        scratch_shapes=[pltpu.VMEM((rows, D_MODEL), F32)],
        name="adaln",
    )(c, c_ctx.reshape(1, D_MODEL), w_ada, b_ada.reshape(1, n_out), g_pre.reshape(1, D_MODEL),
      g_post.reshape(1, D_MODEL))


def _tables_kernel(dl_ref, wf_ref, cc_ref, sc_ref, dm_ref, rd_ref, cd_ref, ab_ref):
    c = RET_CHUNK
    n = lax.broadcasted_iota(jnp.int32, (c, c), 0).astype(F32)
    m = lax.broadcasted_iota(jnp.int32, (c, c), 1).astype(F32)
    diff = n - m

    def log_sigmoid(v):
        return jnp.minimum(v, 0.0) - jnp.log1p(jnp.exp(-jnp.abs(v)))

    for h in range(N_RET_HEADS):
        lg_f = log_sigmoid(jnp.full((c, c), dl_ref[0, h], F32))
        lg_b = log_sigmoid(jnp.full((c, c), dl_ref[1, h], F32))
        dm_ref[h] = QK_SCALE * jnp.where(diff >= 0, jnp.exp(lg_f * jnp.maximum(diff, 0.0)),
                                         jnp.exp(lg_b * jnp.maximum(-diff, 0.0)))
        rd_ref[h, 0, :, 0:c] = (QK_SCALE * jnp.exp(lg_f * (n + 1.0))).astype(BF16)
        rd_ref[h, 0, :, c:2 * c] = (QK_SCALE * jnp.exp(lg_b * (c - n))).astype(BF16)
        rd_ref[h, 1, :, 0:c] = jnp.exp(lg_f * (c - 1.0 - n)).astype(BF16)
        rd_ref[h, 1, :, c:2 * c] = jnp.exp(lg_b * n).astype(BF16)
        cd_ref[h, :, 0:c] = jnp.exp(lg_f[0:SUBLANES, :] * float(c))
        cd_ref[h, :, c:2 * c] = jnp.exp(lg_b[0:SUBLANES, :] * float(c))
    fg = FOURIER_GROUP
    ab_ref[...] = jnp.zeros(ab_ref.shape, BF16)
    for g in range(N_FOURIER_GROUPS):
        wf = wf_ref[g]
        a = jnp.dot(cc_ref[...], wf, preferred_element_type=F32, precision=lax.Precision.HIGHEST)
        b = jnp.dot(sc_ref[...], wf, preferred_element_type=F32, precision=lax.Precision.HIGHEST)
        pair, off = g // 2, (g % 2) * fg
        ab_ref[0, pair, off:off + fg, off:off + fg] = a.astype(BF16)
        ab_ref[1, pair, off:off + fg, off:off + fg] = b.astype(BF16)


def _tables(decay_logit, w_fourier):
    c = RET_CHUNK
    idx = np.arange(FOURIER_GROUP)
    ang = 2.0 * np.pi * ((idx[:, None] * idx[None, :]) % FOURIER_GROUP) / FOURIER_GROUP
    cc = jnp.asarray(np.cos(ang) / np.sqrt(FOURIER_GROUP), F32)
    sc = jnp.asarray(np.sin(ang) / np.sqrt(FOURIER_GROUP), F32)
    return pl.pallas_call(
        _tables_kernel,
        out_shape=(
            jax.ShapeDtypeStruct((N_RET_HEADS, c, c), F32),
            jax.ShapeDtypeStruct((N_RET_HEADS, 2, c, 2 * c), BF16),
            jax.ShapeDtypeStruct((N_RET_HEADS, SUBLANES, 2 * c), F32),
            jax.ShapeDtypeStruct((2, N_FOURIER_GROUPS // 2, 2 * FOURIER_GROUP, 2 * FOURIER_GROUP), BF16),
        ),
        in_specs=[pl.BlockSpec(memory_space=pltpu.SMEM)] + [pl.BlockSpec(memory_space=pltpu.VMEM)] * 3,
        name="tables",
    )(decay_logit, w_fourier, cc, sc)


def _inproj_kernel(x_ref, shift_ref, mult_ref, cos_ref, sin_ref, *refs, n_w, rot_lo, rot_hi, silu_cols):
    w_refs, o_ref, wb_ref = refs[:n_w], refs[n_w], refs[n_w + 1]
    n_out = wb_ref.shape[1]
    hd = RET_HEAD_DIM
    bn = INPROJ_COL_CHUNK

    @pl.when(jnp.logical_and(pl.program_id(0) == 0, pl.program_id(1) == 0))
    def _():
        for i, w_ref in enumerate(w_refs):
            width = w_ref.shape[1]
            for j in range(width // bn):
                wb_ref[:, i * width + j * bn:i * width + (j + 1) * bn] = w_ref[:, j * bn:(j + 1) * bn].astype(BF16)

    x = x_ref[0]
    ms = jnp.mean(x * x, axis=-1, keepdims=True)
    hb = (x * lax.rsqrt(ms + EPS) * mult_ref[0] + shift_ref[0]).astype(BF16)
    is_u1 = (lax.broadcasted_iota(jnp.int32, (x.shape[0], hd), 1) & (hd // 4)) == 0

    def epilogue_weight(j):
        lo = j * bn
        return (lo in silu_cols) + (rot_lo <= lo < rot_hi)

    for j in sorted(range(n_out // bn), key=epilogue_weight, reverse=True):
        p = jnp.dot(hb, wb_ref[:, j * bn:(j + 1) * bn], preferred_element_type=F32)
        for i in range(bn // hd):
            ph = p[:, i * hd:(i + 1) * hd]
            lo = j * bn + i * hd
            if lo in silu_cols:
                ph = _silu(ph)
            if rot_lo <= lo < rot_hi:
                partner = jnp.where(is_u1, pltpu.roll(ph, hd - hd // 4, axis=1), pltpu.roll(ph, hd // 4, axis=1))
                ph = ph * cos_ref[...] + partner * sin_ref[...]
            o_ref[0, :, lo:lo + hd] = ph.astype(BF16)


def _inproj(x, mod, w, cos_t, sin_t, *, mod_row, col_block, blocks, rot_lo, rot_hi, silu_cols, tm):
    bsz, length, _ = x.shape
    n_out = len(blocks) * col_block
    def mod_map(k):
        return (lambda b, t: (b, 0, k)) if mod_row is None else (lambda b, t: (mod_row, 0, k))
    rot_map = (lambda b, t: (t, 0)) if rot_hi > rot_lo else (lambda b, t: (0, 0))
    w_specs = [pl.BlockSpec((D_MODEL, col_block), functools.partial(lambda b, t, blk: (0, blk), blk=blk),
                            pipeline_mode=pl.Buffered(1)) for blk in blocks]
    return pl.pallas_call(
        functools.partial(_inproj_kernel, n_w=len(blocks), rot_lo=rot_lo, rot_hi=rot_hi,
                          silu_cols=frozenset(silu_cols)),
        grid=(bsz, length // tm),
        in_specs=[
            pl.BlockSpec((1, tm, D_MODEL), lambda b, t: (b, t, 0)),
            pl.BlockSpec((1, 1, D_MODEL), mod_map(MOD_SHIFT)),
            pl.BlockSpec((1, 1, D_MODEL), mod_map(MOD_MULT)),
            pl.BlockSpec((tm, RET_HEAD_DIM), rot_map),
            pl.BlockSpec((tm, RET_HEAD_DIM), rot_map),
        ] + w_specs,
        out_specs=pl.BlockSpec((1, tm, n_out), lambda b, t: (b, t, 0)),
        out_shape=jax.ShapeDtypeStruct((bsz, length, n_out), BF16),
        scratch_shapes=[pltpu.VMEM((D_MODEL, n_out), BF16)],
        compiler_params=pltpu.CompilerParams(
            dimension_semantics=("arbitrary", "arbitrary"), vmem_limit_bytes=VMEM_LIMIT),
        name="inproj",
    )(x, mod, mod, cos_t, sin_t, *([w] * len(blocks)))


def _rotary_tables(seq_len):
    n_freq = RET_HEAD_DIM // 4
    pos = np.arange(seq_len)
    row = (pos // GRID_W).astype(np.float64)
    col = (pos % GRID_W).astype(np.float64)
    inv_freq = ROPE_BASE ** (-np.arange(n_freq, dtype=np.float64) / n_freq)
    ang_r = row[:, None] * inv_freq[None, :]
    ang_c = col[:, None] * inv_freq[None, :]
    cos_t = np.concatenate([np.cos(ang_r), np.cos(ang_r), np.cos(ang_c), np.cos(ang_c)], axis=1)
    sin_t = np.concatenate([-np.sin(ang_r), np.sin(ang_r), -np.sin(ang_c), np.sin(ang_c)], axis=1)
    return jnp.asarray(cos_t, F32), jnp.asarray(sin_t, F32)


def _fourier_outproj_kernel(u_ref, g_ref, ab_ref, c_ref, s_ref, js_ref, yr_ref, w_ref, x_ref, gate_ref,
                            o_ref, ue_ref, uo_ref, pe_ref, qo_ref, gs_ref, mid_ref, yf_ref, wb_ref):
    b = pl.program_id(0)
    t = pl.program_id(1)
    n_batch = pl.num_programs(0) - 1
    slot = b % 2
    n = u_ref.shape[1]
    h = n // 2
    fb = FOLD_BLOCK
    nb = h // fb
    mb = FOURIER_ROW_BLOCK
    fg = FOURIER_GROUP
    inv_sqrt_n = 1.0 / np.sqrt(n)
    first_row = lax.broadcasted_iota(jnp.int32, (fb, D_FOURIER), 0) == 0

    @pl.when(jnp.logical_and(b == 0, t == 0))
    def _():
        wb_ref[...] = w_ref[...].astype(BF16)

    def channel_map(v, half):
        return jnp.concatenate(
            [jnp.dot(v[:, p * 2 * fg:(p + 1) * 2 * fg], ab_ref[half, p], preferred_element_type=F32)
             for p in range(N_FOURIER_GROUPS // 2)], axis=1)

    def fold_and_channel_map():
        js = js_ref[...]
        for i in range(nb):
            blk = slice(i * fb, (i + 1) * fb)
            lo = u_ref[0, blk, :].astype(F32)
            src = n - (i + 1) * fb
            r = jnp.dot(js, u_ref[0, src:src + fb, :], preferred_element_type=F32)
            if i > 0:
                r = jnp.where(first_row, u_ref[0, n - i * fb:n - i * fb + 16, :][0:1, :].astype(F32), r)
            ue_ref[blk, :] = (lo + r).astype(BF16)
            uo_ref[blk, :] = (lo - r).astype(BF16)
        for i in range(nb):
            blk = slice(i * fb, (i + 1) * fb)
            pe_ref[blk, :] = channel_map(ue_ref[blk, :], 0).astype(BF16)
            qo_ref[blk, :] = channel_map(uo_ref[blk, :], 1).astype(BF16)
        p_mid = channel_map(u_ref[0, h:h + 16, :], 0)[0:1, :] * inv_sqrt_n
        sign = (1 - 2 * (lax.broadcasted_iota(jnp.int32, (h, D_FOURIER), 0) & 1)).astype(F32)
        y_mid = jnp.sum(pe_ref[...].astype(F32) * sign, axis=0, keepdims=True) * inv_sqrt_n + p_mid
        mid_ref[0] = jnp.broadcast_to(p_mid, (SUBLANES, D_FOURIER))
        mid_ref[1] = jnp.broadcast_to(y_mid, (SUBLANES, D_FOURIER))

    def dft_block(i):
        blk = slice(i * mb, (i + 1) * mb)
        sign = (1 - 2 * (lax.broadcasted_iota(jnp.int32, (mb, D_FOURIER), 0) & 1)).astype(F32)
        yc = jnp.dot(c_ref[blk, :], pe_ref[...], preferred_element_type=F32) + sign * mid_ref[0, 0:1, :]
        ys = jnp.dot(s_ref[blk, :], qo_ref[...], preferred_element_type=F32)
        yf_ref[slot, blk, :] = ((yc - ys) * g_ref[0, blk, :].astype(F32)).astype(BF16)
        gs_ref[blk, :] = (yc + ys).astype(BF16)

    def reflect_upper_half():
        js = js_ref[...]
        for i in range(nb):
            src = h - (i + 1) * fb
            r = jnp.dot(js, gs_ref[src:src + fb, :], preferred_element_type=F32)
            edge = mid_ref[1, 0:1, :] if i == 0 else gs_ref[h - i * fb:h - i * fb + 16, :][0:1, :].astype(F32)
            r = jnp.where(first_row, edge, r)
            rows = slice(h + i * fb, h + (i + 1) * fb)
            yf_ref[slot, rows, :] = (r * g_ref[0, rows, :].astype(F32)).astype(BF16)

    phases = [fold_and_channel_map] + [functools.partial(dft_block, i) for i in range(h // mb)] + [reflect_upper_half]
    per_step = len(phases) // FUSED_STEPS
    assert per_step * FUSED_STEPS == len(phases)
    def fourier_part(k):
        for phase in phases[k * per_step:(k + 1) * per_step]:
            phase()

    def outproj_tile():
        tm = x_ref.shape[1]
        rb = OUTPROJ_ROW_BLOCK
        for r in range(tm // rb):
            rows = slice(r * rb, (r + 1) * rb)
            yf_rows = pl.ds(pl.multiple_of(t * tm, tm) + r * rb, rb)
            y = jnp.dot(yf_ref[1 - slot, yf_rows, :], wb_ref[0:D_FOURIER, :], preferred_element_type=F32)
            y += jnp.dot(yr_ref[0, rows, :], wb_ref[D_FOURIER:D_MIX, :], preferred_element_type=F32)
            ms = jnp.mean(y * y, axis=-1, keepdims=True)
            o_ref[0, rows, :] = x_ref[0, rows, :] + y * lax.rsqrt(ms + EPS) * gate_ref[0]

    interior = jnp.logical_and(b > 0, b < n_batch)
    for k in range(FUSED_STEPS):
        def both(k=k):
            fourier_part(k)
            outproj_tile()
        pl.when(jnp.logical_and(interior, t == k))(both)
        pl.when(jnp.logical_and(b == 0, t == k))(functools.partial(fourier_part, k))
    pl.when(b == n_batch)(outproj_tile)


def _fourier_outproj(proj, yr, ab, c_mat, s_mat, js, w_out, x, mod):
    bsz, length, _ = x.shape
    h = length // 2
    assert h % FOLD_BLOCK == 0 and h % FOURIER_ROW_BLOCK == 0 and length % FUSED_STEPS == 0
    tm = length // FUSED_STEPS
    assert tm % OUTPROJ_ROW_BLOCK == 0

    def cur(col):
        return lambda b, t: (jnp.minimum(b, bsz - 1), 0, col)

    def prev_tile(b, t):
        return (jnp.maximum(b - 1, 0), jnp.where(b == 0, 0, t), 0)

    const2 = lambda b, t: (0, 0)
    resident = pl.Buffered(1)
    return pl.pallas_call(
        _fourier_outproj_kernel,
        grid=(bsz + 1, FUSED_STEPS),
        in_specs=[
            pl.BlockSpec((1, length, D_FOURIER), cur(COL_F_IN // D_FOURIER)),
            pl.BlockSpec((1, length, D_FOURIER), cur(COL_F_GATE // D_FOURIER)),
            pl.BlockSpec((2, N_FOURIER_GROUPS // 2, 2 * FOURIER_GROUP, 2 * FOURIER_GROUP), lambda b, t: (0, 0, 0, 0),
                         pipeline_mode=resident),
            pl.BlockSpec((h, h), const2, pipeline_mode=resident),
            pl.BlockSpec((h, h), const2, pipeline_mode=resident),
            pl.BlockSpec((FOLD_BLOCK, FOLD_BLOCK), const2, pipeline_mode=resident),
            pl.BlockSpec((1, tm, D_RET), prev_tile),
            pl.BlockSpec((D_MIX, D_MODEL), const2, pipeline_mode=resident),
            pl.BlockSpec((1, tm, D_MODEL), prev_tile),
            pl.BlockSpec((1, 1, D_MODEL), lambda b, t: (jnp.maximum(b - 1, 0), 0, MOD_GATE)),
        ],
        out_specs=pl.BlockSpec((1, tm, D_MODEL), prev_tile),
        out_shape=jax.ShapeDtypeStruct((bsz, length, D_MODEL), F32),
        scratch_shapes=[
            pltpu.VMEM((h, D_FOURIER), BF16),
            pltpu.VMEM((h, D_FOURIER), BF16),
            pltpu.VMEM((h, D_FOURIER), BF16),
            pltpu.VMEM((h, D_FOURIER), BF16),
            pltpu.VMEM((h, D_FOURIER), BF16),
            pltpu.VMEM((2, SUBLANES, D_FOURIER), F32),
            pltpu.VMEM((2, length, D_FOURIER), BF16),
            pltpu.VMEM((D_MIX, D_MODEL), BF16),
        ],
        compiler_params=pltpu.CompilerParams(
            dimension_semantics=("arbitrary", "arbitrary"), vmem_limit_bytes=VMEM_LIMIT),
        name="fourier_outproj",
    )(proj, proj, ab, c_mat, s_mat, js, yr, w_out, x, mod)


def _seq_dft_matrices(length):
    h = length // 2
    idx = np.arange(h)
    ang = 2.0 * np.pi * ((idx[:, None] * idx[None, :]) % length) / length
    scale = 1.0 / np.sqrt(length)
    c_mat = jnp.asarray((np.cos(ang) * scale).astype(np.float32)).astype(BF16)
    s_mat = jnp.asarray((np.sin(ang) * scale).astype(np.float32)).astype(BF16)
    r = np.arange(1, FOLD_BLOCK)
    js = np.zeros((FOLD_BLOCK, FOLD_BLOCK), np.float32)
    js[r, FOLD_BLOCK - r] = 1.0
    return c_mat, s_mat, jnp.asarray(js).astype(BF16)


def _ret_kernel(q_ref, k_ref, v_ref, rg_ref, kc_ref, vc_ref, dm_ref, rd_ref, cd_ref, gain_ref,
                o_ref, u_ref, st_ref, a_ref):
    c = RET_CHUNK
    hd = RET_HEAD_DIM
    n_chunks = q_ref.shape[1] // c
    n_ctx = kc_ref.shape[1] // c

    for h in range(N_RET_HEADS):
        cols = slice(h * hd, (h + 1) * hd)
        dm = dm_ref[h]
        q_dec = rd_ref[h, 0]
        k_dec = rd_ref[h, 1]
        c_dec_f = cd_ref[h, 0:1, 0:hd]
        c_dec_b = cd_ref[h, 0:1, hd:2 * hd]
        gain = gain_ref[:, cols]

        def both_decays(t, dec):
            return jnp.concatenate([t * dec[:, 0:hd], t * dec[:, hd:2 * hd]], axis=1)

        def kv_outer(k, v):
            return lax.dot_general(v, both_decays(k, k_dec), (((0,), (0,)), ((), ())), preferred_element_type=F32)

        for j in range(n_ctx):
            rows = slice(j * c, (j + 1) * c)
            u_ref[j] = kv_outer(kc_ref[0, rows, cols], vc_ref[0, rows, cols])
        for j in range(n_chunks):
            rows = slice(j * c, (j + 1) * c)
            u_ref[n_ctx + j] = kv_outer(k_ref[0, rows, cols], v_ref[0, rows, cols])

        sf = jnp.zeros((hd, hd), F32)
        for j in range(n_ctx + n_chunks - 1):
            sf = c_dec_f * sf + u_ref[j, :, 0:hd]
            if j + 1 >= n_ctx:
                st_ref[j + 1 - n_ctx, :, 0:hd] = sf.astype(BF16)
        sb = jnp.zeros((hd, hd), F32)
        for j in reversed(range(n_ctx)):
            sb = c_dec_b * sb + u_ref[j, :, hd:2 * hd]
        for j in reversed(range(n_chunks)):
            st_ref[j, :, hd:2 * hd] = sb.astype(BF16)
            if j > 0:
                sb = c_dec_b * sb + u_ref[n_ctx + j, :, hd:2 * hd]

        for j in range(n_chunks):
            rows = slice(j * c, (j + 1) * c)
            s = lax.dot_general(q_ref[0, rows, cols], k_ref[0, rows, cols], (((1,), (1,)), ((), ())),
                                preferred_element_type=F32)
            a_ref[j] = (s * dm).astype(BF16)

        for j in range(n_chunks):
            rows = slice(j * c, (j + 1) * c)
            o = jnp.dot(a_ref[j], v_ref[0, rows, cols], preferred_element_type=F32)
            o += lax.dot_general(both_decays(q_ref[0, rows, cols], q_dec), st_ref[j], (((1,), (1,)), ((), ())),
                                 preferred_element_type=F32)
            mu = jnp.mean(o, axis=-1, keepdims=True)
            d = o - mu
            var = jnp.mean(d * d, axis=-1, keepdims=True)
            on = d * lax.rsqrt(var + EPS)
            o_ref[0, rows, cols] = (on * gain * rg_ref[0, rows, cols].astype(F32)).astype(BF16)


def _retention(proj, proj_c, dm, rd, cd, gain):
    bsz, length, _ = proj.shape
    ctx_len = proj_c.shape[1]
    hd = RET_HEAD_DIM
    c = RET_CHUNK
    n_chunks = length // c
    n_ctx = ctx_len // c
    assert n_ctx >= 1

    def col(base):
        return lambda b: (b, 0, base // D_RET)

    return pl.pallas_call(
        _ret_kernel,
        grid=(bsz,),
        in_specs=[
            pl.BlockSpec((1, length, D_RET), col(COL_Q)),
            pl.BlockSpec((1, length, D_RET), col(COL_K)),
            pl.BlockSpec((1, length, D_RET), col(COL_V)),
            pl.BlockSpec((1, length, D_RET), col(COL_R_GATE)),
            pl.BlockSpec((1, ctx_len, D_RET), col(0)),
            pl.BlockSpec((1, ctx_len, D_RET), col(D_RET)),
            pl.BlockSpec((N_RET_HEADS, c, c), lambda b: (0, 0, 0)),
            pl.BlockSpec((N_RET_HEADS, 2, c, 2 * hd), lambda b: (0, 0, 0, 0)),
            pl.BlockSpec((N_RET_HEADS, SUBLANES, 2 * hd), lambda b: (0, 0, 0)),
            pl.BlockSpec((1, D_RET), lambda b: (0, 0)),
        ],
        out_specs=pl.BlockSpec((1, length, D_RET), lambda b: (b, 0, 0)),
        out_shape=jax.ShapeDtypeStruct((bsz, length, D_RET), BF16),
        scratch_shapes=[
            pltpu.VMEM((n_ctx + n_chunks, hd, 2 * hd), F32),
            pltpu.VMEM((n_chunks, hd, 2 * hd), BF16),
            pltpu.VMEM((n_chunks, c, c), BF16),
        ],
        compiler_params=pltpu.CompilerParams(
            dimension_semantics=("parallel",), vmem_limit_bytes=VMEM_LIMIT),
        name="retention",
    )(proj, proj, proj, proj, proj_c, proj_c, dm, rd, cd, gain)


def kernel(x, c, ctx, c_ctx, w_ada, b_ada, g_pre, g_post, w_in, w_fourier, decay_logit, ret_gn_gain, w_out):
    bsz, seq_len, _ = x.shape
    ctx_len = ctx.shape[1]
    assert w_ada.shape[0] == 1, "single layer only"

    mod = _adaln(c, c_ctx, w_ada[0], b_ada[0], g_pre[0], g_post[0])

    dm, rd, cd, ab = _tables(decay_logit[0], w_fourier[0])

    cos_t, sin_t = _rotary_tables(seq_len)

    gate_cols = list(range(COL_F_GATE, COL_Q, RET_HEAD_DIM)) + list(range(COL_R_GATE, D_IN, RET_HEAD_DIM))
    proj = _inproj(x, mod, w_in[0], cos_t, sin_t, mod_row=None, col_block=D_IN, blocks=(0,),
                   rot_lo=COL_Q, rot_hi=COL_V, silu_cols=gate_cols, tm=TOKEN_TILE)
    ctx_rows = bsz * ctx_len
    assert COL_V == COL_K + D_RET and COL_K % D_RET == 0
    proj_c = _inproj(ctx.reshape(1, ctx_rows, D_MODEL), mod, w_in[0], cos_t, sin_t, mod_row=bsz,
                     col_block=D_RET, blocks=(COL_K // D_RET, COL_V // D_RET),
                     rot_lo=0, rot_hi=0, silu_cols=(), tm=min(TOKEN_TILE, ctx_rows))
    proj_c = proj_c.reshape(bsz, ctx_len, 2 * D_RET)

    yr = _retention(proj, proj_c, dm, rd, cd, ret_gn_gain[0].reshape(1, D_RET))
    return _fourier_outproj(proj, yr, ab, *_seq_dft_matrices(seq_len), w_out[0], x, mod)
```

```python
import functools

import numpy as np
import jax
import jax.numpy as jnp
from jax import lax
from jax.experimental import pallas as pl
from jax.experimental.pallas import tpu as pltpu

D_MODEL = 1024
GRID_W = 64
D_FOURIER = 512
N_FOURIER_GROUPS = 4
FOURIER_GROUP = D_FOURIER // N_FOURIER_GROUPS
D_RET = 512
N_RET_HEADS = 4
RET_HEAD_DIM = D_RET // N_RET_HEADS
D_MIX = D_FOURIER + D_RET
D_IN = 2 * D_FOURIER + 4 * D_RET
RET_CHUNK = 128
ROPE_BASE = 10000.0
QK_SCALE = RET_HEAD_DIM ** -0.5
EPS = 1e-6

COL_F_IN = 0
COL_F_GATE = D_FOURIER
COL_Q = 2 * D_FOURIER
COL_K = COL_Q + D_RET
COL_V = COL_K + D_RET
COL_R_GATE = COL_V + D_RET

MXU_COLS = 256
FOLD_BLOCK = 256
FOURIER_ROW_BLOCK = 512
INPROJ_COL_CHUNK = 2 * MXU_COLS
OUTPROJ_ROW_BLOCK = 256
FUSED_STEPS = 2
SUBLANES = 8
BF16_TILE_ROWS = 16
MOD_SHIFT, MOD_MULT, MOD_GATE = 0, 1, 2
TOKEN_TILE = 1024
VMEM_LIMIT = 56 * 1024 * 1024

F32 = jnp.float32
BF16 = jnp.bfloat16


def _silu(v):
    return v * (0.5 * jnp.tanh(0.5 * v) + 0.5)


def _adaln_kernel(c_ref, cc_ref, w_ref, b_ref, gpre_ref, gpost_ref, o_ref, s_ref):
    j = pl.program_id(0)
    nb = c_ref.shape[0]
    s_ref[0:nb, :] = _silu(c_ref[...])
    s_ref[nb:, :] = jnp.broadcast_to(_silu(cc_ref[...]), (s_ref.shape[0] - nb, D_MODEL))
    m = jnp.dot(s_ref[...], w_ref[...], preferred_element_type=F32) + b_ref[...]
    add = jnp.where(j == 1, 1.0, 0.0)
    mul = jnp.where(j == 1, gpre_ref[...], jnp.where(j == 2, gpost_ref[...], 1.0))
    o_ref[...] = ((m + add) * mul)[:, None, :]


def _adaln(c, c_ctx, w_ada, b_ada, g_pre, g_post):
    bsz = c.shape[0]
    n_out = w_ada.shape[1]
    bn = D_MODEL
    assert n_out == 3 * bn and bsz % SUBLANES == 0
    rows = bsz + SUBLANES
    const = lambda j: (0, 0)
    return pl.pallas_call(
        _adaln_kernel,
        grid=(n_out // bn,),
        in_specs=[
            pl.BlockSpec((bsz, D_MODEL), const),
            pl.BlockSpec((1, D_MODEL), const),
            pl.BlockSpec((D_MODEL, bn), lambda j: (0, j)),
            pl.BlockSpec((1, bn), lambda j: (0, j)),
            pl.BlockSpec((1, D_MODEL), const),
            pl.BlockSpec((1, D_MODEL), const),
        ],
        out_specs=pl.BlockSpec((rows, 1, bn), lambda j: (0, 0, j)),
        out_shape=jax.ShapeDtypeStruct((rows, 1, n_out), F32),
        scratch_shapes=[pltpu.VMEM((rows, D_MODEL), F32)],
        name="adaln",
    )(c, c_ctx.reshape(1, D_MODEL), w_ada, b_ada.reshape(1, n_out), g_pre.reshape(1, D_MODEL),
      g_post.reshape(1, D_MODEL))


def _tables_kernel(dl_ref, wf_ref, cc_ref, sc_ref, dm_ref, rd_ref, cd_ref, ab_ref):
    c = RET_CHUNK
    n = lax.broadcasted_iota(jnp.int32, (c, c), 0).astype(F32)
    m = lax.broadcasted_iota(jnp.int32, (c, c), 1).astype(F32)
    diff = n - m

    def log_sigmoid(v):
        return jnp.minimum(v, 0.0) - jnp.log1p(jnp.exp(-jnp.abs(v)))

    for h in range(N_RET_HEADS):
        lg_f = log_sigmoid(jnp.full((c, c), dl_ref[0, h], F32))
        lg_b = log_sigmoid(jnp.full((c, c), dl_ref[1, h], F32))
        dm_ref[h] = QK_SCALE * jnp.where(diff >= 0, jnp.exp(lg_f * jnp.maximum(diff, 0.0)),
                                         jnp.exp(lg_b * jnp.maximum(-diff, 0.0)))
        rd_ref[h, 0, :, 0:c] = (QK_SCALE * jnp.exp(lg_f * (n + 1.0))).astype(BF16)
        rd_ref[h, 0, :, c:2 * c] = (QK_SCALE * jnp.exp(lg_b * (c - n))).astype(BF16)
        rd_ref[h, 1, :, 0:c] = jnp.exp(lg_f * (c - 1.0 - n)).astype(BF16)
        rd_ref[h, 1, :, c:2 * c] = jnp.exp(lg_b * n).astype(BF16)
        cd_ref[h, :, 0:c] = jnp.exp(lg_f[0:SUBLANES, :] * float(c))
        cd_ref[h, :, c:2 * c] = jnp.exp(lg_b[0:SUBLANES, :] * float(c))
    fg = FOURIER_GROUP
    ab_ref[...] = jnp.zeros(ab_ref.shape, BF16)
    for g in range(N_FOURIER_GROUPS):
        wf = wf_ref[g]
        a = jnp.dot(cc_ref[...], wf, preferred_element_type=F32, precision=lax.Precision.HIGHEST)
        b = jnp.dot(sc_ref[...], wf, preferred_element_type=F32, precision=lax.Precision.HIGHEST)
        pair, off = g // 2, (g % 2) * fg
        ab_ref[0, pair, off:off + fg, off:off + fg] = a.astype(BF16)
        ab_ref[1, pair, off:off + fg, off:off + fg] = b.astype(BF16)


def _tables(decay_logit, w_fourier):
    c = RET_CHUNK
    idx = np.arange(FOURIER_GROUP)
    ang = 2.0 * np.pi * ((idx[:, None] * idx[None, :]) % FOURIER_GROUP) / FOURIER_GROUP
    cc = jnp.asarray(np.cos(ang) / np.sqrt(FOURIER_GROUP), F32)
    sc = jnp.asarray(np.sin(ang) / np.sqrt(FOURIER_GROUP), F32)
    return pl.pallas_call(
        _tables_kernel,
        out_shape=(
            jax.ShapeDtypeStruct((N_RET_HEADS, c, c), F32),
            jax.ShapeDtypeStruct((N_RET_HEADS, 2, c, 2 * c), BF16),
            jax.ShapeDtypeStruct((N_RET_HEADS, SUBLANES, 2 * c), F32),
            jax.ShapeDtypeStruct((2, N_FOURIER_GROUPS // 2, 2 * FOURIER_GROUP, 2 * FOURIER_GROUP), BF16),
        ),
        in_specs=[pl.BlockSpec(memory_space=pltpu.SMEM)] + [pl.BlockSpec(memory_space=pltpu.VMEM)] * 3,
        name="tables",
    )(decay_logit, w_fourier, cc, sc)


def _inproj_kernel(x_ref, shift_ref, mult_ref, cos_ref, sin_ref, *refs, n_w, rot_lo, rot_hi, silu_cols, vt_lo):
    w_refs, o_ref, vt_ref, wb_ref = refs[:n_w], refs[n_w], refs[n_w + 1], refs[n_w + 2]
    n_out = wb_ref.shape[1]
    hd = RET_HEAD_DIM
    bn = INPROJ_COL_CHUNK

    @pl.when(jnp.logical_and(pl.program_id(0) == 0, pl.program_id(1) == 0))
    def _():
        for i, w_ref in enumerate(w_refs):
            width = w_ref.shape[1]
            for j in range(width // bn):
                wb_ref[:, i * width + j * bn:i * width + (j + 1) * bn] = w_ref[:, j * bn:(j + 1) * bn].astype(BF16)

    x = x_ref[0]
    ms = jnp.mean(x * x, axis=-1, keepdims=True)
    hb = (x * lax.rsqrt(ms + EPS) * mult_ref[0] + shift_ref[0]).astype(BF16)
    is_u1 = (lax.broadcasted_iota(jnp.int32, (x.shape[0], hd), 1) & (hd // 4)) == 0

    def epilogue_weight(j):
        lo = j * bn
        return (lo in silu_cols) + (rot_lo <= lo < rot_hi)

    for j in sorted(range(n_out // bn), key=epilogue_weight, reverse=True):
        p = jnp.dot(hb, wb_ref[:, j * bn:(j + 1) * bn], preferred_element_type=F32)
        for i in range(bn // hd):
            ph = p[:, i * hd:(i + 1) * hd]
            lo = j * bn + i * hd
            if lo in silu_cols:
                ph = _silu(ph)
            if rot_lo <= lo < rot_hi:
                partner = jnp.where(is_u1, pltpu.roll(ph, hd - hd // 4, axis=1), pltpu.roll(ph, hd // 4, axis=1))
                ph = ph * cos_ref[...] + partner * sin_ref[...]
            o_ref[0, :, lo:lo + hd] = ph.astype(BF16)
            if vt_lo <= lo < vt_lo + D_RET:
                vt_ref[0, (lo - vt_lo) // hd, :, :] = ph.T.astype(BF16)


def _inproj(x, mod, w, cos_t, sin_t, *, mod_row, col_block, blocks, rot_lo, rot_hi, silu_cols, vt_lo, tm):
    bsz, length, _ = x.shape
    n_out = len(blocks) * col_block
    def mod_map(k):
        return (lambda b, t: (b, 0, k)) if mod_row is None else (lambda b, t: (mod_row, 0, k))
    rot_map = (lambda b, t: (t, 0)) if rot_hi > rot_lo else (lambda b, t: (0, 0))
    w_specs = [pl.BlockSpec((D_MODEL, col_block), functools.partial(lambda b, t, blk: (0, blk), blk=blk),
                            pipeline_mode=pl.Buffered(1)) for blk in blocks]
    return pl.pallas_call(
        functools.partial(_inproj_kernel, n_w=len(blocks), rot_lo=rot_lo, rot_hi=rot_hi,
                          silu_cols=frozenset(silu_cols), vt_lo=vt_lo),
        grid=(bsz, length // tm),
        in_specs=[
            pl.BlockSpec((1, tm, D_MODEL), lambda b, t: (b, t, 0)),
            pl.BlockSpec((1, 1, D_MODEL), mod_map(MOD_SHIFT)),
            pl.BlockSpec((1, 1, D_MODEL), mod_map(MOD_MULT)),
            pl.BlockSpec((tm, RET_HEAD_DIM), rot_map),
            pl.BlockSpec((tm, RET_HEAD_DIM), rot_map),
        ] + w_specs,
        out_specs=(pl.BlockSpec((1, tm, n_out), lambda b, t: (b, t, 0)),
                   pl.BlockSpec((1, N_RET_HEADS, RET_HEAD_DIM, tm), lambda b, t: (b, 0, 0, t))),
        out_shape=(jax.ShapeDtypeStruct((bsz, length, n_out), BF16),
                   jax.ShapeDtypeStruct((bsz, N_RET_HEADS, RET_HEAD_DIM, length), BF16)),
        scratch_shapes=[pltpu.VMEM((D_MODEL, n_out), BF16)],
        compiler_params=pltpu.CompilerParams(
            dimension_semantics=("arbitrary", "arbitrary"), vmem_limit_bytes=VMEM_LIMIT),
        name="inproj",
    )(x, mod, mod, cos_t, sin_t, *([w] * len(blocks)))


def _rotary_tables(seq_len):
    n_freq = RET_HEAD_DIM // 4
    pos = np.arange(seq_len)
    row = (pos // GRID_W).astype(np.float64)
    col = (pos % GRID_W).astype(np.float64)
    inv_freq = ROPE_BASE ** (-np.arange(n_freq, dtype=np.float64) / n_freq)
    ang_r = row[:, None] * inv_freq[None, :]
    ang_c = col[:, None] * inv_freq[None, :]
    cos_t = np.concatenate([np.cos(ang_r), np.cos(ang_r), np.cos(ang_c), np.cos(ang_c)], axis=1)
    sin_t = np.concatenate([-np.sin(ang_r), np.sin(ang_r), -np.sin(ang_c), np.sin(ang_c)], axis=1)
    return jnp.asarray(cos_t, F32), jnp.asarray(sin_t, F32)


def _fourier_outproj_kernel(u_ref, g_ref, ab_ref, c_ref, s_ref, js_ref, yr_ref, w_ref, x_ref, gate_ref,
                            o_ref, ue_ref, uo_ref, pe_ref, qo_ref, gs_ref, mid_ref, yf_ref, wb_ref):
    b = pl.program_id(0)
    t = pl.program_id(1)
    n_batch = pl.num_programs(0) - 1
    slot = b % 2
    n = u_ref.shape[1]
    h = n // 2
    fb = FOLD_BLOCK
    nb = h // fb
    mb = FOURIER_ROW_BLOCK
    fg = FOURIER_GROUP
    inv_sqrt_n = 1.0 / np.sqrt(n)
    first_row = lax.broadcasted_iota(jnp.int32, (fb, D_FOURIER), 0) == 0

    @pl.when(jnp.logical_and(b == 0, t == 0))
    def _():
        wb_ref[...] = w_ref[...].astype(BF16)

    def channel_map(v, half):
        return jnp.concatenate(
            [jnp.dot(v[:, p * 2 * fg:(p + 1) * 2 * fg], ab_ref[half, p], preferred_element_type=F32)
             for p in range(N_FOURIER_GROUPS // 2)], axis=1)

    def fold_and_channel_map():
        js = js_ref[...]
        for i in range(nb):
            blk = slice(i * fb, (i + 1) * fb)
            lo = u_ref[0, blk, :].astype(F32)
            src = n - (i + 1) * fb
            r = jnp.dot(js, u_ref[0, src:src + fb, :], preferred_element_type=F32)
            if i > 0:
                edge = u_ref[0, n - i * fb:n - i * fb + BF16_TILE_ROWS, :][0:1, :]
                r = jnp.where(first_row, edge.astype(F32), r)
            ue_ref[blk, :] = (lo + r).astype(BF16)
            uo_ref[blk, :] = (lo - r).astype(BF16)
        for i in range(nb):
            blk = slice(i * fb, (i + 1) * fb)
            pe_ref[blk, :] = channel_map(ue_ref[blk, :], 0).astype(BF16)
            qo_ref[blk, :] = channel_map(uo_ref[blk, :], 1).astype(BF16)
        p_mid = channel_map(u_ref[0, h:h + BF16_TILE_ROWS, :], 0)[0:1, :] * inv_sqrt_n
        sign = (1 - 2 * (lax.broadcasted_iota(jnp.int32, (h, D_FOURIER), 0) & 1)).astype(F32)
        y_mid = jnp.sum(pe_ref[...].astype(F32) * sign, axis=0, keepdims=True) * inv_sqrt_n + p_mid
        mid_ref[0] = jnp.broadcast_to(p_mid, (SUBLANES, D_FOURIER))
        mid_ref[1] = jnp.broadcast_to(y_mid, (SUBLANES, D_FOURIER))

    def dft_block(i):
        blk = slice(i * mb, (i + 1) * mb)
        sign = (1 - 2 * (lax.broadcasted_iota(jnp.int32, (mb, D_FOURIER), 0) & 1)).astype(F32)
        yc = jnp.dot(c_ref[blk, :], pe_ref[...], preferred_element_type=F32) + sign * mid_ref[0, 0:1, :]
        ys = jnp.dot(s_ref[blk, :], qo_ref[...], preferred_element_type=F32)
        yf_ref[slot, blk, :] = ((yc - ys) * g_ref[0, blk, :].astype(F32)).astype(BF16)
        gs_ref[blk, :] = (yc + ys).astype(BF16)

    def reflect_upper_half():
        js = js_ref[...]
        for i in range(nb):
            src = h - (i + 1) * fb
            r = jnp.dot(js, gs_ref[src:src + fb, :], preferred_element_type=F32)
            if i == 0:
                edge = mid_ref[1, 0:1, :]
            else:
                edge = gs_ref[h - i * fb:h - i * fb + BF16_TILE_ROWS, :][0:1, :].astype(F32)
            r = jnp.where(first_row, edge, r)
            rows = slice(h + i * fb, h + (i + 1) * fb)
            yf_ref[slot, rows, :] = (r * g_ref[0, rows, :].astype(F32)).astype(BF16)

    phases = [fold_and_channel_map] + [functools.partial(dft_block, i) for i in range(h // mb)] + [reflect_upper_half]
    assert FUSED_STEPS == 2
    bounds = (0, 1, len(phases))

    def fourier_part(k):
        for phase in phases[bounds[k]:bounds[k + 1]]:
            phase()

    def outproj_tile():
        tm = x_ref.shape[1]
        rb = OUTPROJ_ROW_BLOCK
        for r in range(tm // rb):
            rows = slice(r * rb, (r + 1) * rb)
            yf_rows = pl.ds(pl.multiple_of(t * tm, tm) + r * rb, rb)
            y = jnp.dot(yf_ref[1 - slot, yf_rows, :], wb_ref[0:D_FOURIER, :], preferred_element_type=F32)
            y += jnp.dot(yr_ref[0, rows, :], wb_ref[D_FOURIER:D_MIX, :], preferred_element_type=F32)
            ms = jnp.mean(y * y, axis=-1, keepdims=True)
            o_ref[0, rows, :] = x_ref[0, rows, :] + y * lax.rsqrt(ms + EPS) * gate_ref[0]

    interior = jnp.logical_and(b > 0, b < n_batch)
    for k in range(FUSED_STEPS):
        def both(k=k):
            fourier_part(k)
            outproj_tile()
        pl.when(jnp.logical_and(interior, t == k))(both)
        pl.when(jnp.logical_and(b == 0, t == k))(functools.partial(fourier_part, k))
    pl.when(b == n_batch)(outproj_tile)


def _fourier_outproj(proj, yr, ab, c_mat, s_mat, js, w_out, x, mod):
    bsz, length, _ = x.shape
    h = length // 2
    assert h % FOLD_BLOCK == 0 and h % FOURIER_ROW_BLOCK == 0 and length % FUSED_STEPS == 0
    tm = length // FUSED_STEPS
    assert tm % OUTPROJ_ROW_BLOCK == 0

    def cur(col):
        return lambda b, t: (jnp.minimum(b, bsz - 1), 0, col)

    def prev_tile(b, t):
        return (jnp.maximum(b - 1, 0), jnp.where(b == 0, 0, t), 0)

    const2 = lambda b, t: (0, 0)
    resident = pl.Buffered(1)
    return pl.pallas_call(
        _fourier_outproj_kernel,
        grid=(bsz + 1, FUSED_STEPS),
        in_specs=[
            pl.BlockSpec((1, length, D_FOURIER), cur(COL_F_IN // D_FOURIER)),
            pl.BlockSpec((1, length, D_FOURIER), cur(COL_F_GATE // D_FOURIER)),
            pl.BlockSpec((2, N_FOURIER_GROUPS // 2, 2 * FOURIER_GROUP, 2 * FOURIER_GROUP), lambda b, t: (0, 0, 0, 0),
                         pipeline_mode=resident),
            pl.BlockSpec((h, h), const2, pipeline_mode=resident),
            pl.BlockSpec((h, h), const2, pipeline_mode=resident),
            pl.BlockSpec((FOLD_BLOCK, FOLD_BLOCK), const2, pipeline_mode=resident),
            pl.BlockSpec((1, tm, D_RET), prev_tile),
            pl.BlockSpec((D_MIX, D_MODEL), const2, pipeline_mode=resident),
            pl.BlockSpec((1, tm, D_MODEL), prev_tile),
            pl.BlockSpec((1, 1, D_MODEL), lambda b, t: (jnp.maximum(b - 1, 0), 0, MOD_GATE)),
        ],
        out_specs=pl.BlockSpec((1, tm, D_MODEL), prev_tile),
        out_shape=jax.ShapeDtypeStruct((bsz, length, D_MODEL), F32),
        scratch_shapes=[
            pltpu.VMEM((h, D_FOURIER), BF16),
            pltpu.VMEM((h, D_FOURIER), BF16),
            pltpu.VMEM((h, D_FOURIER), BF16),
            pltpu.VMEM((h, D_FOURIER), BF16),
            pltpu.VMEM((h, D_FOURIER), BF16),
            pltpu.VMEM((2, SUBLANES, D_FOURIER), F32),
            pltpu.VMEM((2, length, D_FOURIER), BF16),
            pltpu.VMEM((D_MIX, D_MODEL), BF16),
        ],
        compiler_params=pltpu.CompilerParams(
            dimension_semantics=("arbitrary", "arbitrary"), vmem_limit_bytes=VMEM_LIMIT),
        name="fourier_outproj",
    )(proj, proj, ab, c_mat, s_mat, js, yr, w_out, x, mod)


def _seq_dft_matrices(length):
    h = length // 2
    idx = np.arange(h)
    ang = 2.0 * np.pi * ((idx[:, None] * idx[None, :]) % length) / length
    scale = 1.0 / np.sqrt(length)
    c_mat = jnp.asarray((np.cos(ang) * scale).astype(np.float32)).astype(BF16)
    s_mat = jnp.asarray((np.sin(ang) * scale).astype(np.float32)).astype(BF16)
    r = np.arange(1, FOLD_BLOCK)
    js = np.zeros((FOLD_BLOCK, FOLD_BLOCK), np.float32)
    js[r, FOLD_BLOCK - r] = 1.0
    return c_mat, s_mat, jnp.asarray(js).astype(BF16)


def _ret_kernel(q_ref, k_ref, v_ref, vt_ref, rg_ref, kc_ref, vtc_ref, dm_ref, rd_ref, cd_ref, gain_ref,
                o_ref, u_ref, st_ref, a_ref):
    c = RET_CHUNK
    hd = RET_HEAD_DIM
    n_chunks = q_ref.shape[1] // c
    n_ctx = kc_ref.shape[1] // c

    for h in range(N_RET_HEADS):
        cols = slice(h * hd, (h + 1) * hd)
        dm = dm_ref[h]
        q_dec = rd_ref[h, 0]
        k_dec = rd_ref[h, 1]
        c_dec_f = cd_ref[h, 0:1, 0:hd]
        c_dec_b = cd_ref[h, 0:1, hd:2 * hd]
        gain = gain_ref[:, cols]

        def both_decays(t, dec):
            return jnp.concatenate([t * dec[:, 0:hd], t * dec[:, hd:2 * hd]], axis=1)

        def kv_outer(k, v_t):
            return jnp.dot(v_t, both_decays(k, k_dec), preferred_element_type=F32)

        for j in range(n_ctx):
            rows = slice(j * c, (j + 1) * c)
            u_ref[j] = kv_outer(kc_ref[0, rows, cols], vtc_ref[0, h, :, rows])
        for j in range(n_chunks):
            rows = slice(j * c, (j + 1) * c)
            u_ref[n_ctx + j] = kv_outer(k_ref[0, rows, cols], vt_ref[0, h, :, rows])

        sf = jnp.zeros((hd, hd), F32)
        for j in range(n_ctx + n_chunks - 1):
            sf = c_dec_f * sf + u_ref[j, :, 0:hd]
            if j + 1 >= n_ctx:
                st_ref[j + 1 - n_ctx, :, 0:hd] = sf.astype(BF16)
        sb = jnp.zeros((hd, hd), F32)
        for j in reversed(range(n_ctx)):
            sb = c_dec_b * sb + u_ref[j, :, hd:2 * hd]
        for j in reversed(range(n_chunks)):
            st_ref[j, :, hd:2 * hd] = sb.astype(BF16)
            if j > 0:
                sb = c_dec_b * sb + u_ref[n_ctx + j, :, hd:2 * hd]

        for j in range(n_chunks):
            rows = slice(j * c, (j + 1) * c)
            s = lax.dot_general(q_ref[0, rows, cols], k_ref[0, rows, cols], (((1,), (1,)), ((), ())),
                                preferred_element_type=F32)
            a_ref[j] = (s * dm).astype(BF16)

        for j in range(n_chunks):
            rows = slice(j * c, (j + 1) * c)
            o = jnp.dot(a_ref[j], v_ref[0, rows, cols], preferred_element_type=F32)
            o += lax.dot_general(both_decays(q_ref[0, rows, cols], q_dec), st_ref[j], (((1,), (1,)), ((), ())),
                                 preferred_element_type=F32)
            mu = jnp.mean(o, axis=-1, keepdims=True)
            d = o - mu
            var = jnp.mean(d * d, axis=-1, keepdims=True)
            on = d * lax.rsqrt(var + EPS)
            o_ref[0, rows, cols] = (on * gain * rg_ref[0, rows, cols].astype(F32)).astype(BF16)


def _retention(proj, vt, proj_c, vt_c, dm, rd, cd, gain):
    bsz, length, _ = proj.shape
    ctx_len = proj_c.shape[1]
    hd = RET_HEAD_DIM
    c = RET_CHUNK
    n_chunks = length // c
    n_ctx = ctx_len // c
    assert n_ctx >= 1

    def col(base):
        return lambda b: (b, 0, base // D_RET)

    return pl.pallas_call(
        _ret_kernel,
        grid=(bsz,),
        in_specs=[
            pl.BlockSpec((1, length, D_RET), col(COL_Q)),
            pl.BlockSpec((1, length, D_RET), col(COL_K)),
            pl.BlockSpec((1, length, D_RET), col(COL_V)),
            pl.BlockSpec((1, N_RET_HEADS, hd, length), lambda b: (b, 0, 0, 0)),
            pl.BlockSpec((1, length, D_RET), col(COL_R_GATE)),
            pl.BlockSpec((1, ctx_len, D_RET), col(0)),
            pl.BlockSpec((1, N_RET_HEADS, hd, ctx_len), lambda b: (0, 0, 0, b)),
            pl.BlockSpec((N_RET_HEADS, c, c), lambda b: (0, 0, 0)),
            pl.BlockSpec((N_RET_HEADS, 2, c, 2 * hd), lambda b: (0, 0, 0, 0)),
            pl.BlockSpec((N_RET_HEADS, SUBLANES, 2 * hd), lambda b: (0, 0, 0)),
            pl.BlockSpec((1, D_RET), lambda b: (0, 0)),
        ],
        out_specs=pl.BlockSpec((1, length, D_RET), lambda b: (b, 0, 0)),
        out_shape=jax.ShapeDtypeStruct((bsz, length, D_RET), BF16),
        scratch_shapes=[
            pltpu.VMEM((n_ctx + n_chunks, hd, 2 * hd), F32),
            pltpu.VMEM((n_chunks, hd, 2 * hd), BF16),
            pltpu.VMEM((n_chunks, c, c), BF16),
        ],
        compiler_params=pltpu.CompilerParams(
            dimension_semantics=("parallel",), vmem_limit_bytes=VMEM_LIMIT),
        name="retention",
    )(proj, proj, proj, vt, proj, proj_c, vt_c, dm, rd, cd, gain)


def kernel(x, c, ctx, c_ctx, w_ada, b_ada, g_pre, g_post, w_in, w_fourier, decay_logit, ret_gn_gain, w_out):
    bsz, seq_len, _ = x.shape
    ctx_len = ctx.shape[1]
    assert w_ada.shape[0] == 1, "single layer only"

    mod = _adaln(c, c_ctx, w_ada[0], b_ada[0], g_pre[0], g_post[0])

    dm, rd, cd, ab = _tables(decay_logit[0], w_fourier[0])

    cos_t, sin_t = _rotary_tables(seq_len)

    gate_cols = list(range(COL_F_GATE, COL_Q, RET_HEAD_DIM)) + list(range(COL_R_GATE, D_IN, RET_HEAD_DIM))
    proj, vt = _inproj(x, mod, w_in[0], cos_t, sin_t, mod_row=None, col_block=D_IN, blocks=(0,),
                       rot_lo=COL_Q, rot_hi=COL_V, silu_cols=gate_cols, vt_lo=COL_V, tm=TOKEN_TILE)
    ctx_rows = bsz * ctx_len
    assert COL_V == COL_K + D_RET and COL_K % D_RET == 0
    proj_c, vt_c = _inproj(ctx.reshape(1, ctx_rows, D_MODEL), mod, w_in[0], cos_t, sin_t, mod_row=bsz,
                           col_block=D_RET, blocks=(COL_K // D_RET, COL_V // D_RET),
                           rot_lo=0, rot_hi=0, silu_cols=(), vt_lo=D_RET, tm=min(TOKEN_TILE, ctx_rows))
    proj_c = proj_c.reshape(bsz, ctx_len, 2 * D_RET)

    yr = _retention(proj, vt, proj_c, vt_c, dm, rd, cd, ret_gn_gain[0].reshape(1, D_RET))
    return _fourier_outproj(proj, yr, ab, *_seq_dft_matrices(seq_len), w_out[0], x, mod)
```

```python
import functools

import numpy as np
import jax
import jax.numpy as jnp
from jax import lax
from jax.experimental import pallas as pl
from jax.experimental.pallas import tpu as pltpu

D_MODEL = 1024
GRID_W = 64
D_FOURIER = 512
N_FOURIER_GROUPS = 4
FOURIER_GROUP = D_FOURIER // N_FOURIER_GROUPS
D_RET = 512
N_RET_HEADS = 4
RET_HEAD_DIM = D_RET // N_RET_HEADS
D_MIX = D_FOURIER + D_RET
D_IN = 2 * D_FOURIER + 4 * D_RET
RET_CHUNK = 128
ROPE_BASE = 10000.0
QK_SCALE = RET_HEAD_DIM ** -0.5
EPS = 1e-6

COL_F_IN = 0
COL_F_GATE = D_FOURIER
COL_Q = 2 * D_FOURIER
COL_K = COL_Q + D_RET
COL_V = COL_K + D_RET
COL_R_GATE = COL_V + D_RET

MXU_COLS = 256
FOLD_BLOCK = 256
FOURIER_ROW_BLOCK = 512
INPROJ_COL_CHUNK = 2 * MXU_COLS
OUTPROJ_ROW_BLOCK = 256
FUSED_STEPS = 2
SUBLANES = 8
BF16_TILE_ROWS = 16
MOD_SHIFT, MOD_MULT, MOD_GATE = 0, 1, 2
TOKEN_TILE = 1024
VMEM_LIMIT = 56 * 1024 * 1024

F32 = jnp.float32
BF16 = jnp.bfloat16


def _silu(v):
    return v * (0.5 * jnp.tanh(0.5 * v) + 0.5)


def _adaln_kernel(c_ref, cc_ref, w_ref, b_ref, gpre_ref, gpost_ref, dl_ref, wf_ref, cdft_ref, sdft_ref,
                  o_ref, dm_ref, rd_ref, cd_ref, ab_ref, s_ref):
    j = pl.program_id(0)
    pl.when(j == 0)(functools.partial(_tables_kernel, dl_ref, wf_ref, cdft_ref, sdft_ref,
                                      dm_ref, rd_ref, cd_ref, ab_ref))
    nb = c_ref.shape[0]
    s_ref[0:nb, :] = _silu(c_ref[...])
    s_ref[nb:, :] = jnp.broadcast_to(_silu(cc_ref[...]), (s_ref.shape[0] - nb, D_MODEL))
    m = jnp.dot(s_ref[...], w_ref[...], preferred_element_type=F32) + b_ref[...]
    add = jnp.where(j == 1, 1.0, 0.0)
    mul = jnp.where(j == 1, gpre_ref[...], jnp.where(j == 2, gpost_ref[...], 1.0))
    o_ref[...] = ((m + add) * mul)[:, None, :]


def _adaln(c, c_ctx, w_ada, b_ada, g_pre, g_post, decay_logit, w_fourier):
    c_chunk = RET_CHUNK
    idx = np.arange(FOURIER_GROUP)
    ang = 2.0 * np.pi * ((idx[:, None] * idx[None, :]) % FOURIER_GROUP) / FOURIER_GROUP
    cdft = jnp.asarray(np.cos(ang) / np.sqrt(FOURIER_GROUP), F32)
    sdft = jnp.asarray(np.sin(ang) / np.sqrt(FOURIER_GROUP), F32)
    table_shapes = (
        jax.ShapeDtypeStruct((N_RET_HEADS, c_chunk, c_chunk), F32),
        jax.ShapeDtypeStruct((N_RET_HEADS, 2, c_chunk, 2 * c_chunk), BF16),
        jax.ShapeDtypeStruct((N_RET_HEADS, SUBLANES, 2 * c_chunk), F32),
        jax.ShapeDtypeStruct((2, N_FOURIER_GROUPS // 2, 2 * FOURIER_GROUP, 2 * FOURIER_GROUP), BF16),
    )

    def whole(shape):
        return pl.BlockSpec(shape, lambda j: (0,) * len(shape))
    bsz = c.shape[0]
    n_out = w_ada.shape[1]
    bn = D_MODEL
    assert n_out == 3 * bn and bsz % SUBLANES == 0
    rows = bsz + SUBLANES
    const = lambda j: (0, 0)
    return pl.pallas_call(
        _adaln_kernel,
        grid=(n_out // bn,),
        in_specs=[
            pl.BlockSpec((bsz, D_MODEL), const),
            pl.BlockSpec((1, D_MODEL), const),
            pl.BlockSpec((D_MODEL, bn), lambda j: (0, j)),
            pl.BlockSpec((1, bn), lambda j: (0, j)),
            pl.BlockSpec((1, D_MODEL), const),
            pl.BlockSpec((1, D_MODEL), const),
            pl.BlockSpec(memory_space=pltpu.SMEM),
            whole(w_fourier.shape), whole(cdft.shape), whole(sdft.shape),
        ],
        out_specs=(pl.BlockSpec((rows, 1, bn), lambda j: (0, 0, j)),) + tuple(whole(t.shape) for t in table_shapes),
        out_shape=(jax.ShapeDtypeStruct((rows, 1, n_out), F32),) + table_shapes,
        scratch_shapes=[pltpu.VMEM((rows, D_MODEL), F32)],
        compiler_params=pltpu.CompilerParams(dimension_semantics=("arbitrary",)),
        name="adaln",
    )(c, c_ctx.reshape(1, D_MODEL), w_ada, b_ada.reshape(1, n_out), g_pre.reshape(1, D_MODEL),
      g_post.reshape(1, D_MODEL), decay_logit, w_fourier, cdft, sdft)


def _tables_kernel(dl_ref, wf_ref, cc_ref, sc_ref, dm_ref, rd_ref, cd_ref, ab_ref):
    c = RET_CHUNK
    n = lax.broadcasted_iota(jnp.int32, (c, c), 0).astype(F32)
    m = lax.broadcasted_iota(jnp.int32, (c, c), 1).astype(F32)
    diff = n - m

    def log_sigmoid(v):
        return jnp.minimum(v, 0.0) - jnp.log1p(jnp.exp(-jnp.abs(v)))

    for h in range(N_RET_HEADS):
        lg_f = log_sigmoid(jnp.full((c, c), dl_ref[0, h], F32))
        lg_b = log_sigmoid(jnp.full((c, c), dl_ref[1, h], F32))
        dm_ref[h] = QK_SCALE * jnp.where(diff >= 0, jnp.exp(lg_f * jnp.maximum(diff, 0.0)),
                                         jnp.exp(lg_b * jnp.maximum(-diff, 0.0)))
        rd_ref[h, 0, :, 0:c] = (QK_SCALE * jnp.exp(lg_f * (n + 1.0))).astype(BF16)
        rd_ref[h, 0, :, c:2 * c] = (QK_SCALE * jnp.exp(lg_b * (c - n))).astype(BF16)
        rd_ref[h, 1, :, 0:c] = jnp.exp(lg_f * (c - 1.0 - n)).astype(BF16)
        rd_ref[h, 1, :, c:2 * c] = jnp.exp(lg_b * n).astype(BF16)
        cd_ref[h, :, 0:c] = jnp.exp(lg_f[0:SUBLANES, :] * float(c))
        cd_ref[h, :, c:2 * c] = jnp.exp(lg_b[0:SUBLANES, :] * float(c))
    fg = FOURIER_GROUP
    ab_ref[...] = jnp.zeros(ab_ref.shape, BF16)
    for g in range(N_FOURIER_GROUPS):
        wf = wf_ref[g]
        a = jnp.dot(cc_ref[...], wf, preferred_element_type=F32, precision=lax.Precision.HIGHEST)
        b = jnp.dot(sc_ref[...], wf, preferred_element_type=F32, precision=lax.Precision.HIGHEST)
        pair, off = g // 2, (g % 2) * fg
        ab_ref[0, pair, off:off + fg, off:off + fg] = a.astype(BF16)
        ab_ref[1, pair, off:off + fg, off:off + fg] = b.astype(BF16)


def _inproj_kernel(x_ref, shift_ref, mult_ref, cos_ref, sin_ref, *refs, n_w, rot_lo, rot_hi, silu_cols):
    w_refs, o_ref, wb_ref = refs[:n_w], refs[n_w], refs[n_w + 1]
    n_out = wb_ref.shape[1]
    hd = RET_HEAD_DIM
    bn = INPROJ_COL_CHUNK

    @pl.when(jnp.logical_and(pl.program_id(0) == 0, pl.program_id(1) == 0))
    def _():
        for i, w_ref in enumerate(w_refs):
            width = w_ref.shape[1]
            for j in range(width // bn):
                wb_ref[:, i * width + j * bn:i * width + (j + 1) * bn] = w_ref[:, j * bn:(j + 1) * bn].astype(BF16)

    x = x_ref[0]
    ms = jnp.mean(x * x, axis=-1, keepdims=True)
    hb = (x * lax.rsqrt(ms + EPS) * mult_ref[0] + shift_ref[0]).astype(BF16)
    is_u1 = (lax.broadcasted_iota(jnp.int32, (x.shape[0], hd), 1) & (hd // 4)) == 0

    def epilogue_weight(j):
        lo = j * bn
        return (lo in silu_cols) + (rot_lo <= lo < rot_hi)

    for j in sorted(range(n_out // bn), key=epilogue_weight, reverse=True):
        p = jnp.dot(hb, wb_ref[:, j * bn:(j + 1) * bn], preferred_element_type=F32)
        for i in range(bn // hd):
            ph = p[:, i * hd:(i + 1) * hd]
            lo = j * bn + i * hd
            if lo in silu_cols:
                ph = _silu(ph)
            if rot_lo <= lo < rot_hi:
                partner = jnp.where(is_u1, pltpu.roll(ph, hd - hd // 4, axis=1), pltpu.roll(ph, hd // 4, axis=1))
                ph = ph * cos_ref[...] + partner * sin_ref[...]
            o_ref[0, :, lo:lo + hd] = ph.astype(BF16)


def _inproj(x, mod, w, cos_t, sin_t, *, mod_row, col_block, blocks, rot_lo, rot_hi, silu_cols, tm):
    bsz, length, _ = x.shape
    n_out = len(blocks) * col_block
    def mod_map(k):
        return (lambda b, t: (b, 0, k)) if mod_row is None else (lambda b, t: (mod_row, 0, k))
    rot_map = (lambda b, t: (t, 0)) if rot_hi > rot_lo else (lambda b, t: (0, 0))
    w_specs = [pl.BlockSpec((D_MODEL, col_block), functools.partial(lambda b, t, blk: (0, blk), blk=blk),
                            pipeline_mode=pl.Buffered(1)) for blk in blocks]
    return pl.pallas_call(
        functools.partial(_inproj_kernel, n_w=len(blocks), rot_lo=rot_lo, rot_hi=rot_hi,
                          silu_cols=frozenset(silu_cols)),
        grid=(bsz, length // tm),
        in_specs=[
            pl.BlockSpec((1, tm, D_MODEL), lambda b, t: (b, t, 0)),
            pl.BlockSpec((1, 1, D_MODEL), mod_map(MOD_SHIFT)),
            pl.BlockSpec((1, 1, D_MODEL), mod_map(MOD_MULT)),
            pl.BlockSpec((tm, RET_HEAD_DIM), rot_map),
            pl.BlockSpec((tm, RET_HEAD_DIM), rot_map),
        ] + w_specs,
        out_specs=pl.BlockSpec((1, tm, n_out), lambda b, t: (b, t, 0)),
        out_shape=jax.ShapeDtypeStruct((bsz, length, n_out), BF16),
        scratch_shapes=[pltpu.VMEM((D_MODEL, n_out), BF16)],
        compiler_params=pltpu.CompilerParams(
            dimension_semantics=("arbitrary", "arbitrary"), vmem_limit_bytes=VMEM_LIMIT),
        name="inproj",
    )(x, mod, mod, cos_t, sin_t, *([w] * len(blocks)))


def _rotary_tables(seq_len):
    n_freq = RET_HEAD_DIM // 4
    pos = np.arange(seq_len)
    row = (pos // GRID_W).astype(np.float64)
    col = (pos % GRID_W).astype(np.float64)
    inv_freq = ROPE_BASE ** (-np.arange(n_freq, dtype=np.float64) / n_freq)
    ang_r = row[:, None] * inv_freq[None, :]
    ang_c = col[:, None] * inv_freq[None, :]
    cos_t = np.concatenate([np.cos(ang_r), np.cos(ang_r), np.cos(ang_c), np.cos(ang_c)], axis=1)
    sin_t = np.concatenate([-np.sin(ang_r), np.sin(ang_r), -np.sin(ang_c), np.sin(ang_c)], axis=1)
    return jnp.asarray(cos_t, F32), jnp.asarray(sin_t, F32)


def _fourier_outproj_kernel(u_ref, g_ref, ab_ref, c_ref, s_ref, js_ref, yr_ref, w_ref, x_ref, gate_ref,
                            o_ref, ue_ref, uo_ref, pe_ref, qo_ref, gs_ref, mid_ref, yf_ref, wb_ref):
    b = pl.program_id(0)
    t = pl.program_id(1)
    n_batch = pl.num_programs(0) - 1
    slot = b % 2
    n = u_ref.shape[1]
    h = n // 2
    fb = FOLD_BLOCK
    nb = h // fb
    mb = FOURIER_ROW_BLOCK
    fg = FOURIER_GROUP
    inv_sqrt_n = 1.0 / np.sqrt(n)
    first_row = lax.broadcasted_iota(jnp.int32, (fb, D_FOURIER), 0) == 0

    @pl.when(jnp.logical_and(b == 0, t == 0))
    def _():
        wb_ref[...] = w_ref[...].astype(BF16)

    def channel_map(v, half):
        return jnp.concatenate(
            [jnp.dot(v[:, p * 2 * fg:(p + 1) * 2 * fg], ab_ref[half, p], preferred_element_type=F32)
             for p in range(N_FOURIER_GROUPS // 2)], axis=1)

    def fold_and_channel_map():
        js = js_ref[...]
        for i in range(nb):
            blk = slice(i * fb, (i + 1) * fb)
            lo = u_ref[0, blk, :].astype(F32)
            src = n - (i + 1) * fb
            r = jnp.dot(js, u_ref[0, src:src + fb, :], preferred_element_type=F32)
            if i > 0:
                edge = u_ref[0, n - i * fb:n - i * fb + BF16_TILE_ROWS, :][0:1, :]
                r = jnp.where(first_row, edge.astype(F32), r)
            ue_ref[blk, :] = (lo + r).astype(BF16)
            uo_ref[blk, :] = (lo - r).astype(BF16)
        for i in range(nb):
            blk = slice(i * fb, (i + 1) * fb)
            pe_ref[blk, :] = channel_map(ue_ref[blk, :], 0).astype(BF16)
            qo_ref[blk, :] = channel_map(uo_ref[blk, :], 1).astype(BF16)
        p_mid = channel_map(u_ref[0, h:h + BF16_TILE_ROWS, :], 0)[0:1, :] * inv_sqrt_n
        sign = (1 - 2 * (lax.broadcasted_iota(jnp.int32, (h, D_FOURIER), 0) & 1)).astype(F32)
        y_mid = jnp.sum(pe_ref[...].astype(F32) * sign, axis=0, keepdims=True) * inv_sqrt_n + p_mid
        mid_ref[0] = jnp.broadcast_to(p_mid, (SUBLANES, D_FOURIER))
        mid_ref[1] = jnp.broadcast_to(y_mid, (SUBLANES, D_FOURIER))

    def dft_block(i):
        blk = slice(i * mb, (i + 1) * mb)
        sign = (1 - 2 * (lax.broadcasted_iota(jnp.int32, (mb, D_FOURIER), 0) & 1)).astype(F32)
        yc = jnp.dot(c_ref[blk, :], pe_ref[...], preferred_element_type=F32) + sign * mid_ref[0, 0:1, :]
        ys = jnp.dot(s_ref[blk, :], qo_ref[...], preferred_element_type=F32)
        yf_ref[slot, blk, :] = ((yc - ys) * g_ref[0, blk, :].astype(F32)).astype(BF16)
        gs_ref[blk, :] = (yc + ys).astype(BF16)

    def reflect_upper_half():
        js = js_ref[...]
        for i in range(nb):
            src = h - (i + 1) * fb
            r = jnp.dot(js, gs_ref[src:src + fb, :], preferred_element_type=F32)
            if i == 0:
                edge = mid_ref[1, 0:1, :]
            else:
                edge = gs_ref[h - i * fb:h - i * fb + BF16_TILE_ROWS, :][0:1, :].astype(F32)
            r = jnp.where(first_row, edge, r)
            rows = slice(h + i * fb, h + (i + 1) * fb)
            yf_ref[slot, rows, :] = (r * g_ref[0, rows, :].astype(F32)).astype(BF16)

    phases = [fold_and_channel_map] + [functools.partial(dft_block, i) for i in range(h // mb)] + [reflect_upper_half]
    assert FUSED_STEPS == 2
    bounds = (0, 1, len(phases))

    def fourier_part(k):
        for phase in phases[bounds[k]:bounds[k + 1]]:
            phase()

    def outproj_tile():
        tm = x_ref.shape[1]
        rb = OUTPROJ_ROW_BLOCK
        for r in range(tm // rb):
            rows = slice(r * rb, (r + 1) * rb)
            yf_rows = pl.ds(pl.multiple_of(t * tm, tm) + r * rb, rb)
            y = jnp.dot(yf_ref[1 - slot, yf_rows, :], wb_ref[0:D_FOURIER, :], preferred_element_type=F32)
            y += jnp.dot(yr_ref[0, rows, :], wb_ref[D_FOURIER:D_MIX, :], preferred_element_type=F32)
            ms = jnp.mean(y * y, axis=-1, keepdims=True)
            o_ref[0, rows, :] = x_ref[0, rows, :] + y * lax.rsqrt(ms + EPS) * gate_ref[0]

    interior = jnp.logical_and(b > 0, b < n_batch)
    for k in range(FUSED_STEPS):
        def both(k=k):
            fourier_part(k)
            outproj_tile()
        pl.when(jnp.logical_and(interior, t == k))(both)
        pl.when(jnp.logical_and(b == 0, t == k))(functools.partial(fourier_part, k))
    pl.when(b == n_batch)(outproj_tile)


def _fourier_outproj(proj, yr, ab, c_mat, s_mat, js, w_out, x, mod):
    bsz, length, _ = x.shape
    h = length // 2
    assert h % FOLD_BLOCK == 0 and h % FOURIER_ROW_BLOCK == 0 and length % FUSED_STEPS == 0
    tm = length // FUSED_STEPS
    assert tm % OUTPROJ_ROW_BLOCK == 0

    def cur(col):
        return lambda b, t: (jnp.minimum(b, bsz - 1), 0, col)

    def prev_tile(b, t):
        return (jnp.maximum(b - 1, 0), jnp.where(b == 0, 0, t), 0)

    const2 = lambda b, t: (0, 0)
    resident = pl.Buffered(1)
    return pl.pallas_call(
        _fourier_outproj_kernel,
        grid=(bsz + 1, FUSED_STEPS),
        in_specs=[
            pl.BlockSpec((1, length, D_FOURIER), cur(COL_F_IN // D_FOURIER)),
            pl.BlockSpec((1, length, D_FOURIER), cur(COL_F_GATE // D_FOURIER)),
            pl.BlockSpec((2, N_FOURIER_GROUPS // 2, 2 * FOURIER_GROUP, 2 * FOURIER_GROUP), lambda b, t: (0, 0, 0, 0),
                         pipeline_mode=resident),
            pl.BlockSpec((h, h), const2, pipeline_mode=resident),
            pl.BlockSpec((h, h), const2, pipeline_mode=resident),
            pl.BlockSpec((FOLD_BLOCK, FOLD_BLOCK), const2, pipeline_mode=resident),
            pl.BlockSpec((1, tm, D_RET), prev_tile),
            pl.BlockSpec((D_MIX, D_MODEL), const2, pipeline_mode=resident),
            pl.BlockSpec((1, tm, D_MODEL), prev_tile),
            pl.BlockSpec((1, 1, D_MODEL), lambda b, t: (jnp.maximum(b - 1, 0), 0, MOD_GATE)),
        ],
        out_specs=pl.BlockSpec((1, tm, D_MODEL), prev_tile),
        out_shape=jax.ShapeDtypeStruct((bsz, length, D_MODEL), F32),
        scratch_shapes=[
            pltpu.VMEM((h, D_FOURIER), BF16),
            pltpu.VMEM((h, D_FOURIER), BF16),
            pltpu.VMEM((h, D_FOURIER), BF16),
            pltpu.VMEM((h, D_FOURIER), BF16),
            pltpu.VMEM((h, D_FOURIER), BF16),
            pltpu.VMEM((2, SUBLANES, D_FOURIER), F32),
            pltpu.VMEM((2, length, D_FOURIER), BF16),
            pltpu.VMEM((D_MIX, D_MODEL), BF16),
        ],
        compiler_params=pltpu.CompilerParams(
            dimension_semantics=("arbitrary", "arbitrary"), vmem_limit_bytes=VMEM_LIMIT),
        name="fourier_outproj",
    )(proj, proj, ab, c_mat, s_mat, js, yr, w_out, x, mod)


def _seq_dft_matrices(length):
    h = length // 2
    idx = np.arange(h)
    ang = 2.0 * np.pi * ((idx[:, None] * idx[None, :]) % length) / length
    scale = 1.0 / np.sqrt(length)
    c_mat = jnp.asarray((np.cos(ang) * scale).astype(np.float32)).astype(BF16)
    s_mat = jnp.asarray((np.sin(ang) * scale).astype(np.float32)).astype(BF16)
    r = np.arange(1, FOLD_BLOCK)
    js = np.zeros((FOLD_BLOCK, FOLD_BLOCK), np.float32)
    js[r, FOLD_BLOCK - r] = 1.0
    return c_mat, s_mat, jnp.asarray(js).astype(BF16)


def _ret_kernel(q_ref, k_ref, v_ref, rg_ref, kc_ref, vc_ref, dm_ref, rd_ref, cd_ref, gain_ref,
                o_ref, u_ref, st_ref, a_ref):
    c = RET_CHUNK
    hd = RET_HEAD_DIM
    n_chunks = q_ref.shape[1] // c
    n_ctx = kc_ref.shape[1] // c

    for h in range(N_RET_HEADS):
        cols = slice(h * hd, (h + 1) * hd)
        dm = dm_ref[h]
        q_dec = rd_ref[h, 0]
        k_dec = rd_ref[h, 1]
        c_dec_f = cd_ref[h, 0:1, 0:hd]
        c_dec_b = cd_ref[h, 0:1, hd:2 * hd]
        gain = gain_ref[:, cols]

        def both_decays(t, dec):
            return jnp.concatenate([t * dec[:, 0:hd], t * dec[:, hd:2 * hd]], axis=1)

        def kv_outer(k, v):
            return lax.dot_general(v, both_decays(k, k_dec), (((0,), (0,)), ((), ())), preferred_element_type=F32)

        for j in range(n_ctx):
            rows = slice(j * c, (j + 1) * c)
            u_ref[j] = kv_outer(kc_ref[0, rows, cols], vc_ref[0, rows, cols])
        for j in range(n_chunks):
            rows = slice(j * c, (j + 1) * c)
            u_ref[n_ctx + j] = kv_outer(k_ref[0, rows, cols], v_ref[0, rows, cols])

        sf = jnp.zeros((hd, hd), F32)
        for j in range(n_ctx + n_chunks - 1):
            sf = c_dec_f * sf + u_ref[j, :, 0:hd]
            if j + 1 >= n_ctx:
                st_ref[j + 1 - n_ctx, :, 0:hd] = sf.astype(BF16)
        sb = jnp.zeros((hd, hd), F32)
        for j in reversed(range(n_ctx)):
            sb = c_dec_b * sb + u_ref[j, :, hd:2 * hd]
        for j in reversed(range(n_chunks)):
            st_ref[j, :, hd:2 * hd] = sb.astype(BF16)
            if j > 0:
                sb = c_dec_b * sb + u_ref[n_ctx + j, :, hd:2 * hd]

        for j in range(n_chunks):
            rows = slice(j * c, (j + 1) * c)
            s = lax.dot_general(q_ref[0, rows, cols], k_ref[0, rows, cols], (((1,), (1,)), ((), ())),
                                preferred_element_type=F32)
            a_ref[j] = (s * dm).astype(BF16)

        for j in range(n_chunks):
            rows = slice(j * c, (j + 1) * c)
            o = jnp.dot(a_ref[j], v_ref[0, rows, cols], preferred_element_type=F32)
            o += lax.dot_general(both_decays(q_ref[0, rows, cols], q_dec), st_ref[j], (((1,), (1,)), ((), ())),
                                 preferred_element_type=F32)
            mu = jnp.mean(o, axis=-1, keepdims=True)
            d = o - mu
            var = jnp.mean(d * d, axis=-1, keepdims=True)
            on = d * lax.rsqrt(var + EPS)
            o_ref[0, rows, cols] = (on * gain * rg_ref[0, rows, cols].astype(F32)).astype(BF16)


def _retention(proj, proj_c, dm, rd, cd, gain):
    bsz, length, _ = proj.shape
    ctx_len = proj_c.shape[1]
    hd = RET_HEAD_DIM
    c = RET_CHUNK
    n_chunks = length // c
    n_ctx = ctx_len // c
    assert n_ctx >= 1

    def col(base):
        return lambda b: (b, 0, base // D_RET)

    return pl.pallas_call(
        _ret_kernel,
        grid=(bsz,),
        in_specs=[
            pl.BlockSpec((1, length, D_RET), col(COL_Q)),
            pl.BlockSpec((1, length, D_RET), col(COL_K)),
            pl.BlockSpec((1, length, D_RET), col(COL_V)),
            pl.BlockSpec((1, length, D_RET), col(COL_R_GATE)),
            pl.BlockSpec((1, ctx_len, D_RET), col(0)),
            pl.BlockSpec((1, ctx_len, D_RET), col(D_RET)),
            pl.BlockSpec((N_RET_HEADS, c, c), lambda b: (0, 0, 0)),
            pl.BlockSpec((N_RET_HEADS, 2, c, 2 * hd), lambda b: (0, 0, 0, 0)),
            pl.BlockSpec((N_RET_HEADS, SUBLANES, 2 * hd), lambda b: (0, 0, 0)),
            pl.BlockSpec((1, D_RET), lambda b: (0, 0)),
        ],
        out_specs=pl.BlockSpec((1, length, D_RET), lambda b: (b, 0, 0)),
        out_shape=jax.ShapeDtypeStruct((bsz, length, D_RET), BF16),
        scratch_shapes=[
            pltpu.VMEM((n_ctx + n_chunks, hd, 2 * hd), F32),
            pltpu.VMEM((n_chunks, hd, 2 * hd), BF16),
            pltpu.VMEM((n_chunks, c, c), BF16),
        ],
        compiler_params=pltpu.CompilerParams(
            dimension_semantics=("parallel",), vmem_limit_bytes=VMEM_LIMIT),
        name="retention",
    )(proj, proj, proj, proj, proj_c, proj_c, dm, rd, cd, gain)


def kernel(x, c, ctx, c_ctx, w_ada, b_ada, g_pre, g_post, w_in, w_fourier, decay_logit, ret_gn_gain, w_out):
    bsz, seq_len, _ = x.shape
    ctx_len = ctx.shape[1]
    assert w_ada.shape[0] == 1, "single layer only"

    mod, dm, rd, cd, ab = _adaln(c, c_ctx, w_ada[0], b_ada[0], g_pre[0], g_post[0], decay_logit[0], w_fourier[0])
    cos_t, sin_t = _rotary_tables(seq_len)

    gate_cols = list(range(COL_F_GATE, COL_Q, RET_HEAD_DIM)) + list(range(COL_R_GATE, D_IN, RET_HEAD_DIM))
    proj = _inproj(x, mod, w_in[0], cos_t, sin_t, mod_row=None, col_block=D_IN, blocks=(0,),
                   rot_lo=COL_Q, rot_hi=COL_V, silu_cols=gate_cols, tm=TOKEN_TILE)
    ctx_rows = bsz * ctx_len
    assert COL_V == COL_K + D_RET and COL_K % D_RET == 0
    proj_c = _inproj(ctx.reshape(1, ctx_rows, D_MODEL), mod, w_in[0], cos_t, sin_t, mod_row=bsz,
                     col_block=D_RET, blocks=(COL_K // D_RET, COL_V // D_RET),
                     rot_lo=0, rot_hi=0, silu_cols=(), tm=min(TOKEN_TILE, ctx_rows))
    proj_c = proj_c.reshape(bsz, ctx_len, 2 * D_RET)

    yr = _retention(proj, proj_c, dm, rd, cd, ret_gn_gain[0].reshape(1, D_RET))
    return _fourier_outproj(proj, yr, ab, *_seq_dft_matrices(seq_len), w_out[0], x, mod)
```

```python
import functools

import numpy as np
import jax
import jax.numpy as jnp
from jax import lax
from jax.experimental import pallas as pl
from jax.experimental.pallas import tpu as pltpu

D_MODEL = 1024
GRID_W = 64
D_FOURIER = 512
N_FOURIER_GROUPS = 4
FOURIER_GROUP = D_FOURIER // N_FOURIER_GROUPS
D_RET = 512
N_RET_HEADS = 4
RET_HEAD_DIM = D_RET // N_RET_HEADS
D_MIX = D_FOURIER + D_RET
D_IN = 2 * D_FOURIER + 4 * D_RET
RET_CHUNK = 128
ROPE_BASE = 10000.0
QK_SCALE = RET_HEAD_DIM ** -0.5
EPS = 1e-6

COL_F_IN = 0
COL_F_GATE = D_FOURIER
COL_Q = 2 * D_FOURIER
COL_K = COL_Q + D_RET
COL_V = COL_K + D_RET
COL_R_GATE = COL_V + D_RET

MXU_COLS = 256
FOLD_BLOCK = 256
FOURIER_ROW_BLOCK = 512
INPROJ_COL_CHUNK = 2 * MXU_COLS
OUTPROJ_ROW_BLOCK = 256
FUSED_STEPS = 2
RET_BATCH = 2
SUBLANES = 8
BF16_TILE_ROWS = 16
MOD_SHIFT, MOD_MULT, MOD_GATE = 0, 1, 2
TOKEN_TILE = 1024
VMEM_LIMIT = 56 * 1024 * 1024

F32 = jnp.float32
BF16 = jnp.bfloat16


def _silu(v):
    return v * (0.5 * jnp.tanh(0.5 * v) + 0.5)


def _adaln_kernel(c_ref, cc_ref, w_ref, b_ref, gpre_ref, gpost_ref, dl_ref, wf_ref, cdft_ref, sdft_ref,
                  o_ref, dm_ref, rd_ref, cd_ref, ab_ref, s_ref):
    j = pl.program_id(0)
    pl.when(j == 0)(functools.partial(_tables_kernel, dl_ref, wf_ref, cdft_ref, sdft_ref,
                                      dm_ref, rd_ref, cd_ref, ab_ref))
    nb = c_ref.shape[0]
    s_ref[0:nb, :] = _silu(c_ref[...])
    s_ref[nb:, :] = jnp.broadcast_to(_silu(cc_ref[...]), (s_ref.shape[0] - nb, D_MODEL))
    m = jnp.dot(s_ref[...], w_ref[...], preferred_element_type=F32) + b_ref[...]
    add = jnp.where(j == 1, 1.0, 0.0)
    mul = jnp.where(j == 1, gpre_ref[...], jnp.where(j == 2, gpost_ref[...], 1.0))
    o_ref[...] = ((m + add) * mul)[:, None, :]


def _adaln(c, c_ctx, w_ada, b_ada, g_pre, g_post, decay_logit, w_fourier):
    c_chunk = RET_CHUNK
    idx = np.arange(FOURIER_GROUP)
    ang = 2.0 * np.pi * ((idx[:, None] * idx[None, :]) % FOURIER_GROUP) / FOURIER_GROUP
    cdft = jnp.asarray(np.cos(ang) / np.sqrt(FOURIER_GROUP), F32)
    sdft = jnp.asarray(np.sin(ang) / np.sqrt(FOURIER_GROUP), F32)
    table_shapes = (
        jax.ShapeDtypeStruct((N_RET_HEADS, c_chunk, c_chunk), F32),
        jax.ShapeDtypeStruct((N_RET_HEADS, 2, c_chunk, 2 * c_chunk), BF16),
        jax.ShapeDtypeStruct((N_RET_HEADS, SUBLANES, 2 * c_chunk), F32),
        jax.ShapeDtypeStruct((2, N_FOURIER_GROUPS // 2, 2 * FOURIER_GROUP, 2 * FOURIER_GROUP), BF16),
    )

    def whole(shape):
        return pl.BlockSpec(shape, lambda j: (0,) * len(shape))
    bsz = c.shape[0]
    n_out = w_ada.shape[1]
    bn = D_MODEL
    assert n_out == 3 * bn and bsz % SUBLANES == 0
    rows = bsz + SUBLANES
    const = lambda j: (0, 0)
    return pl.pallas_call(
        _adaln_kernel,
        grid=(n_out // bn,),
        in_specs=[
            pl.BlockSpec((bsz, D_MODEL), const),
            pl.BlockSpec((1, D_MODEL), const),
            pl.BlockSpec((D_MODEL, bn), lambda j: (0, j)),
            pl.BlockSpec((1, bn), lambda j: (0, j)),
            pl.BlockSpec((1, D_MODEL), const),
            pl.BlockSpec((1, D_MODEL), const),
            pl.BlockSpec(memory_space=pltpu.SMEM),
            whole(w_fourier.shape), whole(cdft.shape), whole(sdft.shape),
        ],
        out_specs=(pl.BlockSpec((rows, 1, bn), lambda j: (0, 0, j)),) + tuple(whole(t.shape) for t in table_shapes),
        out_shape=(jax.ShapeDtypeStruct((rows, 1, n_out), F32),) + table_shapes,
        scratch_shapes=[pltpu.VMEM((rows, D_MODEL), F32)],
        compiler_params=pltpu.CompilerParams(dimension_semantics=("arbitrary",)),
        name="adaln",
    )(c, c_ctx.reshape(1, D_MODEL), w_ada, b_ada.reshape(1, n_out), g_pre.reshape(1, D_MODEL),
      g_post.reshape(1, D_MODEL), decay_logit, w_fourier, cdft, sdft)


def _tables_kernel(dl_ref, wf_ref, cc_ref, sc_ref, dm_ref, rd_ref, cd_ref, ab_ref):
    c = RET_CHUNK
    n = lax.broadcasted_iota(jnp.int32, (c, c), 0).astype(F32)
    m = lax.broadcasted_iota(jnp.int32, (c, c), 1).astype(F32)
    diff = n - m

    def log_sigmoid(v):
        return jnp.minimum(v, 0.0) - jnp.log1p(jnp.exp(-jnp.abs(v)))

    for h in range(N_RET_HEADS):
        lg_f = log_sigmoid(jnp.full((c, c), dl_ref[0, h], F32))
        lg_b = log_sigmoid(jnp.full((c, c), dl_ref[1, h], F32))
        dm_ref[h] = QK_SCALE * jnp.where(diff >= 0, jnp.exp(lg_f * jnp.maximum(diff, 0.0)),
                                         jnp.exp(lg_b * jnp.maximum(-diff, 0.0)))
        rd_ref[h, 0, :, 0:c] = (QK_SCALE * jnp.exp(lg_f * (n + 1.0))).astype(BF16)
        rd_ref[h, 0, :, c:2 * c] = (QK_SCALE * jnp.exp(lg_b * (c - n))).astype(BF16)
        rd_ref[h, 1, :, 0:c] = jnp.exp(lg_f * (c - 1.0 - n)).astype(BF16)
        rd_ref[h, 1, :, c:2 * c] = jnp.exp(lg_b * n).astype(BF16)
        cd_ref[h, :, 0:c] = jnp.exp(lg_f[0:SUBLANES, :] * float(c))
        cd_ref[h, :, c:2 * c] = jnp.exp(lg_b[0:SUBLANES, :] * float(c))
    fg = FOURIER_GROUP
    ab_ref[...] = jnp.zeros(ab_ref.shape, BF16)
    for g in range(N_FOURIER_GROUPS):
        wf = wf_ref[g]
        a = jnp.dot(cc_ref[...], wf, preferred_element_type=F32, precision=lax.Precision.HIGHEST)
        b = jnp.dot(sc_ref[...], wf, preferred_element_type=F32, precision=lax.Precision.HIGHEST)
        pair, off = g // 2, (g % 2) * fg
        ab_ref[0, pair, off:off + fg, off:off + fg] = a.astype(BF16)
        ab_ref[1, pair, off:off + fg, off:off + fg] = b.astype(BF16)


def _inproj_kernel(x_ref, shift_ref, mult_ref, cos_ref, sin_ref, *refs, n_w, rot_lo, rot_hi, silu_cols):
    w_refs, o_ref, wb_ref = refs[:n_w], refs[n_w], refs[n_w + 1]
    n_out = wb_ref.shape[1]
    hd = RET_HEAD_DIM
    bn = INPROJ_COL_CHUNK

    @pl.when(jnp.logical_and(pl.program_id(0) == 0, pl.program_id(1) == 0))
    def _():
        for i, w_ref in enumerate(w_refs):
            width = w_ref.shape[1]
            for j in range(width // bn):
                wb_ref[:, i * width + j * bn:i * width + (j + 1) * bn] = w_ref[:, j * bn:(j + 1) * bn].astype(BF16)

    x = x_ref[0]
    ms = jnp.mean(x * x, axis=-1, keepdims=True)
    hb = (x * lax.rsqrt(ms + EPS) * mult_ref[0] + shift_ref[0]).astype(BF16)
    is_u1 = (lax.broadcasted_iota(jnp.int32, (x.shape[0], hd), 1) & (hd // 4)) == 0

    def epilogue_weight(j):
        lo = j * bn
        return (lo in silu_cols) + (rot_lo <= lo < rot_hi)

    for j in sorted(range(n_out // bn), key=epilogue_weight, reverse=True):
        p = jnp.dot(hb, wb_ref[:, j * bn:(j + 1) * bn], preferred_element_type=F32)
        for i in range(bn // hd):
            ph = p[:, i * hd:(i + 1) * hd]
            lo = j * bn + i * hd
            if lo in silu_cols:
                ph = _silu(ph)
            if rot_lo <= lo < rot_hi:
                partner = jnp.where(is_u1, pltpu.roll(ph, hd - hd // 4, axis=1), pltpu.roll(ph, hd // 4, axis=1))
                ph = ph * cos_ref[...] + partner * sin_ref[...]
            o_ref[0, :, lo:lo + hd] = ph.astype(BF16)


def _inproj(x, mod, w, cos_t, sin_t, *, mod_row, col_block, blocks, rot_lo, rot_hi, silu_cols, tm):
    bsz, length, _ = x.shape
    n_out = len(blocks) * col_block
    def mod_map(k):
        return (lambda b, t: (b, 0, k)) if mod_row is None else (lambda b, t: (mod_row, 0, k))
    rot_map = (lambda b, t: (t, 0)) if rot_hi > rot_lo else (lambda b, t: (0, 0))
    w_specs = [pl.BlockSpec((D_MODEL, col_block), functools.partial(lambda b, t, blk: (0, blk), blk=blk),
                            pipeline_mode=pl.Buffered(1)) for blk in blocks]
    return pl.pallas_call(
        functools.partial(_inproj_kernel, n_w=len(blocks), rot_lo=rot_lo, rot_hi=rot_hi,
                          silu_cols=frozenset(silu_cols)),
        grid=(bsz, length // tm),
        in_specs=[
            pl.BlockSpec((1, tm, D_MODEL), lambda b, t: (b, t, 0)),
            pl.BlockSpec((1, 1, D_MODEL), mod_map(MOD_SHIFT)),
            pl.BlockSpec((1, 1, D_MODEL), mod_map(MOD_MULT)),
            pl.BlockSpec((tm, RET_HEAD_DIM), rot_map),
            pl.BlockSpec((tm, RET_HEAD_DIM), rot_map),
        ] + w_specs,
        out_specs=pl.BlockSpec((1, tm, n_out), lambda b, t: (b, t, 0)),
        out_shape=jax.ShapeDtypeStruct((bsz, length, n_out), BF16),
        scratch_shapes=[pltpu.VMEM((D_MODEL, n_out), BF16)],
        compiler_params=pltpu.CompilerParams(
            dimension_semantics=("arbitrary", "arbitrary"), vmem_limit_bytes=VMEM_LIMIT),
        name="inproj",
    )(x, mod, mod, cos_t, sin_t, *([w] * len(blocks)))


def _rotary_tables(seq_len):
    n_freq = RET_HEAD_DIM // 4
    pos = np.arange(seq_len)
    row = (pos // GRID_W).astype(np.float64)
    col = (pos % GRID_W).astype(np.float64)
    inv_freq = ROPE_BASE ** (-np.arange(n_freq, dtype=np.float64) / n_freq)
    ang_r = row[:, None] * inv_freq[None, :]
    ang_c = col[:, None] * inv_freq[None, :]
    cos_t = np.concatenate([np.cos(ang_r), np.cos(ang_r), np.cos(ang_c), np.cos(ang_c)], axis=1)
    sin_t = np.concatenate([-np.sin(ang_r), np.sin(ang_r), -np.sin(ang_c), np.sin(ang_c)], axis=1)
    return jnp.asarray(cos_t, F32), jnp.asarray(sin_t, F32)


def _fourier_outproj_kernel(u_ref, g_ref, ab_ref, c_ref, s_ref, js_ref, yr_ref, w_ref, x_ref, gate_ref,
                            o_ref, ue_ref, uo_ref, pe_ref, qo_ref, gs_ref, mid_ref, yf_ref, wb_ref):
    b = pl.program_id(0)
    t = pl.program_id(1)
    n_batch = pl.num_programs(0) - 1
    slot = b % 2
    n = u_ref.shape[1]
    h = n // 2
    fb = FOLD_BLOCK
    nb = h // fb
    mb = FOURIER_ROW_BLOCK
    fg = FOURIER_GROUP
    inv_sqrt_n = 1.0 / np.sqrt(n)
    first_row = lax.broadcasted_iota(jnp.int32, (fb, D_FOURIER), 0) == 0

    @pl.when(jnp.logical_and(b == 0, t == 0))
    def _():
        wb_ref[...] = w_ref[...].astype(BF16)

    def channel_map(v, half):
        return jnp.concatenate(
            [jnp.dot(v[:, p * 2 * fg:(p + 1) * 2 * fg], ab_ref[half, p], preferred_element_type=F32)
             for p in range(N_FOURIER_GROUPS // 2)], axis=1)

    def fold_and_channel_map():
        js = js_ref[...]
        for i in range(nb):
            blk = slice(i * fb, (i + 1) * fb)
            lo = u_ref[0, blk, :].astype(F32)
            src = n - (i + 1) * fb
            r = jnp.dot(js, u_ref[0, src:src + fb, :], preferred_element_type=F32)
            if i > 0:
                edge = u_ref[0, n - i * fb:n - i * fb + BF16_TILE_ROWS, :][0:1, :]
                r = jnp.where(first_row, edge.astype(F32), r)
            ue_ref[blk, :] = (lo + r).astype(BF16)
            uo_ref[blk, :] = (lo - r).astype(BF16)
        for i in range(nb):
            blk = slice(i * fb, (i + 1) * fb)
            pe_ref[blk, :] = channel_map(ue_ref[blk, :], 0).astype(BF16)
            qo_ref[blk, :] = channel_map(uo_ref[blk, :], 1).astype(BF16)
        p_mid = channel_map(u_ref[0, h:h + BF16_TILE_ROWS, :], 0)[0:1, :] * inv_sqrt_n
        sign = (1 - 2 * (lax.broadcasted_iota(jnp.int32, (h, D_FOURIER), 0) & 1)).astype(F32)
        y_mid = jnp.sum(pe_ref[...].astype(F32) * sign, axis=0, keepdims=True) * inv_sqrt_n + p_mid
        mid_ref[0] = jnp.broadcast_to(p_mid, (SUBLANES, D_FOURIER))
        mid_ref[1] = jnp.broadcast_to(y_mid, (SUBLANES, D_FOURIER))

    def dft_block(i):
        blk = slice(i * mb, (i + 1) * mb)
        sign = (1 - 2 * (lax.broadcasted_iota(jnp.int32, (mb, D_FOURIER), 0) & 1)).astype(F32)
        yc = jnp.dot(c_ref[blk, :], pe_ref[...], preferred_element_type=F32) + sign * mid_ref[0, 0:1, :]
        ys = jnp.dot(s_ref[blk, :], qo_ref[...], preferred_element_type=F32)
        yf_ref[slot, blk, :] = ((yc - ys) * g_ref[0, blk, :].astype(F32)).astype(BF16)
        gs_ref[blk, :] = (yc + ys).astype(BF16)

    def reflect_upper_half():
        js = js_ref[...]
        for i in range(nb):
            src = h - (i + 1) * fb
            r = jnp.dot(js, gs_ref[src:src + fb, :], preferred_element_type=F32)
            if i == 0:
                edge = mid_ref[1, 0:1, :]
            else:
                edge = gs_ref[h - i * fb:h - i * fb + BF16_TILE_ROWS, :][0:1, :].astype(F32)
            r = jnp.where(first_row, edge, r)
            rows = slice(h + i * fb, h + (i + 1) * fb)
            yf_ref[slot, rows, :] = (r * g_ref[0, rows, :].astype(F32)).astype(BF16)

    phases = [fold_and_channel_map] + [functools.partial(dft_block, i) for i in range(h // mb)] + [reflect_upper_half]
    assert FUSED_STEPS == 2
    bounds = (0, 1, len(phases))

    def fourier_part(k):
        for phase in phases[bounds[k]:bounds[k + 1]]:
            phase()

    def outproj_tile():
        tm = x_ref.shape[1]
        rb = OUTPROJ_ROW_BLOCK
        for r in range(tm // rb):
            rows = slice(r * rb, (r + 1) * rb)
            yf_rows = pl.ds(pl.multiple_of(t * tm, tm) + r * rb, rb)
            y = jnp.dot(yf_ref[1 - slot, yf_rows, :], wb_ref[0:D_FOURIER, :], preferred_element_type=F32)
            y += jnp.dot(yr_ref[0, rows, :], wb_ref[D_FOURIER:D_MIX, :], preferred_element_type=F32)
            ms = jnp.mean(y * y, axis=-1, keepdims=True)
            o_ref[0, rows, :] = x_ref[0, rows, :] + y * lax.rsqrt(ms + EPS) * gate_ref[0]

    interior = jnp.logical_and(b > 0, b < n_batch)
    for k in range(FUSED_STEPS):
        def both(k=k):
            fourier_part(k)
            outproj_tile()
        pl.when(jnp.logical_and(interior, t == k))(both)
        pl.when(jnp.logical_and(b == 0, t == k))(functools.partial(fourier_part, k))
    pl.when(b == n_batch)(outproj_tile)


def _fourier_outproj(proj, yr, ab, c_mat, s_mat, js, w_out, x, mod):
    bsz, length, _ = x.shape
    h = length // 2
    assert h % FOLD_BLOCK == 0 and h % FOURIER_ROW_BLOCK == 0 and length % FUSED_STEPS == 0
    tm = length // FUSED_STEPS
    assert tm % OUTPROJ_ROW_BLOCK == 0

    def cur(col):
        return lambda b, t: (jnp.minimum(b, bsz - 1), 0, col)

    def prev_tile(b, t):
        return (jnp.maximum(b - 1, 0), jnp.where(b == 0, 0, t), 0)

    const2 = lambda b, t: (0, 0)
    resident = pl.Buffered(1)
    return pl.pallas_call(
        _fourier_outproj_kernel,
        grid=(bsz + 1, FUSED_STEPS),
        in_specs=[
            pl.BlockSpec((1, length, D_FOURIER), cur(COL_F_IN // D_FOURIER)),
            pl.BlockSpec((1, length, D_FOURIER), cur(COL_F_GATE // D_FOURIER)),
            pl.BlockSpec((2, N_FOURIER_GROUPS // 2, 2 * FOURIER_GROUP, 2 * FOURIER_GROUP), lambda b, t: (0, 0, 0, 0),
                         pipeline_mode=resident),
            pl.BlockSpec((h, h), const2, pipeline_mode=resident),
            pl.BlockSpec((h, h), const2, pipeline_mode=resident),
            pl.BlockSpec((FOLD_BLOCK, FOLD_BLOCK), const2, pipeline_mode=resident),
            pl.BlockSpec((1, tm, D_RET), prev_tile),
            pl.BlockSpec((D_MIX, D_MODEL), const2, pipeline_mode=resident),
            pl.BlockSpec((1, tm, D_MODEL), prev_tile),
            pl.BlockSpec((1, 1, D_MODEL), lambda b, t: (jnp.maximum(b - 1, 0), 0, MOD_GATE)),
        ],
        out_specs=pl.BlockSpec((1, tm, D_MODEL), prev_tile),
        out_shape=jax.ShapeDtypeStruct((bsz, length, D_MODEL), F32),
        scratch_shapes=[
            pltpu.VMEM((h, D_FOURIER), BF16),
            pltpu.VMEM((h, D_FOURIER), BF16),
            pltpu.VMEM((h, D_FOURIER), BF16),
            pltpu.VMEM((h, D_FOURIER), BF16),
            pltpu.VMEM((h, D_FOURIER), BF16),
            pltpu.VMEM((2, SUBLANES, D_FOURIER), F32),
            pltpu.VMEM((2, length, D_FOURIER), BF16),
            pltpu.VMEM((D_MIX, D_MODEL), BF16),
        ],
        compiler_params=pltpu.CompilerParams(
            dimension_semantics=("arbitrary", "arbitrary"), vmem_limit_bytes=VMEM_LIMIT),
        name="fourier_outproj",
    )(proj, proj, ab, c_mat, s_mat, js, yr, w_out, x, mod)


def _seq_dft_matrices(length):
    h = length // 2
    idx = np.arange(h)
    ang = 2.0 * np.pi * ((idx[:, None] * idx[None, :]) % length) / length
    scale = 1.0 / np.sqrt(length)
    c_mat = jnp.asarray((np.cos(ang) * scale).astype(np.float32)).astype(BF16)
    s_mat = jnp.asarray((np.sin(ang) * scale).astype(np.float32)).astype(BF16)
    r = np.arange(1, FOLD_BLOCK)
    js = np.zeros((FOLD_BLOCK, FOLD_BLOCK), np.float32)
    js[r, FOLD_BLOCK - r] = 1.0
    return c_mat, s_mat, jnp.asarray(js).astype(BF16)


def _ret_kernel(q_ref, k_ref, v_ref, rg_ref, kc_ref, vc_ref, dm_ref, rd_ref, cd_ref, gain_ref,
                o_ref, u_ref, st_ref, a_ref):
    for bi in range(q_ref.shape[0]):
        one = pl.ds(bi, 1)
        _ret_batch_element(q_ref.at[one], k_ref.at[one], v_ref.at[one], rg_ref.at[one], kc_ref.at[one],
                           vc_ref.at[one], dm_ref, rd_ref, cd_ref, gain_ref, o_ref.at[one], u_ref, st_ref, a_ref)


def _ret_batch_element(q_ref, k_ref, v_ref, rg_ref, kc_ref, vc_ref, dm_ref, rd_ref, cd_ref, gain_ref,
                       o_ref, u_ref, st_ref, a_ref):
    c = RET_CHUNK
    hd = RET_HEAD_DIM
    n_chunks = q_ref.shape[1] // c
    n_ctx = kc_ref.shape[1] // c

    for h in range(N_RET_HEADS):
        cols = slice(h * hd, (h + 1) * hd)
        dm = dm_ref[h]
        q_dec = rd_ref[h, 0]
        k_dec = rd_ref[h, 1]
        c_dec_f = cd_ref[h, 0:1, 0:hd]
        c_dec_b = cd_ref[h, 0:1, hd:2 * hd]
        gain = gain_ref[:, cols]

        def both_decays(t, dec):
            return jnp.concatenate([t * dec[:, 0:hd], t * dec[:, hd:2 * hd]], axis=1)

        def kv_outer(k, v):
            return lax.dot_general(v, both_decays(k, k_dec), (((0,), (0,)), ((), ())), preferred_element_type=F32)

        for j in range(n_ctx):
            rows = slice(j * c, (j + 1) * c)
            u_ref[j] = kv_outer(kc_ref[0, rows, cols], vc_ref[0, rows, cols])
        for j in range(n_chunks):
            rows = slice(j * c, (j + 1) * c)
            u_ref[n_ctx + j] = kv_outer(k_ref[0, rows, cols], v_ref[0, rows, cols])

        sf = jnp.zeros((hd, hd), F32)
        for j in range(n_ctx + n_chunks - 1):
            sf = c_dec_f * sf + u_ref[j, :, 0:hd]
            if j + 1 >= n_ctx:
                st_ref[j + 1 - n_ctx, :, 0:hd] = sf.astype(BF16)
        sb = jnp.zeros((hd, hd), F32)
        for j in reversed(range(n_ctx)):
            sb = c_dec_b * sb + u_ref[j, :, hd:2 * hd]
        for j in reversed(range(n_chunks)):
            st_ref[j, :, hd:2 * hd] = sb.astype(BF16)
            if j > 0:
                sb = c_dec_b * sb + u_ref[n_ctx + j, :, hd:2 * hd]

        for j in range(n_chunks):
            rows = slice(j * c, (j + 1) * c)
            s = lax.dot_general(q_ref[0, rows, cols], k_ref[0, rows, cols], (((1,), (1,)), ((), ())),
                                preferred_element_type=F32)
            a_ref[j] = (s * dm).astype(BF16)

        for j in range(n_chunks):
            rows = slice(j * c, (j + 1) * c)
            o = jnp.dot(a_ref[j], v_ref[0, rows, cols], preferred_element_type=F32)
            o += lax.dot_general(both_decays(q_ref[0, rows, cols], q_dec), st_ref[j], (((1,), (1,)), ((), ())),
                                 preferred_element_type=F32)
            mu = jnp.mean(o, axis=-1, keepdims=True)
            d = o - mu
            var = jnp.mean(d * d, axis=-1, keepdims=True)
            on = d * lax.rsqrt(var + EPS)
            o_ref[0, rows, cols] = (on * gain * rg_ref[0, rows, cols].astype(F32)).astype(BF16)


def _retention(proj, proj_c, dm, rd, cd, gain):
    bsz, length, _ = proj.shape
    ctx_len = proj_c.shape[1]
    hd = RET_HEAD_DIM
    c = RET_CHUNK
    n_chunks = length // c
    n_ctx = ctx_len // c
    assert n_ctx >= 1
    rbt = RET_BATCH if bsz % RET_BATCH == 0 else 1

    def col(base):
        return lambda b: (b, 0, base // D_RET)

    return pl.pallas_call(
        _ret_kernel,
        grid=(bsz // rbt,),
        in_specs=[
            pl.BlockSpec((rbt, length, D_RET), col(COL_Q)),
            pl.BlockSpec((rbt, length, D_RET), col(COL_K)),
            pl.BlockSpec((rbt, length, D_RET), col(COL_V)),
            pl.BlockSpec((rbt, length, D_RET), col(COL_R_GATE)),
            pl.BlockSpec((rbt, ctx_len, D_RET), col(0)),
            pl.BlockSpec((rbt, ctx_len, D_RET), col(D_RET)),
            pl.BlockSpec((N_RET_HEADS, c, c), lambda b: (0, 0, 0)),
            pl.BlockSpec((N_RET_HEADS, 2, c, 2 * hd), lambda b: (0, 0, 0, 0)),
            pl.BlockSpec((N_RET_HEADS, SUBLANES, 2 * hd), lambda b: (0, 0, 0)),
            pl.BlockSpec((1, D_RET), lambda b: (0, 0)),
        ],
        out_specs=pl.BlockSpec((rbt, length, D_RET), lambda b: (b, 0, 0)),
        out_shape=jax.ShapeDtypeStruct((bsz, length, D_RET), BF16),
        scratch_shapes=[
            pltpu.VMEM((n_ctx + n_chunks, hd, 2 * hd), F32),
            pltpu.VMEM((n_chunks, hd, 2 * hd), BF16),
            pltpu.VMEM((n_chunks, c, c), BF16),
        ],
        compiler_params=pltpu.CompilerParams(
            dimension_semantics=("parallel",), vmem_limit_bytes=VMEM_LIMIT),
        name="retention",
    )(proj, proj, proj, proj, proj_c, proj_c, dm, rd, cd, gain)


def kernel(x, c, ctx, c_ctx, w_ada, b_ada, g_pre, g_post, w_in, w_fourier, decay_logit, ret_gn_gain, w_out):
    bsz, seq_len, _ = x.shape
    ctx_len = ctx.shape[1]
    assert w_ada.shape[0] == 1, "single layer only"

    mod, dm, rd, cd, ab = _adaln(c, c_ctx, w_ada[0], b_ada[0], g_pre[0], g_post[0], decay_logit[0], w_fourier[0])
    cos_t, sin_t = _rotary_tables(seq_len)

    gate_cols = list(range(COL_F_GATE, COL_Q, RET_HEAD_DIM)) + list(range(COL_R_GATE, D_IN, RET_HEAD_DIM))
    proj = _inproj(x, mod, w_in[0], cos_t, sin_t, mod_row=None, col_block=D_IN, blocks=(0,),
                   rot_lo=COL_Q, rot_hi=COL_V, silu_cols=gate_cols, tm=TOKEN_TILE)
    ctx_rows = bsz * ctx_len
    assert COL_V == COL_K + D_RET and COL_K % D_RET == 0
    proj_c = _inproj(ctx.reshape(1, ctx_rows, D_MODEL), mod, w_in[0], cos_t, sin_t, mod_row=bsz,
                     col_block=D_RET, blocks=(COL_K // D_RET, COL_V // D_RET),
                     rot_lo=0, rot_hi=0, silu_cols=(), tm=min(TOKEN_TILE, ctx_rows))
    proj_c = proj_c.reshape(bsz, ctx_len, 2 * D_RET)

    yr = _retention(proj, proj_c, dm, rd, cd, ret_gn_gain[0].reshape(1, D_RET))
    return _fourier_outproj(proj, yr, ab, *_seq_dft_matrices(seq_len), w_out[0], x, mod)
```

```python
import functools

import numpy as np
import jax
import jax.numpy as jnp
from jax import lax
from jax.experimental import pallas as pl
from jax.experimental.pallas import tpu as pltpu

D_MODEL = 1024
GRID_W = 64
D_FOURIER = 512
N_FOURIER_GROUPS = 4
FOURIER_GROUP = D_FOURIER // N_FOURIER_GROUPS
D_RET = 512
N_RET_HEADS = 4
RET_HEAD_DIM = D_RET // N_RET_HEADS
D_MIX = D_FOURIER + D_RET
D_IN = 2 * D_FOURIER + 4 * D_RET
RET_CHUNK = 128
ROPE_BASE = 10000.0
QK_SCALE = RET_HEAD_DIM ** -0.5
EPS = 1e-6

COL_F_IN = 0
COL_F_GATE = D_FOURIER
COL_Q = 2 * D_FOURIER
COL_K = COL_Q + D_RET
COL_V = COL_K + D_RET
COL_R_GATE = COL_V + D_RET

MXU_COLS = 256
FOLD_BLOCK = 256
FOURIER_ROW_BLOCK = 512
INPROJ_COL_CHUNK = 2 * MXU_COLS
OUTPROJ_ROW_BLOCK = 256
FUSED_STEPS = 2
SUBLANES = 8
BF16_TILE_ROWS = 16
MOD_SHIFT, MOD_MULT, MOD_GATE = 0, 1, 2
TOKEN_TILE = 1024
VMEM_LIMIT = 56 * 1024 * 1024

F32 = jnp.float32
BF16 = jnp.bfloat16


def _silu(v):
    return v * (0.5 * jnp.tanh(0.5 * v) + 0.5)


def _adaln_kernel(c_ref, cc_ref, w_ref, b_ref, gpre_ref, gpost_ref, dl_ref, wf_ref, cdft_ref, sdft_ref,
                  o_ref, dm_ref, rd_ref, cd_ref, ab_ref, s_ref):
    j = pl.program_id(0)
    pl.when(j == 0)(functools.partial(_tables_kernel, dl_ref, wf_ref, cdft_ref, sdft_ref,
                                      dm_ref, rd_ref, cd_ref, ab_ref))
    nb = c_ref.shape[0]
    s_ref[0:nb, :] = _silu(c_ref[...])
    s_ref[nb:, :] = jnp.broadcast_to(_silu(cc_ref[...]), (s_ref.shape[0] - nb, D_MODEL))
    m = jnp.dot(s_ref[...], w_ref[...], preferred_element_type=F32) + b_ref[...]
    add = jnp.where(j == 1, 1.0, 0.0)
    mul = jnp.where(j == 1, gpre_ref[...], jnp.where(j == 2, gpost_ref[...], 1.0))
    o_ref[...] = ((m + add) * mul)[:, None, :]


def _adaln(c, c_ctx, w_ada, b_ada, g_pre, g_post, decay_logit, w_fourier):
    c_chunk = RET_CHUNK
    idx = np.arange(FOURIER_GROUP)
    ang = 2.0 * np.pi * ((idx[:, None] * idx[None, :]) % FOURIER_GROUP) / FOURIER_GROUP
    cdft = jnp.asarray(np.cos(ang) / np.sqrt(FOURIER_GROUP), F32)
    sdft = jnp.asarray(np.sin(ang) / np.sqrt(FOURIER_GROUP), F32)
    table_shapes = (
        jax.ShapeDtypeStruct((N_RET_HEADS, c_chunk, c_chunk), F32),
        jax.ShapeDtypeStruct((N_RET_HEADS, 2, c_chunk, 2 * c_chunk), BF16),
        jax.ShapeDtypeStruct((N_RET_HEADS, SUBLANES, 2 * c_chunk), F32),
        jax.ShapeDtypeStruct((2, N_FOURIER_GROUPS // 2, 2 * FOURIER_GROUP, 2 * FOURIER_GROUP), BF16),
    )

    def whole(shape):
        return pl.BlockSpec(shape, lambda j: (0,) * len(shape))
    bsz = c.shape[0]
    n_out = w_ada.shape[1]
    bn = D_MODEL
    assert n_out == 3 * bn and bsz % SUBLANES == 0
    rows = bsz + SUBLANES
    const = lambda j: (0, 0)
    return pl.pallas_call(
        _adaln_kernel,
        grid=(n_out // bn,),
        in_specs=[
            pl.BlockSpec((bsz, D_MODEL), const),
            pl.BlockSpec((1, D_MODEL), const),
            pl.BlockSpec((D_MODEL, bn), lambda j: (0, j)),
            pl.BlockSpec((1, bn), lambda j: (0, j)),
            pl.BlockSpec((1, D_MODEL), const),
            pl.BlockSpec((1, D_MODEL), const),
            pl.BlockSpec(memory_space=pltpu.SMEM),
            whole(w_fourier.shape), whole(cdft.shape), whole(sdft.shape),
        ],
        out_specs=(pl.BlockSpec((rows, 1, bn), lambda j: (0, 0, j)),) + tuple(whole(t.shape) for t in table_shapes),
        out_shape=(jax.ShapeDtypeStruct((rows, 1, n_out), F32),) + table_shapes,
        scratch_shapes=[pltpu.VMEM((rows, D_MODEL), F32)],
        compiler_params=pltpu.CompilerParams(dimension_semantics=("arbitrary",)),
        name="adaln",
    )(c, c_ctx.reshape(1, D_MODEL), w_ada, b_ada.reshape(1, n_out), g_pre.reshape(1, D_MODEL),
      g_post.reshape(1, D_MODEL), decay_logit, w_fourier, cdft, sdft)


def _tables_kernel(dl_ref, wf_ref, cc_ref, sc_ref, dm_ref, rd_ref, cd_ref, ab_ref):
    c = RET_CHUNK
    n = lax.broadcasted_iota(jnp.int32, (c, c), 0).astype(F32)
    m = lax.broadcasted_iota(jnp.int32, (c, c), 1).astype(F32)
    diff = n - m

    def log_sigmoid(v):
        return jnp.minimum(v, 0.0) - jnp.log1p(jnp.exp(-jnp.abs(v)))

    for h in range(N_RET_HEADS):
        lg_f = log_sigmoid(jnp.full((c, c), dl_ref[0, h], F32))
        lg_b = log_sigmoid(jnp.full((c, c), dl_ref[1, h], F32))
        dm_ref[h] = QK_SCALE * jnp.where(diff >= 0, jnp.exp(lg_f * jnp.maximum(diff, 0.0)),
                                         jnp.exp(lg_b * jnp.maximum(-diff, 0.0)))
        rd_ref[h, 0, :, 0:c] = (QK_SCALE * jnp.exp(lg_f * (n + 1.0))).astype(BF16)
        rd_ref[h, 0, :, c:2 * c] = (QK_SCALE * jnp.exp(lg_b * (c - n))).astype(BF16)
        rd_ref[h, 1, :, 0:c] = jnp.exp(lg_f * (c - 1.0 - n)).astype(BF16)
        rd_ref[h, 1, :, c:2 * c] = jnp.exp(lg_b * n).astype(BF16)
        cd_ref[h, :, 0:c] = jnp.exp(lg_f[0:SUBLANES, :] * float(c))
        cd_ref[h, :, c:2 * c] = jnp.exp(lg_b[0:SUBLANES, :] * float(c))
    fg = FOURIER_GROUP
    ab_ref[...] = jnp.zeros(ab_ref.shape, BF16)
    for g in range(N_FOURIER_GROUPS):
        wf = wf_ref[g]
        a = jnp.dot(cc_ref[...], wf, preferred_element_type=F32, precision=lax.Precision.HIGHEST)
        b = jnp.dot(sc_ref[...], wf, preferred_element_type=F32, precision=lax.Precision.HIGHEST)
        pair, off = g // 2, (g % 2) * fg
        ab_ref[0, pair, off:off + fg, off:off + fg] = a.astype(BF16)
        ab_ref[1, pair, off:off + fg, off:off + fg] = b.astype(BF16)


def _inproj_kernel(x_ref, shift_ref, mult_ref, cos_ref, sin_ref, *refs, n_w, rot_lo, rot_hi, silu_cols):
    w_refs, o_ref, wb_ref = refs[:n_w], refs[n_w], refs[n_w + 1]
    n_out = wb_ref.shape[1]
    hd = RET_HEAD_DIM
    bn = INPROJ_COL_CHUNK

    @pl.when(jnp.logical_and(pl.program_id(0) == 0, pl.program_id(1) == 0))
    def _():
        for i, w_ref in enumerate(w_refs):
            width = w_ref.shape[1]
            for j in range(width // bn):
                wb_ref[:, i * width + j * bn:i * width + (j + 1) * bn] = w_ref[:, j * bn:(j + 1) * bn].astype(BF16)

    x = x_ref[0]
    ms = jnp.mean(x * x, axis=-1, keepdims=True)
    hb = (x * lax.rsqrt(ms + EPS) * mult_ref[0] + shift_ref[0]).astype(BF16)
    is_u1 = (lax.broadcasted_iota(jnp.int32, (x.shape[0], hd), 1) & (hd // 4)) == 0

    def epilogue_weight(j):
        lo = j * bn
        return (lo in silu_cols) + (rot_lo <= lo < rot_hi)

    for j in sorted(range(n_out // bn), key=epilogue_weight, reverse=True):
        p = jnp.dot(hb, wb_ref[:, j * bn:(j + 1) * bn], preferred_element_type=F32)
        for i in range(bn // hd):
            ph = p[:, i * hd:(i + 1) * hd]
            lo = j * bn + i * hd
            if lo in silu_cols:
                ph = _silu(ph)
            if rot_lo <= lo < rot_hi:
                partner = jnp.where(is_u1, pltpu.roll(ph, hd - hd // 4, axis=1), pltpu.roll(ph, hd // 4, axis=1))
                ph = ph * cos_ref[...] + partner * sin_ref[...]
            o_ref[0, :, lo:lo + hd] = ph.astype(BF16)


def _inproj(x, mod, w, cos_t, sin_t, *, mod_row, col_block, blocks, rot_lo, rot_hi, silu_cols, tm):
    bsz, length, _ = x.shape
    n_out = len(blocks) * col_block
    def mod_map(k):
        return (lambda b, t: (b, 0, k)) if mod_row is None else (lambda b, t: (mod_row, 0, k))
    rot_map = (lambda b, t: (t, 0)) if rot_hi > rot_lo else (lambda b, t: (0, 0))
    w_specs = [pl.BlockSpec((D_MODEL, col_block), functools.partial(lambda b, t, blk: (0, blk), blk=blk),
                            pipeline_mode=pl.Buffered(1)) for blk in blocks]
    return pl.pallas_call(
        functools.partial(_inproj_kernel, n_w=len(blocks), rot_lo=rot_lo, rot_hi=rot_hi,
                          silu_cols=frozenset(silu_cols)),
        grid=(bsz, length // tm),
        in_specs=[
            pl.BlockSpec((1, tm, D_MODEL), lambda b, t: (b, t, 0)),
            pl.BlockSpec((1, 1, D_MODEL), mod_map(MOD_SHIFT)),
            pl.BlockSpec((1, 1, D_MODEL), mod_map(MOD_MULT)),
            pl.BlockSpec((tm, RET_HEAD_DIM), rot_map),
            pl.BlockSpec((tm, RET_HEAD_DIM), rot_map),
        ] + w_specs,
        out_specs=pl.BlockSpec((1, tm, n_out), lambda b, t: (b, t, 0)),
        out_shape=jax.ShapeDtypeStruct((bsz, length, n_out), BF16),
        scratch_shapes=[pltpu.VMEM((D_MODEL, n_out), BF16)],
        compiler_params=pltpu.CompilerParams(
            dimension_semantics=("arbitrary", "arbitrary"), vmem_limit_bytes=VMEM_LIMIT),
        name="inproj",
    )(x, mod, mod, cos_t, sin_t, *([w] * len(blocks)))


def _rotary_tables(seq_len):
    n_freq = RET_HEAD_DIM // 4
    pos = np.arange(seq_len)
    row = (pos // GRID_W).astype(np.float64)
    col = (pos % GRID_W).astype(np.float64)
    inv_freq = ROPE_BASE ** (-np.arange(n_freq, dtype=np.float64) / n_freq)
    ang_r = row[:, None] * inv_freq[None, :]
    ang_c = col[:, None] * inv_freq[None, :]
    cos_t = np.concatenate([np.cos(ang_r), np.cos(ang_r), np.cos(ang_c), np.cos(ang_c)], axis=1)
    sin_t = np.concatenate([-np.sin(ang_r), np.sin(ang_r), -np.sin(ang_c), np.sin(ang_c)], axis=1)
    return jnp.asarray(cos_t, F32), jnp.asarray(sin_t, F32)


def _fourier_outproj_kernel(u_ref, g_ref, ab_ref, c_ref, s_ref, js_ref, yr_ref, w_ref, x_ref, gate_ref,
                            o_ref, ue_ref, uo_ref, pe_ref, qo_ref, gs_ref, mid_ref, yf_ref, wb_ref):
    b = pl.program_id(0)
    t = pl.program_id(1)
    n_batch = pl.num_programs(0) - 1
    slot = b % 2
    n = u_ref.shape[1]
    h = n // 2
    fb = FOLD_BLOCK
    nb = h // fb
    mb = FOURIER_ROW_BLOCK
    fg = FOURIER_GROUP
    inv_sqrt_n = 1.0 / np.sqrt(n)
    first_row = lax.broadcasted_iota(jnp.int32, (fb, D_FOURIER), 0) == 0

    @pl.when(jnp.logical_and(b == 0, t == 0))
    def _():
        wb_ref[...] = w_ref[...].astype(BF16)

    def channel_map(v, half):
        return jnp.concatenate(
            [jnp.dot(v[:, p * 2 * fg:(p + 1) * 2 * fg], ab_ref[half, p], preferred_element_type=F32)
             for p in range(N_FOURIER_GROUPS // 2)], axis=1)

    def fold_and_channel_map():
        js = js_ref[...]
        for i in range(nb):
            blk = slice(i * fb, (i + 1) * fb)
            lo = u_ref[0, blk, :].astype(F32)
            src = n - (i + 1) * fb
            r = jnp.dot(js, u_ref[0, src:src + fb, :], preferred_element_type=F32)
            if i > 0:
                edge = u_ref[0, n - i * fb:n - i * fb + BF16_TILE_ROWS, :][0:1, :]
                r = jnp.where(first_row, edge.astype(F32), r)
            ue_ref[blk, :] = (lo + r).astype(BF16)
            uo_ref[blk, :] = (lo - r).astype(BF16)
        for i in range(nb):
            blk = slice(i * fb, (i + 1) * fb)
            pe_ref[blk, :] = channel_map(ue_ref[blk, :], 0).astype(BF16)
            qo_ref[blk, :] = channel_map(uo_ref[blk, :], 1).astype(BF16)
        p_mid = channel_map(u_ref[0, h:h + BF16_TILE_ROWS, :], 0)[0:1, :] * inv_sqrt_n
        sign = (1 - 2 * (lax.broadcasted_iota(jnp.int32, (h, D_FOURIER), 0) & 1)).astype(F32)
        y_mid = jnp.sum(pe_ref[...].astype(F32) * sign, axis=0, keepdims=True) * inv_sqrt_n + p_mid
        mid_ref[0] = jnp.broadcast_to(p_mid, (SUBLANES, D_FOURIER))
        mid_ref[1] = jnp.broadcast_to(y_mid, (SUBLANES, D_FOURIER))

    def dft_block(i):
        blk = slice(i * mb, (i + 1) * mb)
        sign = (1 - 2 * (lax.broadcasted_iota(jnp.int32, (mb, D_FOURIER), 0) & 1)).astype(F32)
        yc = jnp.dot(c_ref[blk, :], pe_ref[...], preferred_element_type=F32) + sign * mid_ref[0, 0:1, :]
        ys = jnp.dot(s_ref[blk, :], qo_ref[...], preferred_element_type=F32)
        yf_ref[slot, blk, :] = ((yc - ys) * g_ref[0, blk, :].astype(F32)).astype(BF16)
        gs_ref[blk, :] = (yc + ys).astype(BF16)

    def reflect_upper_half():
        js = js_ref[...]
        for i in range(nb):
            src = h - (i + 1) * fb
            r = jnp.dot(js, gs_ref[src:src + fb, :], preferred_element_type=F32)
            if i == 0:
                edge = mid_ref[1, 0:1, :]
            else:
                edge = gs_ref[h - i * fb:h - i * fb + BF16_TILE_ROWS, :][0:1, :].astype(F32)
            r = jnp.where(first_row, edge, r)
            rows = slice(h + i * fb, h + (i + 1) * fb)
            yf_ref[slot, rows, :] = (r * g_ref[0, rows, :].astype(F32)).astype(BF16)

    phases = [fold_and_channel_map] + [functools.partial(dft_block, i) for i in range(h // mb)] + [reflect_upper_half]
    assert FUSED_STEPS == 2
    bounds = (0, 1, len(phases))

    def fourier_part(k):
        for phase in phases[bounds[k]:bounds[k + 1]]:
            phase()

    def outproj_tile():
        tm = x_ref.shape[1]
        rb = OUTPROJ_ROW_BLOCK
        for r in range(tm // rb):
            rows = slice(r * rb, (r + 1) * rb)
            yf_rows = pl.ds(pl.multiple_of(t * tm, tm) + r * rb, rb)
            y = jnp.dot(yf_ref[1 - slot, yf_rows, :], wb_ref[0:D_FOURIER, :], preferred_element_type=F32)
            y += jnp.dot(yr_ref[0, rows, :], wb_ref[D_FOURIER:D_MIX, :], preferred_element_type=F32)
            ms = jnp.mean(y * y, axis=-1, keepdims=True)
            o_ref[0, rows, :] = x_ref[0, rows, :] + y * lax.rsqrt(ms + EPS) * gate_ref[0]

    interior = jnp.logical_and(b > 0, b < n_batch)
    for k in range(FUSED_STEPS):
        def both(k=k):
            fourier_part(k)
            outproj_tile()
        pl.when(jnp.logical_and(interior, t == k))(both)
        pl.when(jnp.logical_and(b == 0, t == k))(functools.partial(fourier_part, k))
    pl.when(b == n_batch)(outproj_tile)


def _fourier_outproj(proj, yr, ab, c_mat, s_mat, js, w_out, x, mod):
    bsz, length, _ = x.shape
    h = length // 2
    assert h % FOLD_BLOCK == 0 and h % FOURIER_ROW_BLOCK == 0 and length % FUSED_STEPS == 0
    tm = length // FUSED_STEPS
    assert tm % OUTPROJ_ROW_BLOCK == 0

    def cur(col):
        return lambda b, t: (jnp.minimum(b, bsz - 1), 0, col)

    def prev_tile(b, t):
        return (jnp.maximum(b - 1, 0), jnp.where(b == 0, 0, t), 0)

    const2 = lambda b, t: (0, 0)
    resident = pl.Buffered(1)
    return pl.pallas_call(
        _fourier_outproj_kernel,
        grid=(bsz + 1, FUSED_STEPS),
        in_specs=[
            pl.BlockSpec((1, length, D_FOURIER), cur(COL_F_IN // D_FOURIER)),
            pl.BlockSpec((1, length, D_FOURIER), cur(COL_F_GATE // D_FOURIER)),
            pl.BlockSpec((2, N_FOURIER_GROUPS // 2, 2 * FOURIER_GROUP, 2 * FOURIER_GROUP), lambda b, t: (0, 0, 0, 0),
                         pipeline_mode=resident),
            pl.BlockSpec((h, h), const2, pipeline_mode=resident),
            pl.BlockSpec((h, h), const2, pipeline_mode=resident),
            pl.BlockSpec((FOLD_BLOCK, FOLD_BLOCK), const2, pipeline_mode=resident),
            pl.BlockSpec((1, tm, D_RET), prev_tile),
            pl.BlockSpec((D_MIX, D_MODEL), const2, pipeline_mode=resident),
            pl.BlockSpec((1, tm, D_MODEL), prev_tile),
            pl.BlockSpec((1, 1, D_MODEL), lambda b, t: (jnp.maximum(b - 1, 0), 0, MOD_GATE)),
        ],
        out_specs=pl.BlockSpec((1, tm, D_MODEL), prev_tile),
        out_shape=jax.ShapeDtypeStruct((bsz, length, D_MODEL), F32),
        scratch_shapes=[
            pltpu.VMEM((h, D_FOURIER), BF16),
            pltpu.VMEM((h, D_FOURIER), BF16),
            pltpu.VMEM((h, D_FOURIER), BF16),
            pltpu.VMEM((h, D_FOURIER), BF16),
            pltpu.VMEM((h, D_FOURIER), BF16),
            pltpu.VMEM((2, SUBLANES, D_FOURIER), F32),
            pltpu.VMEM((2, length, D_FOURIER), BF16),
            pltpu.VMEM((D_MIX, D_MODEL), BF16),
        ],
        compiler_params=pltpu.CompilerParams(
            dimension_semantics=("arbitrary", "arbitrary"), vmem_limit_bytes=VMEM_LIMIT),
        name="fourier_outproj",
    )(proj, proj, ab, c_mat, s_mat, js, yr, w_out, x, mod)


def _seq_dft_matrices(length):
    h = length // 2
    idx = np.arange(h)
    ang = 2.0 * np.pi * ((idx[:, None] * idx[None, :]) % length) / length
    scale = 1.0 / np.sqrt(length)
    c_mat = jnp.asarray((np.cos(ang) * scale).astype(np.float32)).astype(BF16)
    s_mat = jnp.asarray((np.sin(ang) * scale).astype(np.float32)).astype(BF16)
    r = np.arange(1, FOLD_BLOCK)
    js = np.zeros((FOLD_BLOCK, FOLD_BLOCK), np.float32)
    js[r, FOLD_BLOCK - r] = 1.0
    return c_mat, s_mat, jnp.asarray(js).astype(BF16)


def _ret_kernel(q_ref, k_ref, v_ref, rg_ref, kc_ref, vc_ref, dm_ref, rd_ref, cd_ref, gain_ref,
                o_ref, u_ref, st_ref, a_ref):
    c = RET_CHUNK
    hd = RET_HEAD_DIM
    n_chunks = q_ref.shape[1] // c
    n_ctx = kc_ref.shape[1] // c

    for h in range(N_RET_HEADS):
        cols = slice(h * hd, (h + 1) * hd)
        dm = dm_ref[h]
        q_dec = rd_ref[h, 0]
        k_dec = rd_ref[h, 1]
        c_dec_f = cd_ref[h, 0:1, 0:hd]
        c_dec_b = cd_ref[h, 0:1, hd:2 * hd]
        gain = gain_ref[:, cols]

        def both_decays(t, dec):
            return jnp.concatenate([t * dec[:, 0:hd], t * dec[:, hd:2 * hd]], axis=1)

        def kv_outer(k, v):
            return lax.dot_general(v, both_decays(k, k_dec), (((0,), (0,)), ((), ())), preferred_element_type=F32)

        for j in range(n_ctx):
            rows = slice(j * c, (j + 1) * c)
            u_ref[j] = kv_outer(kc_ref[0, rows, cols], vc_ref[0, rows, cols])
        for j in range(n_chunks):
            rows = slice(j * c, (j + 1) * c)
            u_ref[n_ctx + j] = kv_outer(k_ref[0, rows, cols], v_ref[0, rows, cols])

        sf = jnp.zeros((hd, hd), F32)
        for j in range(n_ctx + n_chunks - 1):
            sf = c_dec_f * sf + u_ref[j, :, 0:hd]
            if j + 1 >= n_ctx:
                st_ref[j + 1 - n_ctx, :, 0:hd] = sf.astype(BF16)
        sb = jnp.zeros((hd, hd), F32)
        for j in reversed(range(n_ctx)):
            sb = c_dec_b * sb + u_ref[j, :, hd:2 * hd]
        for j in reversed(range(n_chunks)):
            st_ref[j, :, hd:2 * hd] = sb.astype(BF16)
            if j > 0:
                sb = c_dec_b * sb + u_ref[n_ctx + j, :, hd:2 * hd]

        for j in range(n_chunks):
            rows = slice(j * c, (j + 1) * c)
            s = lax.dot_general(q_ref[0, rows, cols], k_ref[0, rows, cols], (((1,), (1,)), ((), ())),
                                preferred_element_type=F32)
            a_ref[j] = (s * dm).astype(BF16)

        for j in range(n_chunks):
            rows = slice(j * c, (j + 1) * c)
            o = jnp.dot(a_ref[j], v_ref[0, rows, cols], preferred_element_type=F32)
            o += lax.dot_general(both_decays(q_ref[0, rows, cols], q_dec), st_ref[j], (((1,), (1,)), ((), ())),
                                 preferred_element_type=F32)
            mu = jnp.mean(o, axis=-1, keepdims=True)
            d = o - mu
            var = jnp.mean(d * d, axis=-1, keepdims=True)
            on = d * lax.rsqrt(var + EPS)
            o_ref[0, rows, cols] = (on * gain * rg_ref[0, rows, cols].astype(F32)).astype(BF16)


def _retention(proj, proj_c, dm, rd, cd, gain):
    bsz, length, _ = proj.shape
    ctx_len = proj_c.shape[1]
    hd = RET_HEAD_DIM
    c = RET_CHUNK
    n_chunks = length // c
    n_ctx = ctx_len // c
    assert n_ctx >= 1

    def col(base):
        return lambda b: (b, 0, base // D_RET)

    return pl.pallas_call(
        _ret_kernel,
        grid=(bsz,),
        in_specs=[
            pl.BlockSpec((1, length, D_RET), col(COL_Q)),
            pl.BlockSpec((1, length, D_RET), col(COL_K)),
            pl.BlockSpec((1, length, D_RET), col(COL_V)),
            pl.BlockSpec((1, length, D_RET), col(COL_R_GATE)),
            pl.BlockSpec((1, ctx_len, D_RET), col(0)),
            pl.BlockSpec((1, ctx_len, D_RET), col(D_RET)),
            pl.BlockSpec((N_RET_HEADS, c, c), lambda b: (0, 0, 0)),
            pl.BlockSpec((N_RET_HEADS, 2, c, 2 * hd), lambda b: (0, 0, 0, 0)),
            pl.BlockSpec((N_RET_HEADS, SUBLANES, 2 * hd), lambda b: (0, 0, 0)),
            pl.BlockSpec((1, D_RET), lambda b: (0, 0)),
        ],
        out_specs=pl.BlockSpec((1, length, D_RET), lambda b: (b, 0, 0)),
        out_shape=jax.ShapeDtypeStruct((bsz, length, D_RET), BF16),
        scratch_shapes=[
            pltpu.VMEM((n_ctx + n_chunks, hd, 2 * hd), F32),
            pltpu.VMEM((n_chunks, hd, 2 * hd), BF16),
            pltpu.VMEM((n_chunks, c, c), BF16),
        ],
        compiler_params=pltpu.CompilerParams(
            dimension_semantics=("parallel",), vmem_limit_bytes=VMEM_LIMIT),
        name="retention",
    )(proj, proj, proj, proj, proj_c, proj_c, dm, rd, cd, gain)


def kernel(x, c, ctx, c_ctx, w_ada, b_ada, g_pre, g_post, w_in, w_fourier, decay_logit, ret_gn_gain, w_out):
    bsz, seq_len, _ = x.shape
    ctx_len = ctx.shape[1]
    assert w_ada.shape[0] == 1, "single layer only"

    mod, dm, rd, cd, ab = _adaln(c, c_ctx, w_ada[0], b_ada[0], g_pre[0], g_post[0], decay_logit[0], w_fourier[0])
    cos_t, sin_t = _rotary_tables(seq_len)

    gate_cols = list(range(COL_F_GATE, COL_Q, RET_HEAD_DIM)) + list(range(COL_R_GATE, D_IN, RET_HEAD_DIM))
    proj = _inproj(x, mod, w_in[0], cos_t, sin_t, mod_row=None, col_block=D_IN, blocks=(0,),
                   rot_lo=COL_Q, rot_hi=COL_V, silu_cols=gate_cols, tm=TOKEN_TILE)
    ctx_rows = bsz * ctx_len
    assert COL_V == COL_K + D_RET and COL_K % D_RET == 0
    proj_c = _inproj(ctx.reshape(1, ctx_rows, D_MODEL), mod, w_in[0], cos_t, sin_t, mod_row=bsz,
                     col_block=D_RET, blocks=(COL_K // D_RET, COL_V // D_RET),
                     rot_lo=0, rot_hi=0, silu_cols=(), tm=min(TOKEN_TILE // 2, ctx_rows))
    proj_c = proj_c.reshape(bsz, ctx_len, 2 * D_RET)

    yr = _retention(proj, proj_c, dm, rd, cd, ret_gn_gain[0].reshape(1, D_RET))
    return _fourier_outproj(proj, yr, ab, *_seq_dft_matrices(seq_len), w_out[0], x, mod)
```
